```python
import math
import jax, jax.numpy as jnp
from jax import lax
import numpy as np


D_MODEL = 1024
BATCH = 4
SEQ = 4096
DEPTH = 4
DEC_BATCH = 128
DEC_SEQ = 8
PAST_LEN = 8192
PAGE_SIZE = 128

N_A = DEPTH // 2
N_B = DEPTH - N_A
D_RNN = D_MODEL
N_LRU_BLOCKS = 4
LRU_BW = D_RNN // N_LRU_BLOCKS
CONV_W = 4
C_GATE = 8.0
N_HEADS = 16
N_KV_HEADS = 4
HEAD_DIM = 64
GROUP = N_HEADS // N_KV_HEADS
WINDOW = 128
ATTN_BLOCK = WINDOW
N_BUCKETS = 32
MAX_DISTANCE = 128
D_FF = 3 * D_MODEL
FFN_CONV_W = 3
EPS = 1e-6
NEG_INF = -1e30

kernel_name = "yoco_rglru_swa_sink_convffn_step"


def rmsnorm(x, g):
    xf = x.astype(jnp.float32)
    y = xf * lax.rsqrt(jnp.mean(xf * xf, axis=-1, keepdims=True) + EPS)
    return (y * g.astype(jnp.float32)).astype(x.dtype)


def causal_dwconv(x, buf, w, b):
    T = x.shape[1]
    W = w.shape[0]
    xp = jnp.concatenate([buf.astype(x.dtype), x], axis=1)
    y = b
    for k in range(W):
        y = y + w[k] * xp[:, k:k + T]
    return y, xp[:, T:]


def lru_scan(a, b, h0):
    def step(h, ab):
        a_t, b_t = ab
        h = a_t * h + b_t
        return h, h
    h_last, hs = lax.scan(step, h0, (jnp.swapaxes(a, 0, 1), jnp.swapaxes(b, 0, 1)))
    return jnp.swapaxes(hs, 0, 1), h_last


def rglru_block(xn, h0, conv_buf, w_in, conv_w, conv_b, gr_w, gr_b, gi_w, gi_b, lam, w_out):
    B, T, _ = xn.shape
    u = xn @ w_in
    gate, xb = u[..., :D_RNN], u[..., D_RNN:]
    xc, new_buf = causal_dwconv(xb, conv_buf, conv_w, conv_b)
    xblk = xc.reshape(B, T, N_LRU_BLOCKS, LRU_BW)
    r = jax.nn.sigmoid(jnp.einsum('btnc,ncd->btnd', xblk, gr_w).reshape(B, T, D_RNN) + gr_b)
    i = jax.nn.sigmoid(jnp.einsum('btnc,ncd->btnd', xblk, gi_w).reshape(B, T, D_RNN) + gi_b)
    log_a = -C_GATE * r.astype(jnp.float32) * jax.nn.softplus(-lam.astype(jnp.float32))
    a = jnp.exp(log_a)
    b = jnp.sqrt(-jnp.expm1(2.0 * log_a)) * (i * xc).astype(jnp.float32)
    hs, h_last = lru_scan(a, b, h0.astype(jnp.float32))
    y = (jax.nn.gelu(gate) * hs.astype(xn.dtype)) @ w_out
    return y, h_last.astype(h0.dtype), new_buf.astype(conv_buf.dtype)


def conv_ffn(xn, buf, w_up, conv_w, conv_b, w_down):
    u = xn @ w_up
    g, v = u[..., :D_FF], u[..., D_FF:]
    gc, new_buf = causal_dwconv(g, buf, conv_w, conv_b)
    return (jax.nn.gelu(gc) * v) @ w_down, new_buf.astype(buf.dtype)


def rel_bucket(dist):
    max_exact = N_BUCKETS // 2
    d = jnp.maximum(dist, 0)
    df = jnp.maximum(d, 1).astype(jnp.float32)
    large = max_exact + (jnp.log(df / max_exact) / math.log(MAX_DISTANCE / max_exact)
                         * (N_BUCKETS - max_exact)).astype(jnp.int32)
    large = jnp.minimum(large, N_BUCKETS - 1)
    return jnp.where(d < max_exact, d, large)


def rel_bias_heads(rel_bias, dist):
    b = rel_bias.astype(jnp.float32)[rel_bucket(dist)]
    Q, S = dist.shape
    return jnp.moveaxis(b, -1, 0).reshape(N_KV_HEADS, GROUP, Q, S)


def sink_softmax(scores, sinks):
    s = sinks.astype(jnp.float32)[:, :, None]
    m = jnp.maximum(jnp.max(scores, axis=-1), s)
    p = jnp.exp(scores - m[..., None])
    denom = jnp.sum(p, axis=-1) + jnp.exp(s - m)
    return p / denom[..., None]


def swa_prompt(q, k, v, sinks, rel_bias):
    B, S = q.shape[0], q.shape[1]
    nb = S // ATTN_BLOCK
    qb = q.reshape(B, nb, ATTN_BLOCK, N_KV_HEADS, GROUP, HEAD_DIM)
    kc = k.reshape(B, nb, ATTN_BLOCK, N_KV_HEADS, HEAD_DIM)
    vc = v.reshape(B, nb, ATTN_BLOCK, N_KV_HEADS, HEAD_DIM)
    kb = jnp.concatenate([jnp.concatenate([jnp.zeros_like(kc[:, :1]), kc[:, :-1]], axis=1), kc], axis=2)
    vb = jnp.concatenate([jnp.concatenate([jnp.zeros_like(vc[:, :1]), vc[:, :-1]], axis=1), vc], axis=2)
    scores = jnp.einsum('bnqkgd,bnskd->bnkgqs', qb, kb,
                        preferred_element_type=jnp.float32) * (HEAD_DIM ** -0.5)
    qi = jnp.arange(ATTN_BLOCK)[:, None]
    sj = jnp.arange(2 * ATTN_BLOCK)[None, :]
    dist = qi + ATTN_BLOCK - sj
    band = (dist >= 0) & (dist < WINDOW)
    has_prev = (jnp.arange(nb)[:, None, None] > 0) | (sj >= ATTN_BLOCK)[None]
    valid = band[None] & has_prev
    scores = jnp.where(valid[None, :, None, None], scores + rel_bias_heads(rel_bias, dist), NEG_INF)
    probs = sink_softmax(scores, sinks.reshape(N_KV_HEADS, GROUP))
    out = jnp.einsum('bnkgqs,bnskd->bnqkgd', probs.astype(v.dtype), vb)
    return out.reshape(B, S, N_HEADS * HEAD_DIM)


def swa_sample(q, k_all, v_all, sinks, rel_bias):
    DB, T = q.shape[0], q.shape[1]
    L = k_all.shape[1]
    W = L - T
    qg = q.reshape(DB, T, N_KV_HEADS, GROUP, HEAD_DIM)
    scores = jnp.einsum('btkgd,bskd->bkgts', qg, k_all,
                        preferred_element_type=jnp.float32) * (HEAD_DIM ** -0.5)
    dist = jnp.arange(T)[:, None] + W - jnp.arange(L)[None, :]
    valid = (dist >= 0) & (dist < WINDOW)
    scores = jnp.where(valid, scores + rel_bias_heads(rel_bias, dist), NEG_INF)
    probs = sink_softmax(scores, sinks.reshape(N_KV_HEADS, GROUP))
    out = jnp.einsum('bkgts,bskd->btkgd', probs.astype(v_all.dtype), v_all)
    return out.reshape(DB, T, N_HEADS * HEAD_DIM)


def shared_kv(x, kv_norm, w_kv, b_kv, k_norm):
    B, T, _ = x.shape
    kv = rmsnorm(x, kv_norm) @ w_kv + b_kv
    hkv = N_KV_HEADS * HEAD_DIM
    k = rmsnorm(kv[..., :hkv].reshape(B, T, N_KV_HEADS, HEAD_DIM), k_norm)
    v = kv[..., hkv:].reshape(B, T, N_KV_HEADS, HEAD_DIM)
    return k, v


def trunk(x, lru_h, lru_conv, ffn_conv, win_k, win_v, P, prompt):
    B, T, _ = x.shape
    new_h, new_c, new_f = [], [], []
    k = v = new_k = new_v = None
    for layer in range(DEPTH):
        if layer < N_A:
            i = layer
            y, h_i, c_i = rglru_block(rmsnorm(x, P['a_norm'][i]), lru_h[i], lru_conv[i],
                                      P['a_w_in'][i], P['a_conv_w'][i], P['a_conv_b'][i],
                                      P['a_gate_r_w'][i], P['a_gate_r_b'][i],
                                      P['a_gate_i_w'][i], P['a_gate_i_b'][i],
                                      P['a_lambda'][i], P['a_w_out'][i])
            new_h.append(h_i)
            new_c.append(c_i)
        else:
            j = layer - N_A
            q = rmsnorm(x, P['b_norm'][j]) @ P['w_q'][j] + P['b_q'][j]
            q = rmsnorm(q.reshape(B, T, N_HEADS, HEAD_DIM), P['q_norm'][j])
            if prompt:
                o = swa_prompt(q, k, v, P['sinks'][j], P['rel_bias'])
            else:
                o = swa_sample(q, k, v, P['sinks'][j], P['rel_bias'])
            y = o @ P['w_o'][j] + P['b_o'][j]
        x = x + y
        f, f_i = conv_ffn(rmsnorm(x, P['f_norm'][layer]), ffn_conv[layer], P['f_w_up'][layer],
                          P['f_conv_w'][layer], P['f_conv_b'][layer], P['f_w_down'][layer])
        new_f.append(f_i)
        x = x + f
        if layer == N_A - 1:
            k, v = shared_kv(x, P['kv_norm'], P['w_kv'], P['b_kv'], P['k_norm'])
            if not prompt:
                k = jnp.concatenate([win_k.astype(k.dtype), k], axis=1)
                v = jnp.concatenate([win_v.astype(v.dtype), v], axis=1)
            new_k = k[:, -WINDOW:]
            new_v = v[:, -WINDOW:]
    return x, jnp.stack(new_h), jnp.stack(new_c), jnp.stack(new_f), new_k, new_v


def setup_inputs(seed: int = 0) -> dict:
    key = jax.random.key(seed)
    ks = iter(jax.random.split(key, 48))

    def nrm(shape, scale):
        return scale * jax.random.normal(next(ks), shape, jnp.float32)

    def gain(shape):
        return 1.0 + nrm(shape, 0.05)

    HQ = N_HEADS * HEAD_DIM
    HKV = N_KV_HEADS * HEAD_DIM
    a_c = jax.random.uniform(next(ks), (N_A, D_RNN), jnp.float32, 0.9, 0.999)
    a_base = jnp.exp(jnp.log(a_c) / C_GATE)
    a_lambda = jnp.log(a_base) - jnp.log1p(-a_base)
    return {
        "x_prompt": nrm((BATCH, SEQ, D_MODEL), 1.0),
        "x_sample": nrm((DEC_BATCH, DEC_SEQ, D_MODEL), 1.0),
        "state_lru_h": nrm((N_A, DEC_BATCH, D_RNN), 0.5),
        "state_lru_conv": nrm((N_A, DEC_BATCH, CONV_W - 1, D_RNN), 0.5),
        "state_ffn_conv": nrm((DEPTH, DEC_BATCH, FFN_CONV_W - 1, D_FF), 0.5),
        "cache_k_win": nrm((DEC_BATCH, WINDOW, N_KV_HEADS, HEAD_DIM), 1.0),
        "cache_v_win": nrm((DEC_BATCH, WINDOW, N_KV_HEADS, HEAD_DIM), 1.0),
        "a_norm": gain((N_A, D_MODEL)),
        "a_w_in": nrm((N_A, D_MODEL, 2 * D_RNN), D_MODEL ** -0.5),
        "a_conv_w": nrm((N_A, CONV_W, D_RNN), CONV_W ** -0.5),
        "a_conv_b": nrm((N_A, D_RNN), 0.02),
        "a_gate_r_w": nrm((N_A, N_LRU_BLOCKS, LRU_BW, LRU_BW), LRU_BW ** -0.5),
        "a_gate_r_b": nrm((N_A, D_RNN), 0.02),
        "a_gate_i_w": nrm((N_A, N_LRU_BLOCKS, LRU_BW, LRU_BW), LRU_BW ** -0.5),
        "a_gate_i_b": nrm((N_A, D_RNN), 0.02),
        "a_lambda": a_lambda,
        "a_w_out": nrm((N_A, D_RNN, D_MODEL), D_RNN ** -0.5),
        "kv_norm": gain((D_MODEL,)),
        "w_kv": nrm((D_MODEL, 2 * HKV), D_MODEL ** -0.5),
        "b_kv": nrm((2 * HKV,), 0.02),
        "k_norm": gain((HEAD_DIM,)),
        "b_norm": gain((N_B, D_MODEL)),
        "w_q": nrm((N_B, D_MODEL, HQ), D_MODEL ** -0.5),
        "b_q": nrm((N_B, HQ), 0.02),
        "q_norm": gain((N_B, HEAD_DIM)),
        "sinks": nrm((N_B, N_HEADS), 0.5),
        "w_o": nrm((N_B, HQ, D_MODEL), HQ ** -0.5),
        "b_o": nrm((N_B, D_MODEL), 0.02),
        "rel_bias": nrm((N_BUCKETS, N_HEADS), 0.5),
        "f_norm": gain((DEPTH, D_MODEL)),
        "f_w_up": nrm((DEPTH, D_MODEL, 2 * D_FF), D_MODEL ** -0.5),
        "f_conv_w": nrm((DEPTH, FFN_CONV_W, D_FF), FFN_CONV_W ** -0.5),
        "f_conv_b": nrm((DEPTH, D_FF), 0.02),
        "f_w_down": nrm((DEPTH, D_FF, D_MODEL), D_FF ** -0.5),
    }


def reference(x_prompt, x_sample, state_lru_h, state_lru_conv, state_ffn_conv, cache_k_win, cache_v_win,
              a_norm, a_w_in, a_conv_w, a_conv_b, a_gate_r_w, a_gate_r_b, a_gate_i_w, a_gate_i_b,
              a_lambda, a_w_out, kv_norm, w_kv, b_kv, k_norm, b_norm, w_q, b_q, q_norm, sinks,
              w_o, b_o, rel_bias, f_norm, f_w_up, f_conv_w, f_conv_b, f_w_down):
    P = dict(a_norm=a_norm, a_w_in=a_w_in, a_conv_w=a_conv_w, a_conv_b=a_conv_b,
             a_gate_r_w=a_gate_r_w, a_gate_r_b=a_gate_r_b, a_gate_i_w=a_gate_i_w, a_gate_i_b=a_gate_i_b,
             a_lambda=a_lambda, a_w_out=a_w_out, kv_norm=kv_norm, w_kv=w_kv, b_kv=b_kv, k_norm=k_norm,
             b_norm=b_norm, w_q=w_q, b_q=b_q, q_norm=q_norm, sinks=sinks, w_o=w_o, b_o=b_o,
             rel_bias=rel_bias, f_norm=f_norm, f_w_up=f_w_up, f_conv_w=f_conv_w, f_conv_b=f_conv_b,
             f_w_down=f_w_down)
    B = x_prompt.shape[0]
    dt = x_prompt.dtype
    zero_h = jnp.zeros((N_A, B, D_RNN), dt)
    zero_c = jnp.zeros((N_A, B, CONV_W - 1, D_RNN), dt)
    zero_f = jnp.zeros((DEPTH, B, FFN_CONV_W - 1, D_FF), dt)
    y_prompt, h_p, c_p, f_p, k_p, v_p = trunk(x_prompt, zero_h, zero_c, zero_f, None, None, P, True)
    y_sample, h_s, c_s, f_s, k_s, v_s = trunk(x_sample, state_lru_h, state_lru_conv, state_ffn_conv,
                                              cache_k_win, cache_v_win, P, False)
    return (y_prompt, y_sample, h_p, c_p, f_p, k_p, v_p, h_s, c_s, f_s, k_s, v_s)
```

```python
import functools
import math

import numpy as np
import jax
import jax.numpy as jnp
from jax import lax
from jax.experimental import pallas as pl
from jax.experimental.pallas import tpu as pltpu

D_MODEL = 1024
DEPTH = 4
N_A = DEPTH // 2
D_RNN = D_MODEL
N_LRU_BLOCKS = 4
LRU_BW = D_RNN // N_LRU_BLOCKS
CONV_W = 4
C_GATE = 8.0
N_HEADS = 16
N_KV_HEADS = 4
HEAD_DIM = 64
GROUP = N_HEADS // N_KV_HEADS
HQ = N_HEADS * HEAD_DIM
HKV = N_KV_HEADS * HEAD_DIM
WINDOW = 128
N_BUCKETS = 32
MAX_DISTANCE = 128
D_FF = 3 * D_MODEL
FFN_CONV_W = 3
EPS = 1e-6
NEG_INF = -1e30

F32 = jnp.float32
BF16 = jnp.bfloat16

SUBLANES = 8
FF_CHUNK = 512
VMEM_LIMIT_BYTES = 56 * 1024 * 1024


def _rmsnorm(x, g):
    return x * lax.rsqrt(jnp.mean(x * x, axis=-1, keepdims=True) + EPS) * g


def _softplus(x):
    return jnp.maximum(x, 0.0) + jnp.log(1.0 + jnp.exp(-jnp.abs(x)))


def _mm(a, b):
    return jnp.dot(a.astype(BF16), b, preferred_element_type=F32)


def _const_spec(shape):
    nd = len(shape)
    return pl.BlockSpec(shape, lambda b, j: (0,) * nd, pipeline_mode=pl.Buffered(1))


def _params():
    return pltpu.CompilerParams(dimension_semantics=("arbitrary", "arbitrary"),
                                vmem_limit_bytes=VMEM_LIMIT_BYTES)


def _lru_kernel(x_ref, h0_ref, cbuf_ref, norm_ref, win_ref, cw_ref, cb_ref, grw_ref, grb_ref,
                giw_ref, gib_ref, lam_ref, wout_ref,
                y_ref, hlast_ref, cnew_ref,
                ext_ref, h_ref, a_ref, b_ref, *, nb, tt):
    m = nb * tt
    tail = SUBLANES - (CONV_W - 1)

    @pl.when(pl.program_id(1) == 0)
    def _():
        ext_ref[:, tail:SUBLANES, :] = cbuf_ref[...]
        h_ref[...] = h0_ref[...]

    x = x_ref[...].reshape(m, D_MODEL)
    u = _mm(_rmsnorm(x, norm_ref[...]), win_ref[...])
    gate = u[:, :D_RNN]
    ext_ref[:, SUBLANES:, :] = u[:, D_RNN:].reshape(nb, tt, D_RNN)

    xc = cb_ref[...].reshape(1, 1, D_RNN)
    for k in range(CONV_W):
        xc = xc + cw_ref[k:k + 1, :].reshape(1, 1, D_RNN) * ext_ref[:, tail + k:tail + k + tt, :]
    cnew = ext_ref[:, tail + tt:SUBLANES + tt, :]
    ext_ref[:, tail:SUBLANES, :] = cnew
    cnew_ref[...] = cnew
    xc = xc.reshape(m, D_RNN)

    xcb = xc.astype(BF16)
    rs, is_ = [], []
    for n in range(N_LRU_BLOCKS):
        blk = xcb[:, n * LRU_BW:(n + 1) * LRU_BW]
        rs.append(jnp.dot(blk, grw_ref[n], preferred_element_type=F32))
        is_.append(jnp.dot(blk, giw_ref[n], preferred_element_type=F32))
    r = jax.nn.sigmoid(jnp.concatenate(rs, axis=-1) + grb_ref[...])
    i = jax.nn.sigmoid(jnp.concatenate(is_, axis=-1) + gib_ref[...])
    log_a = (-C_GATE) * r * _softplus(-lam_ref[...])
    a = jnp.exp(log_a)
    b = jnp.sqrt(1.0 - a * a) * (i * xc)

    if tt == SUBLANES:
        hs = _scan_within_vregs(a.reshape(nb, tt, D_RNN), b.reshape(nb, tt, D_RNN), h_ref[...])
        h_last = hs[:, tt - 1:tt, :]
        h_ref[...] = h_last
        hlast_ref[...] = h_last
        hs = hs.reshape(m, D_RNN)
    else:
        a_ref[...] = a
        b_ref[...] = b
        _scan_rows(a_ref, b_ref, h_ref, nb, tt)
        hlast_ref[...] = h_ref[...]
        hs = b_ref[...]

    y = _mm(jax.nn.gelu(gate) * hs, wout_ref[...])
    y_ref[...] = (x + y).reshape(nb, tt, D_MODEL)


def _scan_within_vregs(a, b, h0):
    t = lax.broadcasted_iota(jnp.int32, a.shape, 1)
    d = 1
    while d < a.shape[1]:
        keep = t >= d
        b = jnp.where(keep, a * pltpu.roll(b, d, axis=1) + b, b)
        a = jnp.where(keep, a * pltpu.roll(a, d, axis=1), a)
        d *= 2
    return a * h0 + b


def _scan_rows(a_ref, b_ref, h_ref, nb, tt):
    def step(t, hs):
        out = []
        for n in range(nb):
            row = pl.ds(n * tt + t, 1)
            h = a_ref[row, :] * hs[n] + b_ref[row, :]
            b_ref[row, :] = h
            out.append(h)
        return tuple(out)

    hs = lax.fori_loop(0, tt, step, tuple(h_ref[n] for n in range(nb)), unroll=8)
    for n in range(nb):
        h_ref[n] = hs[n]


def _lru_block(x, h0, cbuf, norm, w_in, cw, cb, grw, grb, giw, gib, lam, w_out, *, nb, tt):
    B, T, _ = x.shape
    kern = functools.partial(_lru_kernel, nb=nb, tt=tt)
    row = lambda v: v.reshape(1, -1)
    return pl.pallas_call(
        kern,
        grid=(B // nb, T // tt),
        in_specs=[
            pl.BlockSpec((nb, tt, D_MODEL), lambda b, j: (b, j, 0)),
            pl.BlockSpec((nb, 1, D_RNN), lambda b, j: (b, 0, 0)),
            pl.BlockSpec((nb, CONV_W - 1, D_RNN), lambda b, j: (b, 0, 0)),
            _const_spec((1, D_MODEL)),
            _const_spec((D_MODEL, 2 * D_RNN)),
            _const_spec((CONV_W, D_RNN)),
            _const_spec((1, D_RNN)),
            _const_spec((N_LRU_BLOCKS, LRU_BW, LRU_BW)),
            _const_spec((1, D_RNN)),
            _const_spec((N_LRU_BLOCKS, LRU_BW, LRU_BW)),
            _const_spec((1, D_RNN)),
            _const_spec((1, D_RNN)),
            _const_spec((D_RNN, D_MODEL)),
        ],
        out_specs=[
            pl.BlockSpec((nb, tt, D_MODEL), lambda b, j: (b, j, 0)),
            pl.BlockSpec((nb, 1, D_RNN), lambda b, j: (b, 0, 0)),
            pl.BlockSpec((nb, CONV_W - 1, D_RNN), lambda b, j: (b, 0, 0)),
        ],
        out_shape=[
            jax.ShapeDtypeStruct((B, T, D_MODEL), F32),
            jax.ShapeDtypeStruct((B, 1, D_RNN), F32),
            jax.ShapeDtypeStruct((B, CONV_W - 1, D_RNN), F32),
        ],
        scratch_shapes=[
            pltpu.VMEM((nb, SUBLANES + tt, D_RNN), F32),
            pltpu.VMEM((nb, 1, D_RNN), F32),
            pltpu.VMEM((nb * tt, D_RNN), F32),
            pltpu.VMEM((nb * tt, D_RNN), F32),
        ],
        compiler_params=_params(),
        name="rglru_block",
    )(x, h0.reshape(B, 1, D_RNN), cbuf, row(norm), w_in, cw, row(cb), grw, row(grb), giw, row(gib), row(lam), w_out)


def _ffn_kernel(x_ref, buf_ref, norm_ref, wup_ref, cw_ref, cb_ref, wdown_ref,
                y_ref, bnew_ref,
                tail_ref, ext_ref, mid_ref, *, nb, tt):
    m = nb * tt
    tail = SUBLANES - (FFN_CONV_W - 1)

    @pl.when(pl.program_id(1) == 0)
    def _():
        tail_ref[...] = buf_ref[...]

    x = x_ref[...].reshape(m, D_MODEL)
    xn = _rmsnorm(x, norm_ref[...]).astype(BF16)
    for c in range(D_FF // FF_CHUNK):
        cs = slice(c * FF_CHUNK, (c + 1) * FF_CHUNK)
        g = jnp.dot(xn, wup_ref[:, cs], preferred_element_type=F32)
        v = jnp.dot(xn, wup_ref[:, D_FF + c * FF_CHUNK:D_FF + (c + 1) * FF_CHUNK],
                    preferred_element_type=F32)
        ext_ref[:, tail:SUBLANES, :] = tail_ref[:, :, cs]
        ext_ref[:, SUBLANES:, :] = g.reshape(nb, tt, FF_CHUNK)
        gc = cb_ref[:, cs].reshape(1, 1, FF_CHUNK)
        for k in range(FFN_CONV_W):
            gc = gc + cw_ref[k:k + 1, cs].reshape(1, 1, FF_CHUNK) * ext_ref[:, tail + k:tail + k + tt, :]
        tail_ref[:, :, cs] = ext_ref[:, tail + tt:SUBLANES + tt, :]
        mid_ref[:, cs] = (jax.nn.gelu(gc).reshape(m, FF_CHUNK) * v).astype(BF16)
    bnew_ref[...] = tail_ref[...]
    y = jnp.dot(mid_ref[...], wdown_ref[...], preferred_element_type=F32)
    y_ref[...] = (x + y).reshape(nb, tt, D_MODEL)


def _conv_ffn(x, buf, norm, w_up, cw, cb, w_down, *, nb, tt):
    B, T, _ = x.shape
    kern = functools.partial(_ffn_kernel, nb=nb, tt=tt)
    return pl.pallas_call(
        kern,
        grid=(B // nb, T // tt),
        in_specs=[
            pl.BlockSpec((nb, tt, D_MODEL), lambda b, j: (b, j, 0)),
            pl.BlockSpec((nb, FFN_CONV_W - 1, D_FF), lambda b, j: (b, 0, 0)),
            _const_spec((1, D_MODEL)),
            _const_spec((D_MODEL, 2 * D_FF)),
            _const_spec((FFN_CONV_W, D_FF)),
            _const_spec((1, D_FF)),
            _const_spec((D_FF, D_MODEL)),
        ],
        out_specs=[
            pl.BlockSpec((nb, tt, D_MODEL), lambda b, j: (b, j, 0)),
            pl.BlockSpec((nb, FFN_CONV_W - 1, D_FF), lambda b, j: (b, 0, 0)),
        ],
        out_shape=[
            jax.ShapeDtypeStruct((B, T, D_MODEL), F32),
            jax.ShapeDtypeStruct((B, FFN_CONV_W - 1, D_FF), F32),
        ],
        scratch_shapes=[
            pltpu.VMEM((nb, FFN_CONV_W - 1, D_FF), F32),
            pltpu.VMEM((nb, SUBLANES + tt, FF_CHUNK), F32),
            pltpu.VMEM((nb * tt, D_FF), BF16),
        ],
        compiler_params=_params(),
        name="conv_ffn",
    )(x, buf, norm.reshape(1, -1), w_up, cw, cb.reshape(1, -1), w_down)


def _kv_kernel(x_ref, norm_ref, w_ref, b_ref, kn_ref, k_ref, v_ref, *, nb, tt):
    m = nb * tt
    x = x_ref[...].reshape(m, D_MODEL)
    kv = _mm(_rmsnorm(x, norm_ref[...]), w_ref[...]) + b_ref[...]
    k = kv[:, :HKV]
    rows = lax.broadcasted_iota(jnp.int32, (HKV, HKV), 0) // HEAD_DIM
    cols = lax.broadcasted_iota(jnp.int32, (HKV, HKV), 1) // HEAD_DIM
    avg = jnp.where(rows == cols, 1.0 / HEAD_DIM, 0.0).astype(BF16)
    sq = k * k
    hi = sq.astype(BF16)
    lo = (sq - hi.astype(F32)).astype(BF16)
    ms = (jnp.dot(hi, avg, preferred_element_type=F32)
          + jnp.dot(lo, avg, preferred_element_type=F32))
    k_ref[...] = (k * lax.rsqrt(ms + EPS) * kn_ref[...]).reshape(nb, tt, HKV)
    v_ref[...] = kv[:, HKV:].reshape(nb, tt, HKV)


def _shared_kv(x, norm, w_kv, b_kv, k_norm, *, nb, tt):
    B, T, _ = x.shape
    kern = functools.partial(_kv_kernel, nb=nb, tt=tt)
    return pl.pallas_call(
        kern,
        grid=(B // nb, T // tt),
        in_specs=[
            pl.BlockSpec((nb, tt, D_MODEL), lambda b, j: (b, j, 0)),
            _const_spec((1, D_MODEL)),
            _const_spec((D_MODEL, 2 * HKV)),
            _const_spec((1, 2 * HKV)),
            _const_spec((1, HKV)),
        ],
        out_specs=[
            pl.BlockSpec((nb, tt, HKV), lambda b, j: (b, j, 0)),
            pl.BlockSpec((nb, tt, HKV), lambda b, j: (b, j, 0)),
        ],
        out_shape=[
            jax.ShapeDtypeStruct((B, T, HKV), F32),
            jax.ShapeDtypeStruct((B, T, HKV), F32),
        ],
        compiler_params=_params(),
        name="shared_kv",
    )(x, norm.reshape(1, -1), w_kv, b_kv.reshape(1, -1), jnp.tile(k_norm, N_KV_HEADS).reshape(1, -1))


def _rel_buckets(tt):
    qi = np.arange(tt)[:, None]
    sj = np.arange(WINDOW + tt)[None, :]
    dist = qi + WINDOW - sj
    max_exact = N_BUCKETS // 2
    d = np.maximum(dist, 0)
    df = np.maximum(d, 1).astype(np.float32)
    large = max_exact + (np.log(df / max_exact) / math.log(MAX_DISTANCE / max_exact)
                         * (N_BUCKETS - max_exact)).astype(np.int32)
    large = np.minimum(large, N_BUCKETS - 1)
    bucket = np.where(d < max_exact, d, large)
    valid = (dist >= 0) & (dist < WINDOW)
    return np.where(valid, bucket, -1).astype(np.int32)


def _bias_kernel(rb_ref, bkt_ref, out_ref):
    bkt = bkt_ref[...]
    in_window = lax.broadcasted_iota(jnp.int32, bkt.shape, 1) < WINDOW

    def head(h, carry):
        acc = jnp.full(bkt.shape, NEG_INF, F32)
        for c in range(N_BUCKETS):
            acc = jnp.where(bkt == c, rb_ref[c, h], acc)
        out_ref[0, h] = acc
        out_ref[1, h] = jnp.where(in_window, NEG_INF, acc)
        return carry

    lax.fori_loop(0, N_HEADS, head, 0)


def _rel_bias_table(rel_bias, tt):
    L = WINDOW + tt
    return pl.pallas_call(
        _bias_kernel,
        in_specs=[
            pl.BlockSpec(memory_space=pltpu.SMEM),
            pl.BlockSpec(memory_space=pltpu.VMEM),
        ],
        out_specs=pl.BlockSpec(memory_space=pltpu.VMEM),
        out_shape=jax.ShapeDtypeStruct((2, N_HEADS, tt, L), F32),
        name="rel_bias_table",
    )(rel_bias, jnp.asarray(_rel_buckets(tt)))


def _attn_kernel(sinks_ref, x_ref, kp_ref, kc_ref, vp_ref, vc_ref, bias_ref, norm_ref, wq_ref,
                 bq_ref, qn_ref, wo_ref, bo_ref, y_ref, q_ref, o_ref, *, nb, tt):
    m = nb * tt
    L = WINDOW + tt
    x = x_ref[...].reshape(m, D_MODEL)
    q_ref[...] = _mm(_rmsnorm(x, norm_ref[...]), wq_ref[...]) + bq_ref[...]
    qgain = qn_ref[...] * (HEAD_DIM ** -0.5)

    def seq(n, carry):
        rows = pl.ds(pl.multiple_of(n * tt, tt), tt)
        for g in range(N_KV_HEADS):
            gs = slice(g * HEAD_DIM, (g + 1) * HEAD_DIM)
            kk = jnp.concatenate([kp_ref[n, :, gs], kc_ref[n, :, gs]], axis=0).astype(BF16)
            vv = jnp.concatenate([vp_ref[n, :, gs], vc_ref[n, :, gs]], axis=0).astype(BF16)
            qs, sk = [], []
            for hh in range(GROUP):
                h = g * GROUP + hh
                qh = q_ref[rows, h * HEAD_DIM:(h + 1) * HEAD_DIM]
                qh = qh * lax.rsqrt(jnp.mean(qh * qh, axis=-1, keepdims=True) + EPS) * qgain
                qs.append(qh)
                sk.append(jnp.full((tt, 1), sinks_ref[h], F32))
            q4 = jnp.concatenate(qs, axis=0).astype(BF16)
            sink = jnp.concatenate(sk, axis=0)
            s = lax.dot_general(q4, kk, (((1,), (1,)), ((), ())), preferred_element_type=F32)
            s = s + bias_ref[0, g * GROUP:(g + 1) * GROUP].reshape(GROUP * tt, L)
            mx = jnp.maximum(jnp.max(s, axis=-1, keepdims=True), sink)
            p = jnp.exp(s - mx)
            den = jnp.sum(p, axis=-1, keepdims=True) + jnp.exp(sink - mx)
            o = jnp.dot(p.astype(BF16), vv, preferred_element_type=F32) / den
            for hh in range(GROUP):
                h = g * GROUP + hh
                o_ref[rows, h * HEAD_DIM:(h + 1) * HEAD_DIM] = o[hh * tt:(hh + 1) * tt]
        return carry

    lax.fori_loop(0, nb, seq, 0)
    y = _mm(o_ref[...], wo_ref[...]) + bo_ref[...]
    y_ref[...] = (x + y).reshape(nb, tt, D_MODEL)


def _attn_block(x, k_prev, k_cur, v_prev, v_cur, bias, sinks, norm, w_q, b_q, q_norm, w_o, b_o,
                *, nb, tt, first_tile_has_no_window):
    B, T, _ = x.shape
    L = WINDOW + tt
    kern = functools.partial(_attn_kernel, nb=nb, tt=tt)
    if first_tile_has_no_window:
        assert tt == WINDOW
        prev_map = lambda b, j, s: (b, jnp.maximum(j - 1, 0), 0)
        bias_map = lambda b, j, s: (jnp.where(j == 0, 1, 0), 0, 0, 0)
    else:
        assert T == tt
        prev_map = lambda b, j, s: (b, 0, 0)
        bias_map = lambda b, j, s: (0, 0, 0, 0)
    cur_map = lambda b, j, s: (b, j, 0)

    def const(shape):
        nd = len(shape)
        return pl.BlockSpec(shape, lambda b, j, s: (0,) * nd, pipeline_mode=pl.Buffered(1))

    grid_spec = pltpu.PrefetchScalarGridSpec(
        num_scalar_prefetch=1,
        grid=(B // nb, T // tt),
        in_specs=[
            pl.BlockSpec((nb, tt, D_MODEL), cur_map),
            pl.BlockSpec((nb, WINDOW, HKV), prev_map),
            pl.BlockSpec((nb, tt, HKV), cur_map),
            pl.BlockSpec((nb, WINDOW, HKV), prev_map),
            pl.BlockSpec((nb, tt, HKV), cur_map),
            pl.BlockSpec((1, N_HEADS, tt, L), bias_map),
            const((1, D_MODEL)),
            const((D_MODEL, HQ)),
            const((1, HQ)),
            const((1, HEAD_DIM)),
            const((HQ, D_MODEL)),
            const((1, D_MODEL)),
        ],
        out_specs=pl.BlockSpec((nb, tt, D_MODEL), cur_map),
        scratch_shapes=[
            pltpu.VMEM((nb * tt, HQ), F32),
            pltpu.VMEM((nb * tt, HQ), F32),
        ],
    )
    return pl.pallas_call(
        kern,
        grid_spec=grid_spec,
        out_shape=jax.ShapeDtypeStruct((B, T, D_MODEL), F32),
        compiler_params=_params(),
        name="swa_block",
    )(sinks, x, k_prev, k_cur, v_prev, v_cur, bias, norm.reshape(1, -1), w_q, b_q.reshape(1, -1),
      q_norm.reshape(1, -1), w_o, b_o.reshape(1, -1))


def _trunk(x, lru_h, lru_conv, ffn_conv, win_k, win_v, P, bias, *, nb, tt, prompt):
    B, T, _ = x.shape
    new_h, new_c, new_f = [], [], []
    k = v = None
    for layer in range(DEPTH):
        if layer < N_A:
            i = layer
            x, h_i, c_i = _lru_block(x, lru_h[i], lru_conv[i], P['a_norm'][i], P['a_w_in'][i],
                                     P['a_conv_w'][i], P['a_conv_b'][i], P['a_gate_r_w'][i],
                                     P['a_gate_r_b'][i], P['a_gate_i_w'][i], P['a_gate_i_b'][i],
                                     P['a_lambda'][i], P['a_w_out'][i], nb=nb, tt=tt)
            new_h.append(h_i.reshape(B, D_RNN))
            new_c.append(c_i)
        else:
            j = layer - N_A
            if prompt:
                k_prev, v_prev = k, v
            else:
                k_prev, v_prev = win_k, win_v
            x = _attn_block(x, k_prev, k, v_prev, v, bias, P['sinks'][j], P['b_norm'][j],
                            P['w_q'][j], P['b_q'][j], P['q_norm'][j], P['w_o'][j], P['b_o'][j],
                            nb=nb, tt=tt, first_tile_has_no_window=prompt)
        x, f_i = _conv_ffn(x, ffn_conv[layer], P['f_norm'][layer], P['f_w_up'][layer],
                           P['f_conv_w'][layer], P['f_conv_b'][layer], P['f_w_down'][layer],
                           nb=nb, tt=tt)
        new_f.append(f_i)
        if layer == N_A - 1:
            k, v = _shared_kv(x, P['kv_norm'], P['w_kv'], P['b_kv'], P['k_norm'], nb=nb, tt=tt)
    if prompt:
        new_k, new_v = k[:, -WINDOW:], v[:, -WINDOW:]
    else:
        new_k = jnp.concatenate([win_k, k], axis=1)[:, -WINDOW:]
        new_v = jnp.concatenate([win_v, v], axis=1)[:, -WINDOW:]
    shape4 = (B, WINDOW, N_KV_HEADS, HEAD_DIM)
    return (x, jnp.stack(new_h), jnp.stack(new_c), jnp.stack(new_f),
            new_k.reshape(shape4), new_v.reshape(shape4))


def kernel(x_prompt, x_sample, state_lru_h, state_lru_conv, state_ffn_conv, cache_k_win, cache_v_win,
           a_norm, a_w_in, a_conv_w, a_conv_b, a_gate_r_w, a_gate_r_b, a_gate_i_w, a_gate_i_b,
           a_lambda, a_w_out, kv_norm, w_kv, b_kv, k_norm, b_norm, w_q, b_q, q_norm, sinks,
           w_o, b_o, rel_bias, f_norm, f_w_up, f_conv_w, f_conv_b, f_w_down):
    bf = lambda w: w.astype(BF16)
    P = dict(a_norm=a_norm, a_w_in=bf(a_w_in), a_conv_w=a_conv_w, a_conv_b=a_conv_b,
             a_gate_r_w=bf(a_gate_r_w), a_gate_r_b=a_gate_r_b, a_gate_i_w=bf(a_gate_i_w),
             a_gate_i_b=a_gate_i_b, a_lambda=a_lambda, a_w_out=bf(a_w_out), kv_norm=kv_norm,
             w_kv=bf(w_kv), b_kv=b_kv, k_norm=k_norm, b_norm=b_norm, w_q=bf(w_q), b_q=b_q,
             q_norm=q_norm, sinks=sinks, w_o=bf(w_o), b_o=b_o, f_norm=f_norm, f_w_up=bf(f_w_up),
             f_conv_w=f_conv_w, f_conv_b=f_conv_b, f_w_down=bf(f_w_down))
    B, S, _ = x_prompt.shape
    DB, DT, _ = x_sample.shape

    zero_h = jnp.zeros((N_A, B, D_RNN), F32)
    zero_c = jnp.zeros((N_A, B, CONV_W - 1, D_RNN), F32)
    zero_f = jnp.zeros((DEPTH, B, FFN_CONV_W - 1, D_FF), F32)
    out_p = _trunk(x_prompt, zero_h, zero_c, zero_f, None, None, P,
                   _rel_bias_table(rel_bias, WINDOW), nb=B, tt=WINDOW, prompt=True)

    win_k = cache_k_win.reshape(DB, WINDOW, HKV)
    win_v = cache_v_win.reshape(DB, WINDOW, HKV)
    out_s = _trunk(x_sample, state_lru_h, state_lru_conv, state_ffn_conv, win_k, win_v, P,
                   _rel_bias_table(rel_bias, DT), nb=DB // 2, tt=DT, prompt=False)
    return (out_p[0], out_s[0]) + out_p[1:] + out_s[1:]
```

```python
import functools
import math

import numpy as np
import jax
import jax.numpy as jnp
from jax import lax
from jax.experimental import pallas as pl
from jax.experimental.pallas import tpu as pltpu

D_MODEL = 1024
DEPTH = 4
N_A = DEPTH // 2
D_RNN = D_MODEL
N_LRU_BLOCKS = 4
LRU_BW = D_RNN // N_LRU_BLOCKS
CONV_W = 4
C_GATE = 8.0
N_HEADS = 16
N_KV_HEADS = 4
HEAD_DIM = 64
GROUP = N_HEADS // N_KV_HEADS
HQ = N_HEADS * HEAD_DIM
HKV = N_KV_HEADS * HEAD_DIM
WINDOW = 128
N_BUCKETS = 32
MAX_DISTANCE = 128
D_FF = 3 * D_MODEL
FFN_CONV_W = 3
EPS = 1e-6
NEG_INF = -1e30

F32 = jnp.float32
BF16 = jnp.bfloat16

SUBLANES = 8
FF_CHUNK = 512
SEQS_PER_STEP = 8
VMEM_LIMIT_BYTES = 56 * 1024 * 1024


def _rmsnorm(x, g):
    return x * lax.rsqrt(jnp.mean(x * x, axis=-1, keepdims=True) + EPS) * g


def _softplus(x):
    return jnp.maximum(x, 0.0) + jnp.log(1.0 + jnp.exp(-jnp.abs(x)))


def _mm(a, b):
    return jnp.dot(a.astype(BF16), b, preferred_element_type=F32)


def _const_spec(shape):
    nd = len(shape)
    return pl.BlockSpec(shape, lambda b, j: (0,) * nd, pipeline_mode=pl.Buffered(1))


def _params():
    return pltpu.CompilerParams(dimension_semantics=("arbitrary", "arbitrary"),
                                vmem_limit_bytes=VMEM_LIMIT_BYTES)


def _lru_kernel(x_ref, h0_ref, cbuf_ref, norm_ref, win_ref, cw_ref, cb_ref, grw_ref, grb_ref,
                giw_ref, gib_ref, lam_ref, wout_ref,
                y_ref, hlast_ref, cnew_ref,
                ext_ref, h_ref, a_ref, b_ref, *, nb, tt):
    m = nb * tt
    tail = SUBLANES - (CONV_W - 1)

    @pl.when(pl.program_id(1) == 0)
    def _():
        ext_ref[:, tail:SUBLANES, :] = cbuf_ref[...]
        h_ref[...] = h0_ref[...]

    x = x_ref[...].reshape(m, D_MODEL)
    u = _mm(_rmsnorm(x, norm_ref[...]), win_ref[...])
    gate = u[:, :D_RNN]
    ext_ref[:, SUBLANES:, :] = u[:, D_RNN:].reshape(nb, tt, D_RNN)

    xc = cb_ref[...].reshape(1, 1, D_RNN)
    for k in range(CONV_W):
        xc = xc + cw_ref[k:k + 1, :].reshape(1, 1, D_RNN) * ext_ref[:, tail + k:tail + k + tt, :]
    cnew = ext_ref[:, tail + tt:SUBLANES + tt, :]
    ext_ref[:, tail:SUBLANES, :] = cnew
    cnew_ref[...] = cnew
    xc = xc.reshape(m, D_RNN)

    xcb = xc.astype(BF16)
    rs, is_ = [], []
    for n in range(N_LRU_BLOCKS):
        blk = xcb[:, n * LRU_BW:(n + 1) * LRU_BW]
        rs.append(jnp.dot(blk, grw_ref[n], preferred_element_type=F32))
        is_.append(jnp.dot(blk, giw_ref[n], preferred_element_type=F32))
    r = jax.nn.sigmoid(jnp.concatenate(rs, axis=-1) + grb_ref[...])
    i = jax.nn.sigmoid(jnp.concatenate(is_, axis=-1) + gib_ref[...])
    log_a = (-C_GATE) * r * _softplus(-lam_ref[...])
    a = jnp.exp(log_a)
    b = jnp.sqrt(1.0 - a * a) * (i * xc)

    if tt == SUBLANES:
        hs = _scan_within_vregs(a.reshape(nb, tt, D_RNN), b.reshape(nb, tt, D_RNN), h_ref[...])
        h_last = hs[:, tt - 1:tt, :]
        h_ref[...] = h_last
        hlast_ref[...] = h_last
        hs = hs.reshape(m, D_RNN)
    else:
        a_ref[...] = a
        b_ref[...] = b
        _scan_rows(a_ref, b_ref, h_ref, nb, tt)
        hlast_ref[...] = h_ref[...]
        hs = b_ref[...]

    y = _mm(jax.nn.gelu(gate) * hs, wout_ref[...])
    y_ref[...] = (x + y).reshape(nb, tt, D_MODEL)


def _scan_within_vregs(a, b, h0):
    t = lax.broadcasted_iota(jnp.int32, a.shape, 1)
    d = 1
    while d < a.shape[1]:
        keep = t >= d
        b = jnp.where(keep, a * pltpu.roll(b, d, axis=1) + b, b)
        a = jnp.where(keep, a * pltpu.roll(a, d, axis=1), a)
        d *= 2
    return a * h0 + b


def _scan_rows(a_ref, b_ref, h_ref, nb, tt):
    def step(t, hs):
        out = []
        for n in range(nb):
            row = pl.ds(n * tt + t, 1)
            h = a_ref[row, :] * hs[n] + b_ref[row, :]
            b_ref[row, :] = h
            out.append(h)
        return tuple(out)

    hs = lax.fori_loop(0, tt, step, tuple(h_ref[n] for n in range(nb)), unroll=8)
    for n in range(nb):
        h_ref[n] = hs[n]


def _lru_block(x, h0, cbuf, norm, w_in, cw, cb, grw, grb, giw, gib, lam, w_out, *, nb, tt):
    B, T, _ = x.shape
    kern = functools.partial(_lru_kernel, nb=nb, tt=tt)
    row = lambda v: v.reshape(1, -1)
    return pl.pallas_call(
        kern,
        grid=(B // nb, T // tt),
        in_specs=[
            pl.BlockSpec((nb, tt, D_MODEL), lambda b, j: (b, j, 0)),
            pl.BlockSpec((nb, 1, D_RNN), lambda b, j: (b, 0, 0)),
            pl.BlockSpec((nb, CONV_W - 1, D_RNN), lambda b, j: (b, 0, 0)),
            _const_spec((1, D_MODEL)),
            _const_spec((D_MODEL, 2 * D_RNN)),
            _const_spec((CONV_W, D_RNN)),
            _const_spec((1, D_RNN)),
            _const_spec((N_LRU_BLOCKS, LRU_BW, LRU_BW)),
            _const_spec((1, D_RNN)),
            _const_spec((N_LRU_BLOCKS, LRU_BW, LRU_BW)),
            _const_spec((1, D_RNN)),
            _const_spec((1, D_RNN)),
            _const_spec((D_RNN, D_MODEL)),
        ],
        out_specs=[
            pl.BlockSpec((nb, tt, D_MODEL), lambda b, j: (b, j, 0)),
            pl.BlockSpec((nb, 1, D_RNN), lambda b, j: (b, 0, 0)),
            pl.BlockSpec((nb, CONV_W - 1, D_RNN), lambda b, j: (b, 0, 0)),
        ],
        out_shape=[
            jax.ShapeDtypeStruct((B, T, D_MODEL), F32),
            jax.ShapeDtypeStruct((B, 1, D_RNN), F32),
            jax.ShapeDtypeStruct((B, CONV_W - 1, D_RNN), F32),
        ],
        scratch_shapes=[
            pltpu.VMEM((nb, SUBLANES + tt, D_RNN), F32),
            pltpu.VMEM((nb, 1, D_RNN), F32),
            pltpu.VMEM((nb * tt, D_RNN), F32),
            pltpu.VMEM((nb * tt, D_RNN), F32),
        ],
        compiler_params=_params(),
        name="rglru_block",
    )(x, h0.reshape(B, 1, D_RNN), cbuf, row(norm), w_in, cw, row(cb), grw, row(grb), giw, row(gib), row(lam), w_out)


def _ffn_kernel(x_ref, buf_ref, norm_ref, wup_ref, cw_ref, cb_ref, wdown_ref,
                y_ref, bnew_ref,
                tail_ref, ext_ref, mid_ref, *, nb, tt):
    m = nb * tt
    tail = SUBLANES - (FFN_CONV_W - 1)

    @pl.when(pl.program_id(1) == 0)
    def _():
        tail_ref[...] = buf_ref[...]

    x = x_ref[...].reshape(m, D_MODEL)
    xn = _rmsnorm(x, norm_ref[...]).astype(BF16)
    for c in range(D_FF // FF_CHUNK):
        cs = slice(c * FF_CHUNK, (c + 1) * FF_CHUNK)
        g = jnp.dot(xn, wup_ref[:, cs], preferred_element_type=F32)
        v = jnp.dot(xn, wup_ref[:, D_FF + c * FF_CHUNK:D_FF + (c + 1) * FF_CHUNK],
                    preferred_element_type=F32)
        ext_ref[:, tail:SUBLANES, :] = tail_ref[:, :, cs]
        ext_ref[:, SUBLANES:, :] = g.reshape(nb, tt, FF_CHUNK)
        gc = cb_ref[:, cs].reshape(1, 1, FF_CHUNK)
        for k in range(FFN_CONV_W):
            gc = gc + cw_ref[k:k + 1, cs].reshape(1, 1, FF_CHUNK) * ext_ref[:, tail + k:tail + k + tt, :]
        tail_ref[:, :, cs] = ext_ref[:, tail + tt:SUBLANES + tt, :]
        mid_ref[:, cs] = (jax.nn.gelu(gc).reshape(m, FF_CHUNK) * v).astype(BF16)
    bnew_ref[...] = tail_ref[...]
    y = jnp.dot(mid_ref[...], wdown_ref[...], preferred_element_type=F32)
    y_ref[...] = (x + y).reshape(nb, tt, D_MODEL)


def _conv_ffn(x, buf, norm, w_up, cw, cb, w_down, *, nb, tt):
    B, T, _ = x.shape
    kern = functools.partial(_ffn_kernel, nb=nb, tt=tt)
    return pl.pallas_call(
        kern,
        grid=(B // nb, T // tt),
        in_specs=[
            pl.BlockSpec((nb, tt, D_MODEL), lambda b, j: (b, j, 0)),
            pl.BlockSpec((nb, FFN_CONV_W - 1, D_FF), lambda b, j: (b, 0, 0)),
            _const_spec((1, D_MODEL)),
            _const_spec((D_MODEL, 2 * D_FF)),
            _const_spec((FFN_CONV_W, D_FF)),
            _const_spec((1, D_FF)),
            _const_spec((D_FF, D_MODEL)),
        ],
        out_specs=[
            pl.BlockSpec((nb, tt, D_MODEL), lambda b, j: (b, j, 0)),
            pl.BlockSpec((nb, FFN_CONV_W - 1, D_FF), lambda b, j: (b, 0, 0)),
        ],
        out_shape=[
            jax.ShapeDtypeStruct((B, T, D_MODEL), F32),
            jax.ShapeDtypeStruct((B, FFN_CONV_W - 1, D_FF), F32),
        ],
        scratch_shapes=[
            pltpu.VMEM((nb, FFN_CONV_W - 1, D_FF), F32),
            pltpu.VMEM((nb, SUBLANES + tt, FF_CHUNK), F32),
            pltpu.VMEM((nb * tt, D_FF), BF16),
        ],
        compiler_params=_params(),
        name="conv_ffn",
    )(x, buf, norm.reshape(1, -1), w_up, cw, cb.reshape(1, -1), w_down)


def _kv_kernel(x_ref, norm_ref, w_ref, b_ref, kn_ref, *rest, nb, tt, key_major_copies):
    m = nb * tt
    x = x_ref[...].reshape(m, D_MODEL)
    xn = _rmsnorm(x, norm_ref[...]).astype(BF16)
    kv = jnp.dot(xn, w_ref[...], preferred_element_type=F32) + b_ref[...]
    k = kv[:, :HKV]
    if key_major_copies:
        wvt_ref, bvt_ref, k_ref, v_ref, kb_ref, vt_ref = rest
        vt = lax.dot_general(wvt_ref[...], xn, (((1,), (1,)), ((), ())),
                             preferred_element_type=F32) + bvt_ref[...]
        for n in range(nb):
            vt_ref[n] = vt[:, n * tt:(n + 1) * tt].astype(BF16)
    else:
        k_ref, v_ref = rest
    rows = lax.broadcasted_iota(jnp.int32, (HKV, HKV), 0) // HEAD_DIM
    cols = lax.broadcasted_iota(jnp.int32, (HKV, HKV), 1) // HEAD_DIM
    avg = jnp.where(rows == cols, 1.0 / HEAD_DIM, 0.0).astype(BF16)
    sq = k * k
    hi = sq.astype(BF16)
    lo = (sq - hi.astype(F32)).astype(BF16)
    ms = (jnp.dot(hi, avg, preferred_element_type=F32)
          + jnp.dot(lo, avg, preferred_element_type=F32))
    kn = (k * lax.rsqrt(ms + EPS) * kn_ref[...]).reshape(nb, tt, HKV)
    k_ref[...] = kn
    v_ref[...] = kv[:, HKV:].reshape(nb, tt, HKV)
    if key_major_copies:
        kb_ref[...] = kn.astype(BF16)


def _shared_kv(x, norm, w_kv, b_kv, k_norm, *, nb, tt, key_major_copies):
    B, T, _ = x.shape
    kern = functools.partial(_kv_kernel, nb=nb, tt=tt, key_major_copies=key_major_copies)
    tok_spec = pl.BlockSpec((nb, tt, HKV), lambda b, j: (b, j, 0))
    in_specs = [
        pl.BlockSpec((nb, tt, D_MODEL), lambda b, j: (b, j, 0)),
        _const_spec((1, D_MODEL)),
        _const_spec((D_MODEL, 2 * HKV)),
        _const_spec((1, 2 * HKV)),
        _const_spec((1, HKV)),
    ]
    args = [x, norm.reshape(1, -1), w_kv, b_kv.reshape(1, -1),
            jnp.tile(k_norm, N_KV_HEADS).reshape(1, -1)]
    out_specs = [tok_spec, tok_spec]
    out_shape = [jax.ShapeDtypeStruct((B, T, HKV), F32), jax.ShapeDtypeStruct((B, T, HKV), F32)]
    if key_major_copies:
        in_specs += [_const_spec((HKV, D_MODEL)), _const_spec((HKV, 1))]
        args += [w_kv[:, HKV:].T, b_kv[HKV:].reshape(-1, 1)]
        out_specs += [tok_spec, pl.BlockSpec((nb, HKV, tt), lambda b, j: (b, 0, j))]
        out_shape += [jax.ShapeDtypeStruct((B, T, HKV), BF16),
                      jax.ShapeDtypeStruct((B, HKV, T), BF16)]
    return pl.pallas_call(
        kern,
        grid=(B // nb, T // tt),
        in_specs=in_specs,
        out_specs=out_specs,
        out_shape=out_shape,
        compiler_params=_params(),
        name="shared_kv",
    )(*args)


def _rel_buckets(tt):
    qi = np.arange(tt)[:, None]
    sj = np.arange(WINDOW + tt)[None, :]
    dist = qi + WINDOW - sj
    max_exact = N_BUCKETS // 2
    d = np.maximum(dist, 0)
    df = np.maximum(d, 1).astype(np.float32)
    large = max_exact + (np.log(df / max_exact) / math.log(MAX_DISTANCE / max_exact)
                         * (N_BUCKETS - max_exact)).astype(np.int32)
    large = np.minimum(large, N_BUCKETS - 1)
    bucket = np.where(d < max_exact, d, large)
    valid = (dist >= 0) & (dist < WINDOW)
    return np.where(valid, bucket, -1).astype(np.int32)


def _bias_of_buckets(rb_ref, bkt, h):
    acc = jnp.full(bkt.shape, NEG_INF, F32)
    for c in range(N_BUCKETS):
        acc = jnp.where(bkt == c, rb_ref[c, h], acc)
    return acc


def _bias_kernel(rb_ref, bkt_ref, out_ref):
    bkt = bkt_ref[...]

    def head(h, carry):
        out_ref[h] = _bias_of_buckets(rb_ref, bkt, h)
        return carry

    lax.fori_loop(0, N_HEADS, head, 0)


def _bias_t_kernel(rb_ref, bkt_ref, out_ref, *, tt):
    bkt = bkt_ref[...]
    before_tile = lax.broadcasted_iota(jnp.int32, bkt.shape, 0) < WINDOW

    def group(g, carry):
        for hh in range(GROUP):
            acc = _bias_of_buckets(rb_ref, bkt, g * GROUP + hh)
            out_ref[0, g, :, hh * tt:(hh + 1) * tt] = acc
            out_ref[1, g, :, hh * tt:(hh + 1) * tt] = jnp.where(before_tile, NEG_INF, acc)
        return carry

    lax.fori_loop(0, N_KV_HEADS, group, 0)


def _rel_bias_table(rel_bias, tt, *, keys_on_rows):
    L = WINDOW + tt
    bkt = _rel_buckets(tt)
    if keys_on_rows:
        kern = functools.partial(_bias_t_kernel, tt=tt)
        bkt = bkt.T
        out_shape = jax.ShapeDtypeStruct((2, N_KV_HEADS, L, GROUP * tt), F32)
    else:
        kern = _bias_kernel
        out_shape = jax.ShapeDtypeStruct((N_HEADS, tt, L), F32)
    return pl.pallas_call(
        kern,
        in_specs=[
            pl.BlockSpec(memory_space=pltpu.SMEM),
            pl.BlockSpec(memory_space=pltpu.VMEM),
        ],
        out_specs=pl.BlockSpec(memory_space=pltpu.VMEM),
        out_shape=out_shape,
        name="rel_bias_table",
    )(rel_bias, jnp.asarray(bkt))


def _attn_sample_kernel(sinks_ref, x_ref, kp_ref, kc_ref, vp_ref, vc_ref, bias_ref, norm_ref, wq_ref,
                        bq_ref, qn_ref, wo_ref, bo_ref, y_ref, q_ref, o_ref, *, nb, tt):
    m = nb * tt
    L = WINDOW + tt
    sb = SEQS_PER_STEP
    x = x_ref[...].reshape(m, D_MODEL)
    q = _mm(_rmsnorm(x, norm_ref[...]), wq_ref[...]) + bq_ref[...]
    qgain = qn_ref[...] * (HEAD_DIM ** -0.5)
    for h in range(N_HEADS):
        hs = slice(h * HEAD_DIM, (h + 1) * HEAD_DIM)
        qh = q[:, hs]
        q_ref[:, hs] = qh * lax.rsqrt(jnp.mean(qh * qh, axis=-1, keepdims=True) + EPS) * qgain
    sinks = [jnp.concatenate([jnp.full((tt, 1), sinks_ref[g * GROUP + hh], F32)
                              for hh in range(GROUP)], axis=0)[None] for g in range(N_KV_HEADS)]

    def seqs(i, carry):
        n0 = pl.multiple_of(i * sb, sb)
        rows = pl.ds(pl.multiple_of(i * (sb * tt), sb * tt), sb * tt)
        qblk = q_ref[rows, :]
        kall = jnp.concatenate([kp_ref[pl.ds(n0, sb)], kc_ref[pl.ds(n0, sb)]], axis=1)
        vall = jnp.concatenate([vp_ref[pl.ds(n0, sb)], vc_ref[pl.ds(n0, sb)]], axis=1)
        scores = []
        for g in range(N_KV_HEADS):
            gs = slice(g * HEAD_DIM, (g + 1) * HEAD_DIM)
            q4 = jnp.concatenate(
                [qblk[:, h * HEAD_DIM:(h + 1) * HEAD_DIM].reshape(sb, tt, HEAD_DIM)
                 for h in range(g * GROUP, (g + 1) * GROUP)], axis=1).astype(BF16)
            s = jnp.einsum('bqd,bkd->bqk', q4, kall[:, :, gs].astype(BF16),
                           preferred_element_type=F32)
            scores.append(s + bias_ref[g * GROUP:(g + 1) * GROUP].reshape(1, GROUP * tt, L))
        probs, dens = [], []
        for g in range(N_KV_HEADS):
            s = scores[g]
            mx = jnp.maximum(jnp.max(s, axis=-1, keepdims=True), sinks[g])
            p = jnp.exp(s - mx)
            dens.append(jnp.sum(p, axis=-1, keepdims=True) + jnp.exp(sinks[g] - mx))
            probs.append(p.astype(BF16))
        for g in range(N_KV_HEADS):
            gs = slice(g * HEAD_DIM, (g + 1) * HEAD_DIM)
            o = jnp.einsum('bqk,bkd->bqd', probs[g], vall[:, :, gs].astype(BF16),
                           preferred_element_type=F32) / dens[g]
            for hh in range(GROUP):
                h = g * GROUP + hh
                o_ref[rows, h * HEAD_DIM:(h + 1) * HEAD_DIM] = (
                    o[:, hh * tt:(hh + 1) * tt, :].reshape(sb * tt, HEAD_DIM))
        return carry

    lax.fori_loop(0, nb // sb, seqs, 0)
    y = _mm(o_ref[...], wo_ref[...]) + bo_ref[...]
    y_ref[...] = (x + y).reshape(nb, tt, D_MODEL)


def _attn_prompt_kernel(sinks_ref, x_ref, kp_ref, kc_ref, vtp_ref, vtc_ref, bias_ref, norm_ref,
                        wqt_ref, bq_ref, qn_ref, wo_ref, bo_ref, y_ref, qt_ref, ot_ref, *, nb, tt):
    m = nb * tt
    x = x_ref[...].reshape(m, D_MODEL)
    xn = _rmsnorm(x, norm_ref[...]).astype(BF16)
    qt = lax.dot_general(wqt_ref[...], xn, (((1,), (1,)), ((), ())),
                         preferred_element_type=F32) + bq_ref[...]
    qgain = qn_ref[...] * (HEAD_DIM ** -0.5)
    for h in range(N_HEADS):
        hs = slice(h * HEAD_DIM, (h + 1) * HEAD_DIM)
        qh = qt[hs, :]
        ms = jnp.mean(qh * qh, axis=0, keepdims=True)
        qt_ref[hs, :] = (qh * lax.rsqrt(ms + EPS) * qgain).astype(BF16)

    head_of_lane = lax.broadcasted_iota(jnp.int32, (1, GROUP * tt), 1) // tt
    for g in range(N_KV_HEADS):
        gs = slice(g * HEAD_DIM, (g + 1) * HEAD_DIM)
        sink = jnp.full((1, GROUP * tt), sinks_ref[g * GROUP], F32)
        for hh in range(1, GROUP):
            sink = jnp.where(head_of_lane == hh, sinks_ref[g * GROUP + hh], sink)
        bias = bias_ref[0, g]
        for n in range(nb):
            ts = slice(n * tt, (n + 1) * tt)
            kk = jnp.concatenate([kp_ref[n, :, gs], kc_ref[n, :, gs]], axis=0)
            q4 = jnp.concatenate([qt_ref[h * HEAD_DIM:(h + 1) * HEAD_DIM, ts]
                                  for h in range(g * GROUP, (g + 1) * GROUP)], axis=1)
            s = jnp.dot(kk, q4, preferred_element_type=F32) + bias
            mx = jnp.maximum(jnp.max(s, axis=0, keepdims=True), sink)
            p = jnp.exp(s - mx)
            den = jnp.sum(p, axis=0, keepdims=True) + jnp.exp(sink - mx)
            vt = jnp.concatenate([vtp_ref[n, gs, :], vtc_ref[n, gs, :]], axis=1)
            o = jnp.dot(vt, p.astype(BF16), preferred_element_type=F32) / den
            for hh in range(GROUP):
                h = g * GROUP + hh
                ot_ref[h * HEAD_DIM:(h + 1) * HEAD_DIM, ts] = o[:, hh * tt:(hh + 1) * tt]
    y = _mm(ot_ref[...].T, wo_ref[...]) + bo_ref[...]
    y_ref[...] = (x + y).reshape(nb, tt, D_MODEL)


def _const_spec3(shape):
    nd = len(shape)
    return pl.BlockSpec(shape, lambda b, j, s: (0,) * nd, pipeline_mode=pl.Buffered(1))


def _attn_sample_block(x, k_win, k_new, v_win, v_new, bias, sinks, norm, w_q, b_q, q_norm, w_o, b_o,
                       *, nb):
    B, tt, _ = x.shape
    L = WINDOW + tt
    assert nb % SEQS_PER_STEP == 0 and tt % SUBLANES == 0
    kern = functools.partial(_attn_sample_kernel, nb=nb, tt=tt)
    seq_map = lambda b, j, s: (b, 0, 0)
    grid_spec = pltpu.PrefetchScalarGridSpec(
        num_scalar_prefetch=1,
        grid=(B // nb, 1),
        in_specs=[
            pl.BlockSpec((nb, tt, D_MODEL), seq_map),
            pl.BlockSpec((nb, WINDOW, HKV), seq_map),
            pl.BlockSpec((nb, tt, HKV), seq_map),
            pl.BlockSpec((nb, WINDOW, HKV), seq_map),
            pl.BlockSpec((nb, tt, HKV), seq_map),
            _const_spec3((N_HEADS, tt, L)),
            _const_spec3((1, D_MODEL)),
            _const_spec3((D_MODEL, HQ)),
            _const_spec3((1, HQ)),
            _const_spec3((1, HEAD_DIM)),
            _const_spec3((HQ, D_MODEL)),
            _const_spec3((1, D_MODEL)),
        ],
        out_specs=pl.BlockSpec((nb, tt, D_MODEL), seq_map),
        scratch_shapes=[
            pltpu.VMEM((nb * tt, HQ), F32),
            pltpu.VMEM((nb * tt, HQ), F32),
        ],
    )
    return pl.pallas_call(
        kern,
        grid_spec=grid_spec,
        out_shape=jax.ShapeDtypeStruct((B, tt, D_MODEL), F32),
        compiler_params=_params(),
        name="swa_sample",
    )(sinks, x, k_win, k_new, v_win, v_new, bias, norm.reshape(1, -1), w_q, b_q.reshape(1, -1),
      q_norm.reshape(1, -1), w_o, b_o.reshape(1, -1))


def _attn_prompt_block(x, kb, vt, bias_t, sinks, norm, w_q, b_q, q_norm, w_o, b_o, *, nb):
    B, T, _ = x.shape
    tt = WINDOW
    L = WINDOW + tt
    kern = functools.partial(_attn_prompt_kernel, nb=nb, tt=tt)
    cur_map = lambda b, j, s: (b, j, 0)
    prev_map = lambda b, j, s: (b, jnp.maximum(j - 1, 0), 0)
    grid_spec = pltpu.PrefetchScalarGridSpec(
        num_scalar_prefetch=1,
        grid=(B // nb, T // tt),
        in_specs=[
            pl.BlockSpec((nb, tt, D_MODEL), cur_map),
            pl.BlockSpec((nb, tt, HKV), prev_map),
            pl.BlockSpec((nb, tt, HKV), cur_map),
            pl.BlockSpec((nb, HKV, tt), lambda b, j, s: (b, 0, jnp.maximum(j - 1, 0))),
            pl.BlockSpec((nb, HKV, tt), lambda b, j, s: (b, 0, j)),
            pl.BlockSpec((1, N_KV_HEADS, L, GROUP * tt),
                         lambda b, j, s: (jnp.where(j == 0, 1, 0), 0, 0, 0)),
            _const_spec3((1, D_MODEL)),
            _const_spec3((HQ, D_MODEL)),
            _const_spec3((HQ, 1)),
            _const_spec3((HEAD_DIM, 1)),
            _const_spec3((HQ, D_MODEL)),
            _const_spec3((1, D_MODEL)),
        ],
        out_specs=pl.BlockSpec((nb, tt, D_MODEL), cur_map),
        scratch_shapes=[
            pltpu.VMEM((HQ, nb * tt), BF16),
            pltpu.VMEM((HQ, nb * tt), F32),
        ],
    )
    return pl.pallas_call(
        kern,
        grid_spec=grid_spec,
        out_shape=jax.ShapeDtypeStruct((B, T, D_MODEL), F32),
        compiler_params=_params(),
        name="swa_prompt",
    )(sinks, x, kb, kb, vt, vt, bias_t, norm.reshape(1, -1), w_q.T, b_q.reshape(-1, 1),
      q_norm.reshape(-1, 1), w_o, b_o.reshape(1, -1))


def _trunk(x, lru_h, lru_conv, ffn_conv, win_k, win_v, P, bias, *, nb, tt, prompt):
    B, T, _ = x.shape
    new_h, new_c, new_f = [], [], []
    k = v = kb = vt = None
    for layer in range(DEPTH):
        if layer < N_A:
            i = layer
            x, h_i, c_i = _lru_block(x, lru_h[i], lru_conv[i], P['a_norm'][i], P['a_w_in'][i],
                                     P['a_conv_w'][i], P['a_conv_b'][i], P['a_gate_r_w'][i],
                                     P['a_gate_r_b'][i], P['a_gate_i_w'][i], P['a_gate_i_b'][i],
                                     P['a_lambda'][i], P['a_w_out'][i], nb=nb, tt=tt)
            new_h.append(h_i.reshape(B, D_RNN))
            new_c.append(c_i)
        else:
            j = layer - N_A
            attn_w = (P['sinks'][j], P['b_norm'][j], P['w_q'][j], P['b_q'][j], P['q_norm'][j],
                      P['w_o'][j], P['b_o'][j])
            if prompt:
                x = _attn_prompt_block(x, kb, vt, bias, *attn_w, nb=nb)
            else:
                x = _attn_sample_block(x, win_k, k, win_v, v, bias, *attn_w, nb=nb)
        x, f_i = _conv_ffn(x, ffn_conv[layer], P['f_norm'][layer], P['f_w_up'][layer],
                           P['f_conv_w'][layer], P['f_conv_b'][layer], P['f_w_down'][layer],
                           nb=nb, tt=tt)
        new_f.append(f_i)
        if layer == N_A - 1:
            kv = _shared_kv(x, P['kv_norm'], P['w_kv'], P['b_kv'], P['k_norm'], nb=nb, tt=tt,
                            key_major_copies=prompt)
            if prompt:
                k, v, kb, vt = kv
            else:
                k, v = kv
    if prompt:
        new_k, new_v = k[:, -WINDOW:], v[:, -WINDOW:]
    else:
        new_k = jnp.concatenate([win_k, k], axis=1)[:, -WINDOW:]
        new_v = jnp.concatenate([win_v, v], axis=1)[:, -WINDOW:]
    shape4 = (B, WINDOW, N_KV_HEADS, HEAD_DIM)
    return (x, jnp.stack(new_h), jnp.stack(new_c), jnp.stack(new_f),
            new_k.reshape(shape4), new_v.reshape(shape4))


def kernel(x_prompt, x_sample, state_lru_h, state_lru_conv, state_ffn_conv, cache_k_win, cache_v_win,
           a_norm, a_w_in, a_conv_w, a_conv_b, a_gate_r_w, a_gate_r_b, a_gate_i_w, a_gate_i_b,
           a_lambda, a_w_out, kv_norm, w_kv, b_kv, k_norm, b_norm, w_q, b_q, q_norm, sinks,
           w_o, b_o, rel_bias, f_norm, f_w_up, f_conv_w, f_conv_b, f_w_down):
    bf = lambda w: w.astype(BF16)
    P = dict(a_norm=a_norm, a_w_in=bf(a_w_in), a_conv_w=a_conv_w, a_conv_b=a_conv_b,
             a_gate_r_w=bf(a_gate_r_w), a_gate_r_b=a_gate_r_b, a_gate_i_w=bf(a_gate_i_w),
             a_gate_i_b=a_gate_i_b, a_lambda=a_lambda, a_w_out=bf(a_w_out), kv_norm=kv_norm,
             w_kv=bf(w_kv), b_kv=b_kv, k_norm=k_norm, b_norm=b_norm, w_q=bf(w_q), b_q=b_q,
             q_norm=q_norm, sinks=sinks, w_o=bf(w_o), b_o=b_o, f_norm=f_norm, f_w_up=bf(f_w_up),
             f_conv_w=f_conv_w, f_conv_b=f_conv_b, f_w_down=bf(f_w_down))
    B, S, _ = x_prompt.shape
    DB, DT, _ = x_sample.shape

    zero_h = jnp.zeros((N_A, B, D_RNN), F32)
    zero_c = jnp.zeros((N_A, B, CONV_W - 1, D_RNN), F32)
    zero_f = jnp.zeros((DEPTH, B, FFN_CONV_W - 1, D_FF), F32)
    out_p = _trunk(x_prompt, zero_h, zero_c, zero_f, None, None, P,
                   _rel_bias_table(rel_bias, WINDOW, keys_on_rows=True), nb=B, tt=WINDOW,
                   prompt=True)

    win_k = cache_k_win.reshape(DB, WINDOW, HKV)
    win_v = cache_v_win.reshape(DB, WINDOW, HKV)
    out_s = _trunk(x_sample, state_lru_h, state_lru_conv, state_ffn_conv, win_k, win_v, P,
                   _rel_bias_table(rel_bias, DT, keys_on_rows=False), nb=DB // 2, tt=DT,
                   prompt=False)
    return (out_p[0], out_s[0]) + out_p[1:] + out_s[1:]
```

```python
import functools
import math

import numpy as np
import jax
import jax.numpy as jnp
from jax import lax
from jax.experimental import pallas as pl
from jax.experimental.pallas import tpu as pltpu

D_MODEL = 1024
DEPTH = 4
N_A = DEPTH // 2
D_RNN = D_MODEL
N_LRU_BLOCKS = 4
LRU_BW = D_RNN // N_LRU_BLOCKS
CONV_W = 4
C_GATE = 8.0
N_HEADS = 16
N_KV_HEADS = 4
HEAD_DIM = 64
GROUP = N_HEADS // N_KV_HEADS
HQ = N_HEADS * HEAD_DIM
HKV = N_KV_HEADS * HEAD_DIM
WINDOW = 128
N_BUCKETS = 32
MAX_DISTANCE = 128
D_FF = 3 * D_MODEL
FFN_CONV_W = 3
EPS = 1e-6
NEG_INF = -1e30
LOG2_E = math.log2(math.e)

F32 = jnp.float32
BF16 = jnp.bfloat16

SUBLANES = 8
FF_CHUNK = 256
SEQS_PER_STEP = 8
VMEM_LIMIT_BYTES = 56 * 1024 * 1024


def _rmsnorm(x, g):
    return x * lax.rsqrt(jnp.mean(x * x, axis=-1, keepdims=True) + EPS) * g


def _softplus(x):
    return jnp.maximum(x, 0.0) + jnp.log(1.0 + jnp.exp(-jnp.abs(x)))


def _mm(a, b):
    return jnp.dot(a.astype(BF16), b, preferred_element_type=F32)


def _const_spec(shape):
    nd = len(shape)
    return pl.BlockSpec(shape, lambda b, j: (0,) * nd, pipeline_mode=pl.Buffered(1))


def _params():
    return pltpu.CompilerParams(dimension_semantics=("arbitrary", "arbitrary"),
                                vmem_limit_bytes=VMEM_LIMIT_BYTES)


def _lru_kernel(x_ref, h0_ref, cbuf_ref, norm_ref, win_ref, cw_ref, cb_ref, grw_ref, grb_ref,
                giw_ref, gib_ref, lam_ref, wout_ref,
                y_ref, hlast_ref, cnew_ref,
                ext_ref, h_ref, mid_ref, *, nb, tt):
    m = nb * tt
    tail = SUBLANES - (CONV_W - 1)

    @pl.when(pl.program_id(1) == 0)
    def _():
        ext_ref[:, tail:SUBLANES, :] = cbuf_ref[...]
        h_ref[...] = h0_ref[...]

    x = x_ref[...].reshape(m, D_MODEL)
    xn = _rmsnorm(x, norm_ref[...]).astype(BF16)
    decay = (-C_GATE * LOG2_E) * _softplus(-lam_ref[...])

    for n in range(N_LRU_BLOCKS):
        cs = slice(n * LRU_BW, (n + 1) * LRU_BW)
        xb = jnp.dot(xn, win_ref[:, D_RNN + n * LRU_BW:D_RNN + (n + 1) * LRU_BW],
                     preferred_element_type=F32)
        ext_ref[:, SUBLANES:, cs] = xb.reshape(nb, tt, LRU_BW)
        xc = cb_ref[:, cs].reshape(1, 1, LRU_BW)
        for k in range(CONV_W):
            xc = xc + cw_ref[k:k + 1, cs].reshape(1, 1, LRU_BW) * ext_ref[:, tail + k:tail + k + tt, cs]
        ext_ref[:, tail:SUBLANES, cs] = ext_ref[:, tail + tt:SUBLANES + tt, cs]
        xc = xc.reshape(m, LRU_BW)
        xcb = xc.astype(BF16)
        r = jax.nn.sigmoid(jnp.dot(xcb, grw_ref[n], preferred_element_type=F32) + grb_ref[:, cs])
        i = jax.nn.sigmoid(jnp.dot(xcb, giw_ref[n], preferred_element_type=F32) + gib_ref[:, cs])
        a = jnp.exp2(r * decay[:, cs])
        b = jnp.sqrt(1.0 - a * a) * (i * xc)
        hs, h_last = _linear_scan(a.reshape(nb, tt, LRU_BW), b.reshape(nb, tt, LRU_BW),
                                  h_ref[:, :, cs])
        h_ref[:, :, cs] = h_last
        gate = jnp.dot(xn, win_ref[:, cs], preferred_element_type=F32)
        mid_ref[:, cs] = (jax.nn.gelu(gate) * hs.reshape(m, LRU_BW)).astype(BF16)

    cnew_ref[...] = ext_ref[:, tail:SUBLANES, :]
    hlast_ref[...] = h_ref[...]
    y = jnp.dot(mid_ref[...], wout_ref[...], preferred_element_type=F32)
    y_ref[...] = (x + y).reshape(nb, tt, D_MODEL)


def _linear_scan(a, b, h0):
    nb, tt, c = a.shape
    groups = tt // SUBLANES
    a = a.reshape(nb * groups, SUBLANES, c)
    b = b.reshape(nb * groups, SUBLANES, c)
    t = lax.broadcasted_iota(jnp.int32, a.shape, 1)
    d = 1
    while d < SUBLANES:
        keep = t >= d
        b = jnp.where(keep, a * pltpu.roll(b, d, axis=1) + b, b)
        a = jnp.where(keep, a * pltpu.roll(a, d, axis=1), a)
        d *= 2
    a = a.reshape(nb, groups, SUBLANES, c)
    b = b.reshape(nb, groups, SUBLANES, c)
    h = h0
    out = []
    for g in range(groups):
        hg = a[:, g] * h + b[:, g]
        out.append(hg)
        h = hg[:, SUBLANES - 1:SUBLANES, :]
    return jnp.concatenate(out, axis=1), h


def _lru_block(x, h0, cbuf, norm, w_in, cw, cb, grw, grb, giw, gib, lam, w_out, *, nb, tt):
    B, T, _ = x.shape
    kern = functools.partial(_lru_kernel, nb=nb, tt=tt)
    row = lambda v: v.reshape(1, -1)
    return pl.pallas_call(
        kern,
        grid=(B // nb, T // tt),
        in_specs=[
            pl.BlockSpec((nb, tt, D_MODEL), lambda b, j: (b, j, 0)),
            pl.BlockSpec((nb, 1, D_RNN), lambda b, j: (b, 0, 0)),
            pl.BlockSpec((nb, CONV_W - 1, D_RNN), lambda b, j: (b, 0, 0)),
            _const_spec((1, D_MODEL)),
            _const_spec((D_MODEL, 2 * D_RNN)),
            _const_spec((CONV_W, D_RNN)),
            _const_spec((1, D_RNN)),
            _const_spec((N_LRU_BLOCKS, LRU_BW, LRU_BW)),
            _const_spec((1, D_RNN)),
            _const_spec((N_LRU_BLOCKS, LRU_BW, LRU_BW)),
            _const_spec((1, D_RNN)),
            _const_spec((1, D_RNN)),
            _const_spec((D_RNN, D_MODEL)),
        ],
        out_specs=[
            pl.BlockSpec((nb, tt, D_MODEL), lambda b, j: (b, j, 0)),
            pl.BlockSpec((nb, 1, D_RNN), lambda b, j: (b, 0, 0)),
            pl.BlockSpec((nb, CONV_W - 1, D_RNN), lambda b, j: (b, 0, 0)),
        ],
        out_shape=[
            jax.ShapeDtypeStruct((B, T, D_MODEL), F32),
            jax.ShapeDtypeStruct((B, 1, D_RNN), F32),
            jax.ShapeDtypeStruct((B, CONV_W - 1, D_RNN), F32),
        ],
        scratch_shapes=[
            pltpu.VMEM((nb, SUBLANES + tt, D_RNN), F32),
            pltpu.VMEM((nb, 1, D_RNN), F32),
            pltpu.VMEM((nb * tt, D_RNN), BF16),
        ],
        compiler_params=_params(),
        name="rglru_block",
    )(x, h0.reshape(B, 1, D_RNN), cbuf, row(norm), w_in, cw, row(cb), grw, row(grb), giw, row(gib), row(lam), w_out)


def _ffn_kernel(x_ref, buf_ref, norm_ref, wup_ref, cw_ref, cb_ref, wdown_ref,
                y_ref, bnew_ref,
                tail_ref, ext_ref, mid_ref, *, nb, tt):
    m = nb * tt
    tail = SUBLANES - (FFN_CONV_W - 1)

    @pl.when(pl.program_id(1) == 0)
    def _():
        tail_ref[...] = buf_ref[...]

    x = x_ref[...].reshape(m, D_MODEL)
    xn = _rmsnorm(x, norm_ref[...]).astype(BF16)
    for c in range(D_FF // FF_CHUNK):
        cs = slice(c * FF_CHUNK, (c + 1) * FF_CHUNK)
        g = jnp.dot(xn, wup_ref[:, cs], preferred_element_type=F32)
        v = jnp.dot(xn, wup_ref[:, D_FF + c * FF_CHUNK:D_FF + (c + 1) * FF_CHUNK],
                    preferred_element_type=F32)
        ext_ref[:, tail:SUBLANES, :] = tail_ref[:, :, cs]
        ext_ref[:, SUBLANES:, :] = g.reshape(nb, tt, FF_CHUNK)
        gc = cb_ref[:, cs].reshape(1, 1, FF_CHUNK)
        for k in range(FFN_CONV_W):
            gc = gc + cw_ref[k:k + 1, cs].reshape(1, 1, FF_CHUNK) * ext_ref[:, tail + k:tail + k + tt, :]
        tail_ref[:, :, cs] = ext_ref[:, tail + tt:SUBLANES + tt, :]
        mid_ref[:, cs] = (jax.nn.gelu(gc).reshape(m, FF_CHUNK) * v).astype(BF16)
    bnew_ref[...] = tail_ref[...]
    y = jnp.dot(mid_ref[...], wdown_ref[...], preferred_element_type=F32)
    y_ref[...] = (x + y).reshape(nb, tt, D_MODEL)


def _conv_ffn(x, buf, norm, w_up, cw, cb, w_down, *, nb, tt):
    B, T, _ = x.shape
    kern = functools.partial(_ffn_kernel, nb=nb, tt=tt)
    return pl.pallas_call(
        kern,
        grid=(B // nb, T // tt),
        in_specs=[
            pl.BlockSpec((nb, tt, D_MODEL), lambda b, j: (b, j, 0)),
            pl.BlockSpec((nb, FFN_CONV_W - 1, D_FF), lambda b, j: (b, 0, 0)),
            _const_spec((1, D_MODEL)),
            _const_spec((D_MODEL, 2 * D_FF)),
            _const_spec((FFN_CONV_W, D_FF)),
            _const_spec((1, D_FF)),
            _const_spec((D_FF, D_MODEL)),
        ],
        out_specs=[
            pl.BlockSpec((nb, tt, D_MODEL), lambda b, j: (b, j, 0)),
            pl.BlockSpec((nb, FFN_CONV_W - 1, D_FF), lambda b, j: (b, 0, 0)),
        ],
        out_shape=[
            jax.ShapeDtypeStruct((B, T, D_MODEL), F32),
            jax.ShapeDtypeStruct((B, FFN_CONV_W - 1, D_FF), F32),
        ],
        scratch_shapes=[
            pltpu.VMEM((nb, FFN_CONV_W - 1, D_FF), F32),
            pltpu.VMEM((nb, SUBLANES + tt, FF_CHUNK), F32),
            pltpu.VMEM((nb * tt, D_FF), BF16),
        ],
        compiler_params=_params(),
        name="conv_ffn",
    )(x, buf, norm.reshape(1, -1), w_up, cw, cb.reshape(1, -1), w_down)


def _kv_kernel(x_ref, norm_ref, w_ref, b_ref, kn_ref, *rest, nb, tt, key_major_copies):
    m = nb * tt
    x = x_ref[...].reshape(m, D_MODEL)
    xn = _rmsnorm(x, norm_ref[...]).astype(BF16)
    kv = jnp.dot(xn, w_ref[...], preferred_element_type=F32) + b_ref[...]
    k = kv[:, :HKV]
    if key_major_copies:
        wvt_ref, bvt_ref, k_ref, v_ref, kb_ref, vt_ref = rest
        vt = lax.dot_general(wvt_ref[...], xn, (((1,), (1,)), ((), ())),
                             preferred_element_type=F32) + bvt_ref[...]
        for n in range(nb):
            vt_ref[n] = vt[:, n * tt:(n + 1) * tt].astype(BF16)
    else:
        k_ref, v_ref = rest
    rows = lax.broadcasted_iota(jnp.int32, (HKV, HKV), 0) // HEAD_DIM
    cols = lax.broadcasted_iota(jnp.int32, (HKV, HKV), 1) // HEAD_DIM
    avg = jnp.where(rows == cols, 1.0 / HEAD_DIM, 0.0).astype(BF16)
    sq = k * k
    hi = sq.astype(BF16)
    lo = (sq - hi.astype(F32)).astype(BF16)
    ms = (jnp.dot(hi, avg, preferred_element_type=F32)
          + jnp.dot(lo, avg, preferred_element_type=F32))
    kn = (k * lax.rsqrt(ms + EPS) * kn_ref[...]).reshape(nb, tt, HKV)
    k_ref[...] = kn
    v_ref[...] = kv[:, HKV:].reshape(nb, tt, HKV)
    if key_major_copies:
        kb_ref[...] = kn.astype(BF16)


def _shared_kv(x, norm, w_kv, b_kv, k_norm, *, nb, tt, key_major_copies):
    B, T, _ = x.shape
    kern = functools.partial(_kv_kernel, nb=nb, tt=tt, key_major_copies=key_major_copies)
    tok_spec = pl.BlockSpec((nb, tt, HKV), lambda b, j: (b, j, 0))
    in_specs = [
        pl.BlockSpec((nb, tt, D_MODEL), lambda b, j: (b, j, 0)),
        _const_spec((1, D_MODEL)),
        _const_spec((D_MODEL, 2 * HKV)),
        _const_spec((1, 2 * HKV)),
        _const_spec((1, HKV)),
    ]
    args = [x, norm.reshape(1, -1), w_kv, b_kv.reshape(1, -1),
            jnp.tile(k_norm, N_KV_HEADS).reshape(1, -1)]
    out_specs = [tok_spec, tok_spec]
    out_shape = [jax.ShapeDtypeStruct((B, T, HKV), F32), jax.ShapeDtypeStruct((B, T, HKV), F32)]
    if key_major_copies:
        in_specs += [_const_spec((HKV, D_MODEL)), _const_spec((HKV, 1))]
        args += [w_kv[:, HKV:].T, b_kv[HKV:].reshape(-1, 1)]
        out_specs += [tok_spec, pl.BlockSpec((nb, HKV, tt), lambda b, j: (b, 0, j))]
        out_shape += [jax.ShapeDtypeStruct((B, T, HKV), BF16),
                      jax.ShapeDtypeStruct((B, HKV, T), BF16)]
    return pl.pallas_call(
        kern,
        grid=(B // nb, T // tt),
        in_specs=in_specs,
        out_specs=out_specs,
        out_shape=out_shape,
        compiler_params=_params(),
        name="shared_kv",
    )(*args)


def _rel_buckets(tt):
    qi = np.arange(tt)[:, None]
    sj = np.arange(WINDOW + tt)[None, :]
    dist = qi + WINDOW - sj
    max_exact = N_BUCKETS // 2
    d = np.maximum(dist, 0)
    df = np.maximum(d, 1).astype(np.float32)
    large = max_exact + (np.log(df / max_exact) / math.log(MAX_DISTANCE / max_exact)
                         * (N_BUCKETS - max_exact)).astype(np.int32)
    large = np.minimum(large, N_BUCKETS - 1)
    bucket = np.where(d < max_exact, d, large)
    valid = (dist >= 0) & (dist < WINDOW)
    return np.where(valid, bucket, -1).astype(np.int32)


def _bias_of_buckets(rb_ref, bkt, h):
    acc = jnp.full(bkt.shape, NEG_INF, F32)
    for c in range(N_BUCKETS):
        acc = jnp.where(bkt == c, rb_ref[c, h], acc)
    return acc


def _bias_kernel(rb_ref, bkt_ref, out_ref):
    bkt = bkt_ref[...]

    def head(h, carry):
        out_ref[h] = _bias_of_buckets(rb_ref, bkt, h)
        return carry

    lax.fori_loop(0, N_HEADS, head, 0)


def _bias_t_kernel(rb_ref, bkt_ref, out_ref, *, tt):
    bkt = bkt_ref[...]
    before_tile = lax.broadcasted_iota(jnp.int32, bkt.shape, 0) < WINDOW

    def group(g, carry):
        for hh in range(GROUP):
            acc = _bias_of_buckets(rb_ref, bkt, g * GROUP + hh)
            out_ref[0, g, :, hh * tt:(hh + 1) * tt] = acc
            out_ref[1, g, :, hh * tt:(hh + 1) * tt] = jnp.where(before_tile, NEG_INF, acc)
        return carry

    lax.fori_loop(0, N_KV_HEADS, group, 0)


def _rel_bias_table(rel_bias, tt, *, keys_on_rows):
    L = WINDOW + tt
    bkt = _rel_buckets(tt)
    if keys_on_rows:
        kern = functools.partial(_bias_t_kernel, tt=tt)
        bkt = bkt.T
        out_shape = jax.ShapeDtypeStruct((2, N_KV_HEADS, L, GROUP * tt), F32)
    else:
        kern = _bias_kernel
        out_shape = jax.ShapeDtypeStruct((N_HEADS, tt, L), F32)
    return pl.pallas_call(
        kern,
        in_specs=[
            pl.BlockSpec(memory_space=pltpu.SMEM),
            pl.BlockSpec(memory_space=pltpu.VMEM),
        ],
        out_specs=pl.BlockSpec(memory_space=pltpu.VMEM),
        out_shape=out_shape,
        name="rel_bias_table",
    )(rel_bias, jnp.asarray(bkt))


def _attn_sample_kernel(sinks_ref, x_ref, kp_ref, kc_ref, vp_ref, vc_ref, bias_ref, norm_ref, wq_ref,
                        bq_ref, qn_ref, wo_ref, bo_ref, y_ref, q_ref, o_ref, *, nb, tt):
    m = nb * tt
    L = WINDOW + tt
    sb = SEQS_PER_STEP
    x = x_ref[...].reshape(m, D_MODEL)
    q = _mm(_rmsnorm(x, norm_ref[...]), wq_ref[...]) + bq_ref[...]
    qgain = qn_ref[...] * (HEAD_DIM ** -0.5)
    for h in range(N_HEADS):
        hs = slice(h * HEAD_DIM, (h + 1) * HEAD_DIM)
        qh = q[:, hs]
        q_ref[:, hs] = qh * lax.rsqrt(jnp.mean(qh * qh, axis=-1, keepdims=True) + EPS) * qgain
    sinks = [jnp.concatenate([jnp.full((tt, 1), sinks_ref[g * GROUP + hh], F32)
                              for hh in range(GROUP)], axis=0)[None] for g in range(N_KV_HEADS)]

    def seqs(i, carry):
        n0 = pl.multiple_of(i * sb, sb)
        rows = pl.ds(pl.multiple_of(i * (sb * tt), sb * tt), sb * tt)
        qblk = q_ref[rows, :]
        kall = jnp.concatenate([kp_ref[pl.ds(n0, sb)], kc_ref[pl.ds(n0, sb)]], axis=1)
        vall = jnp.concatenate([vp_ref[pl.ds(n0, sb)], vc_ref[pl.ds(n0, sb)]], axis=1)
        scores = []
        for g in range(N_KV_HEADS):
            gs = slice(g * HEAD_DIM, (g + 1) * HEAD_DIM)
            q4 = jnp.concatenate(
                [qblk[:, h * HEAD_DIM:(h + 1) * HEAD_DIM].reshape(sb, tt, HEAD_DIM)
                 for h in range(g * GROUP, (g + 1) * GROUP)], axis=1).astype(BF16)
            s = jnp.einsum('bqd,bkd->bqk', q4, kall[:, :, gs].astype(BF16),
                           preferred_element_type=F32)
            scores.append(s + bias_ref[g * GROUP:(g + 1) * GROUP].reshape(1, GROUP * tt, L))
        probs, dens = [], []
        for g in range(N_KV_HEADS):
            s = scores[g]
            mx = jnp.maximum(jnp.max(s, axis=-1, keepdims=True), sinks[g])
            p = jnp.exp(s - mx)
            dens.append(jnp.sum(p, axis=-1, keepdims=True) + jnp.exp(sinks[g] - mx))
            probs.append(p.astype(BF16))
        for g in range(N_KV_HEADS):
            gs = slice(g * HEAD_DIM, (g + 1) * HEAD_DIM)
            o = jnp.einsum('bqk,bkd->bqd', probs[g], vall[:, :, gs].astype(BF16),
                           preferred_element_type=F32) / dens[g]
            for hh in range(GROUP):
                h = g * GROUP + hh
                o_ref[rows, h * HEAD_DIM:(h + 1) * HEAD_DIM] = (
                    o[:, hh * tt:(hh + 1) * tt, :].reshape(sb * tt, HEAD_DIM))
        return carry

    lax.fori_loop(0, nb // sb, seqs, 0)
    y = _mm(o_ref[...], wo_ref[...]) + bo_ref[...]
    y_ref[...] = (x + y).reshape(nb, tt, D_MODEL)


def _attn_prompt_kernel(sinks_ref, x_ref, kp_ref, kc_ref, vtp_ref, vtc_ref, bias_ref, norm_ref,
                        wqt_ref, bq_ref, qn_ref, wo_ref, bo_ref, y_ref, qt_ref, ot_ref, *, nb, tt):
    m = nb * tt
    x = x_ref[...].reshape(m, D_MODEL)
    xn = _rmsnorm(x, norm_ref[...]).astype(BF16)
    qt = lax.dot_general(wqt_ref[...], xn, (((1,), (1,)), ((), ())),
                         preferred_element_type=F32) + bq_ref[...]
    qgain = qn_ref[...] * (HEAD_DIM ** -0.5)
    for h in range(N_HEADS):
        hs = slice(h * HEAD_DIM, (h + 1) * HEAD_DIM)
        qh = qt[hs, :]
        ms = jnp.mean(qh * qh, axis=0, keepdims=True)
        qt_ref[hs, :] = (qh * lax.rsqrt(ms + EPS) * qgain).astype(BF16)

    head_of_lane = lax.broadcasted_iota(jnp.int32, (1, GROUP * tt), 1) // tt
    for g in range(N_KV_HEADS):
        gs = slice(g * HEAD_DIM, (g + 1) * HEAD_DIM)
        sink = jnp.full((1, GROUP * tt), sinks_ref[g * GROUP], F32)
        for hh in range(1, GROUP):
            sink = jnp.where(head_of_lane == hh, sinks_ref[g * GROUP + hh], sink)
        bias = bias_ref[0, g]
        for n in range(nb):
            ts = slice(n * tt, (n + 1) * tt)
            kk = jnp.concatenate([kp_ref[n, :, gs], kc_ref[n, :, gs]], axis=0)
            q4 = jnp.concatenate([qt_ref[h * HEAD_DIM:(h + 1) * HEAD_DIM, ts]
                                  for h in range(g * GROUP, (g + 1) * GROUP)], axis=1)
            s = jnp.dot(kk, q4, preferred_element_type=F32) + bias
            mx = jnp.maximum(jnp.max(s, axis=0, keepdims=True), sink)
            p = jnp.exp(s - mx)
            den = jnp.sum(p, axis=0, keepdims=True) + jnp.exp(sink - mx)
            vt = jnp.concatenate([vtp_ref[n, gs, :], vtc_ref[n, gs, :]], axis=1)
            o = jnp.dot(vt, p.astype(BF16), preferred_element_type=F32) / den
            for hh in range(GROUP):
                h = g * GROUP + hh
                ot_ref[h * HEAD_DIM:(h + 1) * HEAD_DIM, ts] = o[:, hh * tt:(hh + 1) * tt]
    y = _mm(ot_ref[...].T, wo_ref[...]) + bo_ref[...]
    y_ref[...] = (x + y).reshape(nb, tt, D_MODEL)


def _const_spec3(shape):
    nd = len(shape)
    return pl.BlockSpec(shape, lambda b, j, s: (0,) * nd, pipeline_mode=pl.Buffered(1))


def _attn_sample_block(x, k_win, k_new, v_win, v_new, bias, sinks, norm, w_q, b_q, q_norm, w_o, b_o,
                       *, nb):
    B, tt, _ = x.shape
    L = WINDOW + tt
    assert nb % SEQS_PER_STEP == 0 and tt % SUBLANES == 0
    kern = functools.partial(_attn_sample_kernel, nb=nb, tt=tt)
    seq_map = lambda b, j, s: (b, 0, 0)
    grid_spec = pltpu.PrefetchScalarGridSpec(
        num_scalar_prefetch=1,
        grid=(B // nb, 1),
        in_specs=[
            pl.BlockSpec((nb, tt, D_MODEL), seq_map),
            pl.BlockSpec((nb, WINDOW, HKV), seq_map),
            pl.BlockSpec((nb, tt, HKV), seq_map),
            pl.BlockSpec((nb, WINDOW, HKV), seq_map),
            pl.BlockSpec((nb, tt, HKV), seq_map),
            _const_spec3((N_HEADS, tt, L)),
            _const_spec3((1, D_MODEL)),
            _const_spec3((D_MODEL, HQ)),
            _const_spec3((1, HQ)),
            _const_spec3((1, HEAD_DIM)),
            _const_spec3((HQ, D_MODEL)),
            _const_spec3((1, D_MODEL)),
        ],
        out_specs=pl.BlockSpec((nb, tt, D_MODEL), seq_map),
        scratch_shapes=[
            pltpu.VMEM((nb * tt, HQ), F32),
            pltpu.VMEM((nb * tt, HQ), F32),
        ],
    )
    return pl.pallas_call(
        kern,
        grid_spec=grid_spec,
        out_shape=jax.ShapeDtypeStruct((B, tt, D_MODEL), F32),
        compiler_params=_params(),
        name="swa_sample",
    )(sinks, x, k_win, k_new, v_win, v_new, bias, norm.reshape(1, -1), w_q, b_q.reshape(1, -1),
      q_norm.reshape(1, -1), w_o, b_o.reshape(1, -1))


def _attn_prompt_block(x, kb, vt, bias_t, sinks, norm, w_q, b_q, q_norm, w_o, b_o, *, nb):
    B, T, _ = x.shape
    tt = WINDOW
    L = WINDOW + tt
    kern = functools.partial(_attn_prompt_kernel, nb=nb, tt=tt)
    cur_map = lambda b, j, s: (b, j, 0)
    prev_map = lambda b, j, s: (b, jnp.maximum(j - 1, 0), 0)
    grid_spec = pltpu.PrefetchScalarGridSpec(
        num_scalar_prefetch=1,
        grid=(B // nb, T // tt),
        in_specs=[
            pl.BlockSpec((nb, tt, D_MODEL), cur_map),
            pl.BlockSpec((nb, tt, HKV), prev_map),
            pl.BlockSpec((nb, tt, HKV), cur_map),
            pl.BlockSpec((nb, HKV, tt), lambda b, j, s: (b, 0, jnp.maximum(j - 1, 0))),
            pl.BlockSpec((nb, HKV, tt), lambda b, j, s: (b, 0, j)),
            pl.BlockSpec((1, N_KV_HEADS, L, GROUP * tt),
                         lambda b, j, s: (jnp.where(j == 0, 1, 0), 0, 0, 0)),
            _const_spec3((1, D_MODEL)),
            _const_spec3((HQ, D_MODEL)),
            _const_spec3((HQ, 1)),
            _const_spec3((HEAD_DIM, 1)),
            _const_spec3((HQ, D_MODEL)),
            _const_spec3((1, D_MODEL)),
        ],
        out_specs=pl.BlockSpec((nb, tt, D_MODEL), cur_map),
        scratch_shapes=[
            pltpu.VMEM((HQ, nb * tt), BF16),
            pltpu.VMEM((HQ, nb * tt), F32),
        ],
    )
    return pl.pallas_call(
        kern,
        grid_spec=grid_spec,
        out_shape=jax.ShapeDtypeStruct((B, T, D_MODEL), F32),
        compiler_params=_params(),
        name="swa_prompt",
    )(sinks, x, kb, kb, vt, vt, bias_t, norm.reshape(1, -1), w_q.T, b_q.reshape(-1, 1),
      q_norm.reshape(-1, 1), w_o, b_o.reshape(1, -1))


def _trunk(x, lru_h, lru_conv, ffn_conv, win_k, win_v, P, bias, *, nb, tt, prompt):
    B, T, _ = x.shape
    new_h, new_c, new_f = [], [], []
    k = v = kb = vt = None
    for layer in range(DEPTH):
        if layer < N_A:
            i = layer
            x, h_i, c_i = _lru_block(x, lru_h[i], lru_conv[i], P['a_norm'][i], P['a_w_in'][i],
                                     P['a_conv_w'][i], P['a_conv_b'][i], P['a_gate_r_w'][i],
                                     P['a_gate_r_b'][i], P['a_gate_i_w'][i], P['a_gate_i_b'][i],
                                     P['a_lambda'][i], P['a_w_out'][i], nb=nb, tt=tt)
            new_h.append(h_i.reshape(B, D_RNN))
            new_c.append(c_i)
        else:
            j = layer - N_A
            attn_w = (P['sinks'][j], P['b_norm'][j], P['w_q'][j], P['b_q'][j], P['q_norm'][j],
                      P['w_o'][j], P['b_o'][j])
            if prompt:
                x = _attn_prompt_block(x, kb, vt, bias, *attn_w, nb=nb)
            else:
                x = _attn_sample_block(x, win_k, k, win_v, v, bias, *attn_w, nb=nb)
        x, f_i = _conv_ffn(x, ffn_conv[layer], P['f_norm'][layer], P['f_w_up'][layer],
                           P['f_conv_w'][layer], P['f_conv_b'][layer], P['f_w_down'][layer],
                           nb=nb, tt=tt)
        new_f.append(f_i)
        if layer == N_A - 1:
            kv = _shared_kv(x, P['kv_norm'], P['w_kv'], P['b_kv'], P['k_norm'], nb=nb, tt=tt,
                            key_major_copies=prompt)
            if prompt:
                k, v, kb, vt = kv
            else:
                k, v = kv
    if prompt:
        new_k, new_v = k[:, -WINDOW:], v[:, -WINDOW:]
    else:
        new_k = jnp.concatenate([win_k, k], axis=1)[:, -WINDOW:]
        new_v = jnp.concatenate([win_v, v], axis=1)[:, -WINDOW:]
    shape4 = (B, WINDOW, N_KV_HEADS, HEAD_DIM)
    return (x, jnp.stack(new_h), jnp.stack(new_c), jnp.stack(new_f),
            new_k.reshape(shape4), new_v.reshape(shape4))


def kernel(x_prompt, x_sample, state_lru_h, state_lru_conv, state_ffn_conv, cache_k_win, cache_v_win,
           a_norm, a_w_in, a_conv_w, a_conv_b, a_gate_r_w, a_gate_r_b, a_gate_i_w, a_gate_i_b,
           a_lambda, a_w_out, kv_norm, w_kv, b_kv, k_norm, b_norm, w_q, b_q, q_norm, sinks,
           w_o, b_o, rel_bias, f_norm, f_w_up, f_conv_w, f_conv_b, f_w_down):
    bf = lambda w: w.astype(BF16)
    P = dict(a_norm=a_norm, a_w_in=bf(a_w_in), a_conv_w=a_conv_w, a_conv_b=a_conv_b,
             a_gate_r_w=bf(a_gate_r_w), a_gate_r_b=a_gate_r_b, a_gate_i_w=bf(a_gate_i_w),
             a_gate_i_b=a_gate_i_b, a_lambda=a_lambda, a_w_out=bf(a_w_out), kv_norm=kv_norm,
             w_kv=bf(w_kv), b_kv=b_kv, k_norm=k_norm, b_norm=b_norm, w_q=bf(w_q), b_q=b_q,
             q_norm=q_norm, sinks=sinks, w_o=bf(w_o), b_o=b_o, f_norm=f_norm, f_w_up=bf(f_w_up),
             f_conv_w=f_conv_w, f_conv_b=f_conv_b, f_w_down=bf(f_w_down))
    B, S, _ = x_prompt.shape
    DB, DT, _ = x_sample.shape

    zero_h = jnp.zeros((N_A, B, D_RNN), F32)
    zero_c = jnp.zeros((N_A, B, CONV_W - 1, D_RNN), F32)
    zero_f = jnp.zeros((DEPTH, B, FFN_CONV_W - 1, D_FF), F32)
    out_p = _trunk(x_prompt, zero_h, zero_c, zero_f, None, None, P,
                   _rel_bias_table(rel_bias, WINDOW, keys_on_rows=True), nb=B, tt=WINDOW,
                   prompt=True)

    win_k = cache_k_win.reshape(DB, WINDOW, HKV)
    win_v = cache_v_win.reshape(DB, WINDOW, HKV)
    out_s = _trunk(x_sample, state_lru_h, state_lru_conv, state_ffn_conv, win_k, win_v, P,
                   _rel_bias_table(rel_bias, DT, keys_on_rows=False), nb=DB // 2, tt=DT,
                   prompt=False)
    return (out_p[0], out_s[0]) + out_p[1:] + out_s[1:]
```

```python
import functools
import math

import numpy as np
import jax
import jax.numpy as jnp
from jax import lax
from jax.experimental import pallas as pl
from jax.experimental.pallas import tpu as pltpu

D_MODEL = 1024
DEPTH = 4
N_A = DEPTH // 2
D_RNN = D_MODEL
N_LRU_BLOCKS = 4
LRU_BW = D_RNN // N_LRU_BLOCKS
CONV_W = 4
C_GATE = 8.0
N_HEADS = 16
N_KV_HEADS = 4
HEAD_DIM = 64
GROUP = N_HEADS // N_KV_HEADS
HQ = N_HEADS * HEAD_DIM
HKV = N_KV_HEADS * HEAD_DIM
WINDOW = 128
N_BUCKETS = 32
MAX_DISTANCE = 128
D_FF = 3 * D_MODEL
FFN_CONV_W = 3
EPS = 1e-6
NEG_INF = -1e30
LOG2_E = math.log2(math.e)

F32 = jnp.float32
BF16 = jnp.bfloat16

SUBLANES = 8
FF_CHUNK = 256
SEQS_PER_STEP = 8
VMEM_LIMIT_BYTES = 56 * 1024 * 1024


def _rmsnorm(x, g):
    return x * lax.rsqrt(jnp.mean(x * x, axis=-1, keepdims=True) + EPS) * g


def _softplus(x):
    return jnp.maximum(x, 0.0) + jnp.log(1.0 + jnp.exp(-jnp.abs(x)))


def _mm(a, b):
    return jnp.dot(a.astype(BF16), b, preferred_element_type=F32)


def _causal_dwconv(x, last_group, w, b):
    nb, tt, c = x.shape
    taps = w.shape[0]
    groups = tt // SUBLANES
    x4 = x.reshape(nb, groups, SUBLANES, c)
    t = lax.broadcasted_iota(jnp.int32, (1, 1, SUBLANES, c), 2)
    tap = lambda k: w[k:k + 1, :].reshape(1, 1, 1, c)
    y = b.reshape(1, 1, 1, c) + tap(taps - 1) * x4
    for s in range(1, taps):
        cur = pltpu.roll(x4.reshape(nb * groups, SUBLANES, c), s, axis=1)
        cur = cur.reshape(nb, groups, SUBLANES, c)
        before = pltpu.roll(last_group, s, axis=1)[:, None]
        if groups > 1:
            before = jnp.concatenate([before, cur[:, :groups - 1]], axis=1)
        y = y + tap(taps - 1 - s) * jnp.where(t >= s, cur, before)
    return y.reshape(nb, tt, c), x4[:, groups - 1]


def _const_spec(shape, layer=None):
    nd = len(shape)
    if layer is None:
        return pl.BlockSpec(shape, lambda b, j: (0,) * nd, pipeline_mode=pl.Buffered(1))
    return pl.BlockSpec((None,) + shape, lambda b, j: (layer,) + (0,) * nd,
                        pipeline_mode=pl.Buffered(1))


def _params(**kwargs):
    return pltpu.CompilerParams(dimension_semantics=("arbitrary", "arbitrary"),
                                vmem_limit_bytes=VMEM_LIMIT_BYTES, **kwargs)


def _lru_kernel(x_ref, h0_ref, cbuf_ref, norm_ref, win_ref, cw_ref, cb_ref, grw_ref, grb_ref,
                giw_ref, gib_ref, lam_ref, wout_ref,
                y_ref, hlast_ref, cnew_ref,
                last_ref, h_ref, mid_ref, *, nb, tt):
    m = nb * tt
    tail = SUBLANES - (CONV_W - 1)

    @pl.when(pl.program_id(1) == 0)
    def _():
        last_ref[:, :tail, :] = jnp.zeros((nb, tail, D_RNN), F32)
        last_ref[:, tail:, :] = cbuf_ref[...]
        h_ref[...] = h0_ref[...]

    x = x_ref[...].reshape(m, D_MODEL)
    xn = _rmsnorm(x, norm_ref[...]).astype(BF16)
    decay = (-C_GATE * LOG2_E) * _softplus(-lam_ref[...])

    for n in range(N_LRU_BLOCKS):
        cs = slice(n * LRU_BW, (n + 1) * LRU_BW)
        xb = jnp.dot(xn, win_ref[:, D_RNN + n * LRU_BW:D_RNN + (n + 1) * LRU_BW],
                     preferred_element_type=F32)
        xc, last = _causal_dwconv(xb.reshape(nb, tt, LRU_BW), last_ref[:, :, cs],
                                  cw_ref[:, cs], cb_ref[:, cs])
        last_ref[:, :, cs] = last
        xc = xc.reshape(m, LRU_BW)
        xcb = xc.astype(BF16)
        r = jax.nn.sigmoid(jnp.dot(xcb, grw_ref[n], preferred_element_type=F32) + grb_ref[:, cs])
        i = jax.nn.sigmoid(jnp.dot(xcb, giw_ref[n], preferred_element_type=F32) + gib_ref[:, cs])
        a = jnp.exp2(r * decay[:, cs])
        b = jnp.exp2(0.5 * jnp.log2(1.0 - a * a)) * (i * xc)
        hs, h_last = _linear_scan(a.reshape(nb, tt, LRU_BW), b.reshape(nb, tt, LRU_BW),
                                  h_ref[:, :, cs])
        h_ref[:, :, cs] = h_last
        gate = jnp.dot(xn, win_ref[:, cs], preferred_element_type=F32)
        mid_ref[:, cs] = (jax.nn.gelu(gate) * hs.reshape(m, LRU_BW)).astype(BF16)

    cnew_ref[...] = last_ref[:, tail:, :]
    hlast_ref[...] = h_ref[...]
    y = jnp.dot(mid_ref[...], wout_ref[...], preferred_element_type=F32)
    y_ref[...] = (x + y).reshape(nb, tt, D_MODEL)


def _linear_scan(a, b, h0):
    nb, tt, c = a.shape
    groups = tt // SUBLANES
    a = a.reshape(nb * groups, SUBLANES, c)
    b = b.reshape(nb * groups, SUBLANES, c)
    t = lax.broadcasted_iota(jnp.int32, a.shape, 1)
    d = 1
    while d < SUBLANES:
        keep = t >= d
        b = jnp.where(keep, a * pltpu.roll(b, d, axis=1) + b, b)
        a = jnp.where(keep, a * pltpu.roll(a, d, axis=1), a)
        d *= 2
    a = a.reshape(nb, groups, SUBLANES, c)
    b = b.reshape(nb, groups, SUBLANES, c)
    h = h0
    out = []
    for g in range(groups):
        hg = a[:, g] * h + b[:, g]
        out.append(hg)
        h = hg[:, SUBLANES - 1:SUBLANES, :]
    return jnp.concatenate(out, axis=1), h


def _lru_block(x, h0, cbuf, norm, w_in, cw, cb, grw, grb, giw, gib, lam, w_out, *, nb, tt, layer):
    B, T, _ = x.shape
    kern = functools.partial(_lru_kernel, nb=nb, tt=tt)
    row = lambda v: v.reshape(1, -1)
    return pl.pallas_call(
        kern,
        grid=(B // nb, T // tt),
        in_specs=[
            pl.BlockSpec((nb, tt, D_MODEL), lambda b, j: (b, j, 0)),
            pl.BlockSpec((nb, 1, D_RNN), lambda b, j: (b, 0, 0)),
            pl.BlockSpec((nb, CONV_W - 1, D_RNN), lambda b, j: (b, 0, 0)),
            _const_spec((1, D_MODEL)),
            _const_spec((D_MODEL, 2 * D_RNN), layer),
            _const_spec((CONV_W, D_RNN)),
            _const_spec((1, D_RNN)),
            _const_spec((N_LRU_BLOCKS, LRU_BW, LRU_BW), layer),
            _const_spec((1, D_RNN)),
            _const_spec((N_LRU_BLOCKS, LRU_BW, LRU_BW), layer),
            _const_spec((1, D_RNN)),
            _const_spec((1, D_RNN)),
            _const_spec((D_RNN, D_MODEL), layer),
        ],
        out_specs=[
            pl.BlockSpec((nb, tt, D_MODEL), lambda b, j: (b, j, 0)),
            pl.BlockSpec((nb, 1, D_RNN), lambda b, j: (b, 0, 0)),
            pl.BlockSpec((nb, CONV_W - 1, D_RNN), lambda b, j: (b, 0, 0)),
        ],
        out_shape=[
            jax.ShapeDtypeStruct((B, T, D_MODEL), F32),
            jax.ShapeDtypeStruct((B, 1, D_RNN), F32),
            jax.ShapeDtypeStruct((B, CONV_W - 1, D_RNN), F32),
        ],
        scratch_shapes=[
            pltpu.VMEM((nb, SUBLANES, D_RNN), F32),
            pltpu.VMEM((nb, 1, D_RNN), F32),
            pltpu.VMEM((nb * tt, D_RNN), BF16),
        ],
        compiler_params=_params(),
        name="rglru_block",
    )(x, h0.reshape(B, 1, D_RNN), cbuf, row(norm), w_in, cw, row(cb), grw, row(grb), giw, row(gib), row(lam), w_out)


def _ffn_kernel(x_ref, buf_ref, norm_ref, wup_ref, cw_ref, cb_ref, wdown_ref,
                y_ref, bnew_ref,
                last_ref, ext_ref, mid_ref, *, nb, tt):
    m = nb * tt
    tail = SUBLANES - (FFN_CONV_W - 1)

    @pl.when(pl.program_id(1) == 0)
    def _():
        last_ref[:, :tail, :] = jnp.zeros((nb, tail, D_FF), F32)
        last_ref[:, tail:, :] = buf_ref[...]

    x = x_ref[...].reshape(m, D_MODEL)
    xn = _rmsnorm(x, norm_ref[...]).astype(BF16)
    for c in range(D_FF // FF_CHUNK):
        cs = slice(c * FF_CHUNK, (c + 1) * FF_CHUNK)
        g = jnp.dot(xn, wup_ref[:, cs], preferred_element_type=F32).reshape(nb, tt, FF_CHUNK)
        v = jnp.dot(xn, wup_ref[:, D_FF + c * FF_CHUNK:D_FF + (c + 1) * FF_CHUNK],
                    preferred_element_type=F32)
        if tt == SUBLANES:
            gc, last = _causal_dwconv(g, last_ref[:, :, cs], cw_ref[:, cs], cb_ref[:, cs])
            last_ref[:, :, cs] = last
        else:
            ext_ref[:, tail:SUBLANES, :] = last_ref[:, tail:, cs]
            ext_ref[:, SUBLANES:, :] = g
            gc = cb_ref[:, cs].reshape(1, 1, FF_CHUNK)
            for k in range(FFN_CONV_W):
                gc = gc + (cw_ref[k:k + 1, cs].reshape(1, 1, FF_CHUNK)
                           * ext_ref[:, tail + k:tail + k + tt, :])
            last_ref[:, tail:, cs] = ext_ref[:, tail + tt:SUBLANES + tt, :]
        mid_ref[:, cs] = (jax.nn.gelu(gc).reshape(m, FF_CHUNK) * v).astype(BF16)
    bnew_ref[...] = last_ref[:, tail:, :]
    y = jnp.dot(mid_ref[...], wdown_ref[...], preferred_element_type=F32)
    y_ref[...] = (x + y).reshape(nb, tt, D_MODEL)


def _conv_ffn(x, buf, norm, w_up, cw, cb, w_down, *, nb, tt, layer):
    B, T, _ = x.shape
    kern = functools.partial(_ffn_kernel, nb=nb, tt=tt)
    return pl.pallas_call(
        kern,
        grid=(B // nb, T // tt),
        in_specs=[
            pl.BlockSpec((nb, tt, D_MODEL), lambda b, j: (b, j, 0)),
            pl.BlockSpec((None, nb, FFN_CONV_W - 1, D_FF), lambda b, j: (layer, b, 0, 0)),
            _const_spec((1, D_MODEL)),
            _const_spec((D_MODEL, 2 * D_FF), layer),
            _const_spec((FFN_CONV_W, D_FF)),
            _const_spec((1, D_FF)),
            _const_spec((D_FF, D_MODEL), layer),
        ],
        out_specs=[
            pl.BlockSpec((nb, tt, D_MODEL), lambda b, j: (b, j, 0)),
            pl.BlockSpec((nb, FFN_CONV_W - 1, D_FF), lambda b, j: (b, 0, 0)),
        ],
        out_shape=[
            jax.ShapeDtypeStruct((B, T, D_MODEL), F32),
            jax.ShapeDtypeStruct((B, FFN_CONV_W - 1, D_FF), F32),
        ],
        scratch_shapes=[
            pltpu.VMEM((nb, SUBLANES, D_FF), F32),
            pltpu.VMEM((nb, SUBLANES + tt, FF_CHUNK), F32),
            pltpu.VMEM((nb * tt, D_FF), BF16),
        ],
        compiler_params=_params(),
        name="conv_ffn",
    )(x, buf, norm.reshape(1, -1), w_up, cw, cb.reshape(1, -1), w_down)


def _kv_kernel(x_ref, norm_ref, w_ref, b_ref, kn_ref, *rest, nb, tt, key_major_copies):
    m = nb * tt
    x = x_ref[...].reshape(m, D_MODEL)
    xn = _rmsnorm(x, norm_ref[...]).astype(BF16)
    kv = jnp.dot(xn, w_ref[...], preferred_element_type=F32) + b_ref[...]
    k = kv[:, :HKV]
    if key_major_copies:
        wvt_ref, bvt_ref, k_ref, v_ref, kb_ref, vt_ref = rest
        vt = lax.dot_general(wvt_ref[...], xn, (((1,), (1,)), ((), ())),
                             preferred_element_type=F32) + bvt_ref[...]
        for n in range(nb):
            vt_ref[n] = vt[:, n * tt:(n + 1) * tt].astype(BF16)
    else:
        k_ref, v_ref = rest
    rows = lax.broadcasted_iota(jnp.int32, (HKV, HKV), 0) // HEAD_DIM
    cols = lax.broadcasted_iota(jnp.int32, (HKV, HKV), 1) // HEAD_DIM
    avg = jnp.where(rows == cols, 1.0 / HEAD_DIM, 0.0).astype(BF16)
    sq = k * k
    hi = sq.astype(BF16)
    lo = (sq - hi.astype(F32)).astype(BF16)
    ms = (jnp.dot(hi, avg, preferred_element_type=F32)
          + jnp.dot(lo, avg, preferred_element_type=F32))
    kn = (k * lax.rsqrt(ms + EPS) * kn_ref[...]).reshape(nb, tt, HKV)
    k_ref[...] = kn
    v_ref[...] = kv[:, HKV:].reshape(nb, tt, HKV)
    if key_major_copies:
        kb_ref[...] = kn.astype(BF16)


def _shared_kv(x, norm, w_kv, b_kv, k_norm, *, nb, tt, key_major_copies):
    B, T, _ = x.shape
    kern = functools.partial(_kv_kernel, nb=nb, tt=tt, key_major_copies=key_major_copies)
    tok_spec = pl.BlockSpec((nb, tt, HKV), lambda b, j: (b, j, 0))
    in_specs = [
        pl.BlockSpec((nb, tt, D_MODEL), lambda b, j: (b, j, 0)),
        _const_spec((1, D_MODEL)),
        _const_spec((D_MODEL, 2 * HKV)),
        _const_spec((1, 2 * HKV)),
        _const_spec((1, HKV)),
    ]
    args = [x, norm.reshape(1, -1), w_kv, b_kv.reshape(1, -1),
            jnp.tile(k_norm, N_KV_HEADS).reshape(1, -1)]
    out_specs = [tok_spec, tok_spec]
    out_shape = [jax.ShapeDtypeStruct((B, T, HKV), F32), jax.ShapeDtypeStruct((B, T, HKV), F32)]
    if key_major_copies:
        in_specs += [_const_spec((HKV, D_MODEL)), _const_spec((HKV, 1))]
        args += [w_kv[:, HKV:].T, b_kv[HKV:].reshape(-1, 1)]
        out_specs += [tok_spec, pl.BlockSpec((nb, HKV, tt), lambda b, j: (b, 0, j))]
        out_shape += [jax.ShapeDtypeStruct((B, T, HKV), BF16),
                      jax.ShapeDtypeStruct((B, HKV, T), BF16)]
    return pl.pallas_call(
        kern,
        grid=(B // nb, T // tt),
        in_specs=in_specs,
        out_specs=out_specs,
        out_shape=out_shape,
        compiler_params=_params(),
        name="shared_kv",
    )(*args)


def _rel_buckets(tt):
    qi = np.arange(tt)[:, None]
    sj = np.arange(WINDOW + tt)[None, :]
    dist = qi + WINDOW - sj
    max_exact = N_BUCKETS // 2
    d = np.maximum(dist, 0)
    df = np.maximum(d, 1).astype(np.float32)
    large = max_exact + (np.log(df / max_exact) / math.log(MAX_DISTANCE / max_exact)
                         * (N_BUCKETS - max_exact)).astype(np.int32)
    large = np.minimum(large, N_BUCKETS - 1)
    bucket = np.where(d < max_exact, d, large)
    valid = (dist >= 0) & (dist < WINDOW)
    return np.where(valid, bucket, -1).astype(np.int32)


def _bias_of_buckets(rb_ref, bkt, h):
    acc = jnp.full(bkt.shape, NEG_INF, F32)
    for c in range(N_BUCKETS):
        acc = jnp.where(bkt == c, rb_ref[c, h], acc)
    return acc


def _bias_kernel(rb_ref, bkt_ref, out_ref):
    bkt = bkt_ref[...]

    def head(h, carry):
        out_ref[h] = _bias_of_buckets(rb_ref, bkt, h)
        return carry

    lax.fori_loop(0, N_HEADS, head, 0)


def _bias_t_kernel(rb_ref, bkt_ref, out_ref, *, tt):
    bkt = bkt_ref[...]
    before_tile = lax.broadcasted_iota(jnp.int32, bkt.shape, 0) < WINDOW

    def group(g, carry):
        for hh in range(GROUP):
            acc = _bias_of_buckets(rb_ref, bkt, g * GROUP + hh)
            out_ref[0, g, :, hh * tt:(hh + 1) * tt] = acc
            out_ref[1, g, :, hh * tt:(hh + 1) * tt] = jnp.where(before_tile, NEG_INF, acc)
        return carry

    lax.fori_loop(0, N_KV_HEADS, group, 0)


def _rel_bias_table(rel_bias, tt, *, keys_on_rows):
    L = WINDOW + tt
    bkt = _rel_buckets(tt)
    if keys_on_rows:
        kern = functools.partial(_bias_t_kernel, tt=tt)
        bkt = bkt.T
        out_shape = jax.ShapeDtypeStruct((2, N_KV_HEADS, L, GROUP * tt), F32)
    else:
        kern = _bias_kernel
        out_shape = jax.ShapeDtypeStruct((N_HEADS, tt, L), F32)
    return pl.pallas_call(
        kern,
        in_specs=[
            pl.BlockSpec(memory_space=pltpu.SMEM),
            pl.BlockSpec(memory_space=pltpu.VMEM),
        ],
        out_specs=pl.BlockSpec(memory_space=pltpu.VMEM),
        out_shape=out_shape,
        name="rel_bias_table",
    )(rel_bias, jnp.asarray(bkt))


def _attn_sample_kernel(sinks_ref, x_ref, kp_ref, kc_ref, vp_ref, vc_ref, bias_ref, norm_ref, wq_ref,
                        bq_ref, qn_ref, wo_ref, bo_ref, y_ref, q_ref, o_ref, *, nb, tt):
    m = nb * tt
    L = WINDOW + tt
    sb = SEQS_PER_STEP
    x = x_ref[...].reshape(m, D_MODEL)
    q = _mm(_rmsnorm(x, norm_ref[...]), wq_ref[...]) + bq_ref[...]
    qgain = qn_ref[...] * (HEAD_DIM ** -0.5)
    for h in range(N_HEADS):
        hs = slice(h * HEAD_DIM, (h + 1) * HEAD_DIM)
        qh = q[:, hs]
        q_ref[:, hs] = qh * lax.rsqrt(jnp.mean(qh * qh, axis=-1, keepdims=True) + EPS) * qgain
    sinks = [jnp.concatenate([jnp.full((tt, 1), sinks_ref[g * GROUP + hh], F32)
                              for hh in range(GROUP)], axis=0)[None] for g in range(N_KV_HEADS)]

    def seqs(i, carry):
        n0 = pl.multiple_of(i * sb, sb)
        rows = pl.ds(pl.multiple_of(i * (sb * tt), sb * tt), sb * tt)
        qblk = q_ref[rows, :]
        kall = jnp.concatenate([kp_ref[pl.ds(n0, sb)], kc_ref[pl.ds(n0, sb)]], axis=1)
        vall = jnp.concatenate([vp_ref[pl.ds(n0, sb)], vc_ref[pl.ds(n0, sb)]], axis=1)
        scores = []
        for g in range(N_KV_HEADS):
            gs = slice(g * HEAD_DIM, (g + 1) * HEAD_DIM)
            q4 = jnp.concatenate(
                [qblk[:, h * HEAD_DIM:(h + 1) * HEAD_DIM].reshape(sb, tt, HEAD_DIM)
                 for h in range(g * GROUP, (g + 1) * GROUP)], axis=1).astype(BF16)
            s = jnp.einsum('bqd,bkd->bqk', q4, kall[:, :, gs].astype(BF16),
                           preferred_element_type=F32)
            scores.append(s + bias_ref[g * GROUP:(g + 1) * GROUP].reshape(1, GROUP * tt, L))
        probs, dens = [], []
        for g in range(N_KV_HEADS):
            s = scores[g]
            mx = jnp.maximum(jnp.max(s, axis=-1, keepdims=True), sinks[g])
            p = jnp.exp(s - mx)
            dens.append(jnp.sum(p, axis=-1, keepdims=True) + jnp.exp(sinks[g] - mx))
            probs.append(p.astype(BF16))
        for g in range(N_KV_HEADS):
            gs = slice(g * HEAD_DIM, (g + 1) * HEAD_DIM)
            o = jnp.einsum('bqk,bkd->bqd', probs[g], vall[:, :, gs].astype(BF16),
                           preferred_element_type=F32) / dens[g]
            for hh in range(GROUP):
                h = g * GROUP + hh
                o_ref[rows, h * HEAD_DIM:(h + 1) * HEAD_DIM] = (
                    o[:, hh * tt:(hh + 1) * tt, :].reshape(sb * tt, HEAD_DIM))
        return carry

    lax.fori_loop(0, nb // sb, seqs, 0)
    y = _mm(o_ref[...], wo_ref[...]) + bo_ref[...]
    y_ref[...] = (x + y).reshape(nb, tt, D_MODEL)


def _attn_prompt_kernel(sinks_ref, x_ref, kp_ref, kc_ref, vtp_ref, vtc_ref, bias_ref, norm_ref,
                        wqt_ref, bq_ref, qn_ref, wo_ref, bo_ref, y_ref, qt_ref, ot_ref, s_ref,
                        *, nb, tt):
    m = nb * tt
    x = x_ref[...].reshape(m, D_MODEL)
    xn = _rmsnorm(x, norm_ref[...]).astype(BF16)
    qt = lax.dot_general(wqt_ref[...], xn, (((1,), (1,)), ((), ())),
                         preferred_element_type=F32) + bq_ref[...]
    qgain = qn_ref[...] * (HEAD_DIM ** -0.5)
    for h in range(N_HEADS):
        hs = slice(h * HEAD_DIM, (h + 1) * HEAD_DIM)
        qh = qt[hs, :]
        ms = jnp.mean(qh * qh, axis=0, keepdims=True)
        qt_ref[hs, :] = (qh * lax.rsqrt(ms + EPS) * qgain).astype(BF16)

    head_of_lane = lax.broadcasted_iota(jnp.int32, (1, GROUP * tt), 1) // tt
    sinks = []
    for g in range(N_KV_HEADS):
        sink = jnp.full((1, GROUP * tt), sinks_ref[g * GROUP], F32)
        for hh in range(1, GROUP):
            sink = jnp.where(head_of_lane == hh, sinks_ref[g * GROUP + hh], sink)
        sinks.append(sink)

    def scores(g, n):
        gs = slice(g * HEAD_DIM, (g + 1) * HEAD_DIM)
        ts = slice(n * tt, (n + 1) * tt)
        kk = jnp.concatenate([kp_ref[n, :, gs], kc_ref[n, :, gs]], axis=0)
        q4 = jnp.concatenate([qt_ref[h * HEAD_DIM:(h + 1) * HEAD_DIM, ts]
                              for h in range(g * GROUP, (g + 1) * GROUP)], axis=1)
        return jnp.dot(kk, q4, preferred_element_type=F32)

    pairs = [(g, n) for g in range(N_KV_HEADS) for n in range(nb)]
    for idx, (g, n) in enumerate(pairs):
        s_ref[idx] = scores(g, n) + bias_ref[0, g]
    for idx, (g, n) in enumerate(pairs):
        gs = slice(g * HEAD_DIM, (g + 1) * HEAD_DIM)
        ts = slice(n * tt, (n + 1) * tt)
        s = s_ref[idx]
        mx = jnp.maximum(jnp.max(s, axis=0, keepdims=True), sinks[g])
        p = jnp.exp(s - mx)
        den = jnp.sum(p, axis=0, keepdims=True) + jnp.exp(sinks[g] - mx)
        vt = jnp.concatenate([vtp_ref[n, gs, :], vtc_ref[n, gs, :]], axis=1)
        o = jnp.dot(vt, p.astype(BF16), preferred_element_type=F32) / den
        for hh in range(GROUP):
            h = g * GROUP + hh
            ot_ref[h * HEAD_DIM:(h + 1) * HEAD_DIM, ts] = o[:, hh * tt:(hh + 1) * tt]
    y = _mm(ot_ref[...].T, wo_ref[...]) + bo_ref[...]
    y_ref[...] = (x + y).reshape(nb, tt, D_MODEL)


def _const_spec3(shape, layer=None):
    nd = len(shape)
    if layer is None:
        return pl.BlockSpec(shape, lambda b, j, s: (0,) * nd, pipeline_mode=pl.Buffered(1))
    return pl.BlockSpec((None,) + shape, lambda b, j, s: (layer,) + (0,) * nd,
                        pipeline_mode=pl.Buffered(1))


def _attn_sample_block(x, k_win, k_new, v_win, v_new, bias, sinks, norm, w_q, b_q, q_norm, w_o, b_o,
                       *, nb, layer):
    B, tt, _ = x.shape
    L = WINDOW + tt
    assert nb % SEQS_PER_STEP == 0 and tt % SUBLANES == 0
    kern = functools.partial(_attn_sample_kernel, nb=nb, tt=tt)
    seq_map = lambda b, j, s: (b, 0, 0)
    grid_spec = pltpu.PrefetchScalarGridSpec(
        num_scalar_prefetch=1,
        grid=(B // nb, 1),
        in_specs=[
            pl.BlockSpec((nb, tt, D_MODEL), seq_map),
            pl.BlockSpec((nb, WINDOW, HKV), seq_map),
            pl.BlockSpec((nb, tt, HKV), seq_map),
            pl.BlockSpec((nb, WINDOW, HKV), seq_map),
            pl.BlockSpec((nb, tt, HKV), seq_map),
            _const_spec3((N_HEADS, tt, L)),
            _const_spec3((1, D_MODEL)),
            _const_spec3((D_MODEL, HQ), layer),
            _const_spec3((1, HQ)),
            _const_spec3((1, HEAD_DIM)),
            _const_spec3((HQ, D_MODEL), layer),
            _const_spec3((1, D_MODEL)),
        ],
        out_specs=pl.BlockSpec((nb, tt, D_MODEL), seq_map),
        scratch_shapes=[
            pltpu.VMEM((nb * tt, HQ), F32),
            pltpu.VMEM((nb * tt, HQ), F32),
        ],
    )
    return pl.pallas_call(
        kern,
        grid_spec=grid_spec,
        out_shape=jax.ShapeDtypeStruct((B, tt, D_MODEL), F32),
        compiler_params=_params(),
        name="swa_sample",
    )(sinks, x, k_win, k_new, v_win, v_new, bias, norm.reshape(1, -1), w_q, b_q.reshape(1, -1),
      q_norm.reshape(1, -1), w_o, b_o.reshape(1, -1))


def _attn_prompt_block(x, kb, vt, bias_t, sinks, norm, w_qt, b_q, q_norm, w_o, b_o, *, nb, layer):
    B, T, _ = x.shape
    tt = WINDOW
    L = WINDOW + tt
    kern = functools.partial(_attn_prompt_kernel, nb=nb, tt=tt)
    cur_map = lambda b, j, s: (b, j, 0)
    prev_map = lambda b, j, s: (b, jnp.maximum(j - 1, 0), 0)
    grid_spec = pltpu.PrefetchScalarGridSpec(
        num_scalar_prefetch=1,
        grid=(B // nb, T // tt),
        in_specs=[
            pl.BlockSpec((nb, tt, D_MODEL), cur_map),
            pl.BlockSpec((nb, tt, HKV), prev_map),
            pl.BlockSpec((nb, tt, HKV), cur_map),
            pl.BlockSpec((nb, HKV, tt), lambda b, j, s: (b, 0, jnp.maximum(j - 1, 0))),
            pl.BlockSpec((nb, HKV, tt), lambda b, j, s: (b, 0, j)),
            pl.BlockSpec((1, N_KV_HEADS, L, GROUP * tt),
                         lambda b, j, s: (jnp.where(j == 0, 1, 0), 0, 0, 0)),
            _const_spec3((1, D_MODEL)),
            _const_spec3((HQ, D_MODEL), layer),
            _const_spec3((HQ, 1)),
            _const_spec3((HEAD_DIM, 1)),
            _const_spec3((HQ, D_MODEL), layer),
            _const_spec3((1, D_MODEL)),
        ],
        out_specs=pl.BlockSpec((nb, tt, D_MODEL), cur_map),
        scratch_shapes=[
            pltpu.VMEM((HQ, nb * tt), BF16),
            pltpu.VMEM((HQ, nb * tt), F32),
            pltpu.VMEM((N_KV_HEADS * nb, L, GROUP * tt), F32),
        ],
    )
    return pl.pallas_call(
        kern,
        grid_spec=grid_spec,
        out_shape=jax.ShapeDtypeStruct((B, T, D_MODEL), F32),
        compiler_params=_params(),
        name="swa_prompt",
    )(sinks, x, kb, kb, vt, vt, bias_t, norm.reshape(1, -1), w_qt, b_q.reshape(-1, 1),
      q_norm.reshape(-1, 1), w_o, b_o.reshape(1, -1))


def _trunk(x, lru_h, lru_conv, ffn_conv, win_k, win_v, P, bias, *, nb, tt, prompt):
    B, T, _ = x.shape
    new_h, new_c, new_f = [], [], []
    k = v = kb = vt = None
    for layer in range(DEPTH):
        if layer < N_A:
            i = layer
            x, h_i, c_i = _lru_block(x, lru_h[i], lru_conv[i], P['a_norm'][i], P['a_w_in'],
                                     P['a_conv_w'][i], P['a_conv_b'][i], P['a_gate_r_w'],
                                     P['a_gate_r_b'][i], P['a_gate_i_w'], P['a_gate_i_b'][i],
                                     P['a_lambda'][i], P['a_w_out'], nb=nb, tt=tt, layer=i)
            new_h.append(h_i.reshape(B, D_RNN))
            new_c.append(c_i)
        else:
            j = layer - N_A
            if prompt:
                x = _attn_prompt_block(x, kb, vt, bias, P['sinks'][j], P['b_norm'][j], P['w_qt'],
                                       P['b_q'][j], P['q_norm'][j], P['w_o'], P['b_o'][j],
                                       nb=nb, layer=j)
            else:
                x = _attn_sample_block(x, win_k, k, win_v, v, bias, P['sinks'][j], P['b_norm'][j],
                                       P['w_q'], P['b_q'][j], P['q_norm'][j], P['w_o'], P['b_o'][j],
                                       nb=nb, layer=j)
        x, f_i = _conv_ffn(x, ffn_conv, P['f_norm'][layer], P['f_w_up'],
                           P['f_conv_w'][layer], P['f_conv_b'][layer], P['f_w_down'],
                           nb=nb, tt=tt, layer=layer)
        new_f.append(f_i)
        if layer == N_A - 1:
            kv = _shared_kv(x, P['kv_norm'], P['w_kv'], P['b_kv'], P['k_norm'], nb=nb, tt=tt,
                            key_major_copies=prompt)
            if prompt:
                k, v, kb, vt = kv
            else:
                k, v = kv
    if prompt:
        new_k, new_v = k[:, -WINDOW:], v[:, -WINDOW:]
    else:
        new_k = jnp.concatenate([win_k, k], axis=1)[:, -WINDOW:]
        new_v = jnp.concatenate([win_v, v], axis=1)[:, -WINDOW:]
    shape4 = (B, WINDOW, N_KV_HEADS, HEAD_DIM)
    return (x, jnp.stack(new_h), jnp.stack(new_c), jnp.stack(new_f),
            new_k.reshape(shape4), new_v.reshape(shape4))


def kernel(x_prompt, x_sample, state_lru_h, state_lru_conv, state_ffn_conv, cache_k_win, cache_v_win,
           a_norm, a_w_in, a_conv_w, a_conv_b, a_gate_r_w, a_gate_r_b, a_gate_i_w, a_gate_i_b,
           a_lambda, a_w_out, kv_norm, w_kv, b_kv, k_norm, b_norm, w_q, b_q, q_norm, sinks,
           w_o, b_o, rel_bias, f_norm, f_w_up, f_conv_w, f_conv_b, f_w_down):
    bf = lambda w: w.astype(BF16)
    P = dict(a_norm=a_norm, a_w_in=bf(a_w_in), a_conv_w=a_conv_w, a_conv_b=a_conv_b,
             a_gate_r_w=bf(a_gate_r_w), a_gate_r_b=a_gate_r_b, a_gate_i_w=bf(a_gate_i_w),
             a_gate_i_b=a_gate_i_b, a_lambda=a_lambda, a_w_out=bf(a_w_out), kv_norm=kv_norm,
             w_kv=bf(w_kv), b_kv=b_kv, k_norm=k_norm, b_norm=b_norm, w_q=bf(w_q), b_q=b_q,
             q_norm=q_norm, sinks=sinks, w_o=bf(w_o), b_o=b_o, f_norm=f_norm, f_w_up=bf(f_w_up),
             f_conv_w=f_conv_w, f_conv_b=f_conv_b, f_w_down=bf(f_w_down))
    P['w_qt'] = jnp.swapaxes(P['w_q'], 1, 2)
    B, S, _ = x_prompt.shape
    DB, DT, _ = x_sample.shape

    zero_h = jnp.zeros((N_A, B, D_RNN), F32)
    zero_c = jnp.zeros((N_A, B, CONV_W - 1, D_RNN), F32)
    zero_f = jnp.zeros((DEPTH, B, FFN_CONV_W - 1, D_FF), F32)
    out_p = _trunk(x_prompt, zero_h, zero_c, zero_f, None, None, P,
                   _rel_bias_table(rel_bias, WINDOW, keys_on_rows=True), nb=B, tt=WINDOW,
                   prompt=True)

    win_k = cache_k_win.reshape(DB, WINDOW, HKV)
    win_v = cache_v_win.reshape(DB, WINDOW, HKV)
    out_s = _trunk(x_sample, state_lru_h, state_lru_conv, state_ffn_conv, win_k, win_v, P,
                   _rel_bias_table(rel_bias, DT, keys_on_rows=False), nb=DB // 2, tt=DT,
                   prompt=False)
    return (out_p[0], out_s[0]) + out_p[1:] + out_s[1:]
```

```python
import functools
import math

import numpy as np
import jax
import jax.numpy as jnp
from jax import lax
from jax.experimental import pallas as pl
from jax.experimental.pallas import tpu as pltpu

D_MODEL = 1024
DEPTH = 4
N_A = DEPTH // 2
D_RNN = D_MODEL
N_LRU_BLOCKS = 4
LRU_BW = D_RNN // N_LRU_BLOCKS
CONV_W = 4
C_GATE = 8.0
N_HEADS = 16
N_KV_HEADS = 4
HEAD_DIM = 64
GROUP = N_HEADS // N_KV_HEADS
HQ = N_HEADS * HEAD_DIM
HKV = N_KV_HEADS * HEAD_DIM
WINDOW = 128
N_BUCKETS = 32
MAX_DISTANCE = 128
D_FF = 3 * D_MODEL
FFN_CONV_W = 3
EPS = 1e-6
NEG_INF = -1e30
LOG2_E = math.log2(math.e)

F32 = jnp.float32
BF16 = jnp.bfloat16

SUBLANES = 8
FF_CHUNK = 256
FFN_ROWS = 512
SEQS_PER_STEP = 8
VMEM_LIMIT_BYTES = 56 * 1024 * 1024


def _rmsnorm(x, g):
    return x * lax.rsqrt(jnp.mean(x * x, axis=-1, keepdims=True) + EPS) * g


def _softplus(x):
    return jnp.maximum(x, 0.0) + jnp.log(1.0 + jnp.exp(-jnp.abs(x)))


def _mm(a, b):
    return jnp.dot(a.astype(BF16), b, preferred_element_type=F32)


def _causal_dwconv(x, last_group, w, b):
    nb, tt, c = x.shape
    taps = w.shape[0]
    groups = tt // SUBLANES
    x4 = x.reshape(nb, groups, SUBLANES, c)
    t = lax.broadcasted_iota(jnp.int32, (1, 1, SUBLANES, c), 2)
    tap = lambda k: w[k:k + 1, :].reshape(1, 1, 1, c)
    y = b.reshape(1, 1, 1, c) + tap(taps - 1) * x4
    for s in range(1, taps):
        cur = pltpu.roll(x4.reshape(nb * groups, SUBLANES, c), s, axis=1)
        cur = cur.reshape(nb, groups, SUBLANES, c)
        before = pltpu.roll(last_group, s, axis=1)[:, None]
        if groups > 1:
            before = jnp.concatenate([before, cur[:, :groups - 1]], axis=1)
        y = y + tap(taps - 1 - s) * jnp.where(t >= s, cur, before)
    return y.reshape(nb, tt, c), x4[:, groups - 1]


def _const_spec(shape, layer=None):
    nd = len(shape)
    if layer is None:
        return pl.BlockSpec(shape, lambda b, j: (0,) * nd, pipeline_mode=pl.Buffered(1))
    return pl.BlockSpec((None,) + shape, lambda b, j: (layer,) + (0,) * nd,
                        pipeline_mode=pl.Buffered(1))


def _params(**kwargs):
    return pltpu.CompilerParams(dimension_semantics=("arbitrary", "arbitrary"),
                                vmem_limit_bytes=VMEM_LIMIT_BYTES, **kwargs)


def _lru_tile(load_x, store_y, norm_ref, win_ref, cw_ref, cb_ref, grw_ref, grb_ref, giw_ref, gib_ref,
              lam_ref, wout_ref, hlast_ref, cnew_ref, last_ref, h_ref, mid_ref, *, nb, tt):
    m = nb * tt
    tail = SUBLANES - (CONV_W - 1)
    x = load_x()
    xn = _rmsnorm(x, norm_ref[...]).astype(BF16)
    decay = (-C_GATE * LOG2_E) * _softplus(-lam_ref[...])
    for n in range(N_LRU_BLOCKS):
        cs = slice(n * LRU_BW, (n + 1) * LRU_BW)
        xb = jnp.dot(xn, win_ref[:, D_RNN + n * LRU_BW:D_RNN + (n + 1) * LRU_BW],
                     preferred_element_type=F32)
        xc, last = _causal_dwconv(xb.reshape(nb, tt, LRU_BW), last_ref[:, :, cs],
                                  cw_ref[:, cs], cb_ref[:, cs])
        last_ref[:, :, cs] = last
        xc = xc.reshape(m, LRU_BW)
        xcb = xc.astype(BF16)
        r = jax.nn.sigmoid(jnp.dot(xcb, grw_ref[n], preferred_element_type=F32) + grb_ref[:, cs])
        i = jax.nn.sigmoid(jnp.dot(xcb, giw_ref[n], preferred_element_type=F32) + gib_ref[:, cs])
        a = jnp.exp2(r * decay[:, cs])
        b = jnp.exp2(0.5 * jnp.log2(1.0 - a * a)) * (i * xc)
        hs, h_last = _linear_scan(a.reshape(nb, tt, LRU_BW), b.reshape(nb, tt, LRU_BW),
                                  h_ref[:, :, cs])
        h_ref[:, :, cs] = h_last
        gate = jnp.dot(xn, win_ref[:, cs], preferred_element_type=F32)
        mid_ref[:, cs] = (jax.nn.gelu(gate) * hs.reshape(m, LRU_BW)).astype(BF16)
    cnew_ref[...] = last_ref[:, tail:, :]
    hlast_ref[...] = h_ref[...]
    store_y(x + jnp.dot(mid_ref[...], wout_ref[...], preferred_element_type=F32))


def _linear_scan(a, b, h0):
    nb, tt, c = a.shape
    groups = tt // SUBLANES
    a = a.reshape(nb * groups, SUBLANES, c)
    b = b.reshape(nb * groups, SUBLANES, c)
    t = lax.broadcasted_iota(jnp.int32, a.shape, 1)
    d = 1
    while d < SUBLANES:
        keep = t >= d
        b = jnp.where(keep, a * pltpu.roll(b, d, axis=1) + b, b)
        a = jnp.where(keep, a * pltpu.roll(a, d, axis=1), a)
        d *= 2
    a = a.reshape(nb, groups, SUBLANES, c)
    b = b.reshape(nb, groups, SUBLANES, c)
    h = h0
    out = []
    for g in range(groups):
        hg = a[:, g] * h + b[:, g]
        out.append(hg)
        h = hg[:, SUBLANES - 1:SUBLANES, :]
    return jnp.concatenate(out, axis=1), h


_LRU_WEIGHTS = 10
_FFN_WEIGHTS = 5


def _init_conv_rows(last_ref, buf_ref, taps):
    nb, _, c = last_ref.shape
    tail = SUBLANES - (taps - 1)
    last_ref[:, :tail, :] = jnp.zeros((nb, tail, c), F32)
    last_ref[:, tail:, :] = buf_ref[...]


def _lru_kernel(x_ref, h0_ref, cbuf_ref, *refs, nb, tt):
    lru_w = refs[:_LRU_WEIGHTS]
    y_ref, hlast_ref, cnew_ref, last_ref, h_ref, mid_ref = refs[_LRU_WEIGHTS:]
    m = nb * tt

    @pl.when(pl.program_id(1) == 0)
    def _():
        _init_conv_rows(last_ref, cbuf_ref, CONV_W)
        h_ref[...] = h0_ref[...]

    def store_out(y):
        y_ref[...] = y.reshape(nb, tt, D_MODEL)

    _lru_tile(lambda: x_ref[...].reshape(m, D_MODEL), store_out, *lru_w,
              hlast_ref, cnew_ref, last_ref, h_ref, mid_ref, nb=nb, tt=tt)


def _lru_block(x, h0, cbuf, norm, w_in, cw, cb, grw, grb, giw, gib, lam, w_out, *, nb, tt, layer):
    B, T, _ = x.shape
    kern = functools.partial(_lru_kernel, nb=nb, tt=tt)
    row = lambda v: v.reshape(1, -1)
    return pl.pallas_call(
        kern,
        grid=(B // nb, T // tt),
        in_specs=[
            pl.BlockSpec((nb, tt, D_MODEL), lambda b, j: (b, j, 0)),
            pl.BlockSpec((nb, 1, D_RNN), lambda b, j: (b, 0, 0)),
            pl.BlockSpec((nb, CONV_W - 1, D_RNN), lambda b, j: (b, 0, 0)),
            _const_spec((1, D_MODEL)),
            _const_spec((D_MODEL, 2 * D_RNN), layer),
            _const_spec((CONV_W, D_RNN)),
            _const_spec((1, D_RNN)),
            _const_spec((N_LRU_BLOCKS, LRU_BW, LRU_BW), layer),
            _const_spec((1, D_RNN)),
            _const_spec((N_LRU_BLOCKS, LRU_BW, LRU_BW), layer),
            _const_spec((1, D_RNN)),
            _const_spec((1, D_RNN)),
            _const_spec((D_RNN, D_MODEL), layer),
        ],
        out_specs=[
            pl.BlockSpec((nb, tt, D_MODEL), lambda b, j: (b, j, 0)),
            pl.BlockSpec((nb, 1, D_RNN), lambda b, j: (b, 0, 0)),
            pl.BlockSpec((nb, CONV_W - 1, D_RNN), lambda b, j: (b, 0, 0)),
        ],
        out_shape=[
            jax.ShapeDtypeStruct((B, T, D_MODEL), F32),
            jax.ShapeDtypeStruct((B, 1, D_RNN), F32),
            jax.ShapeDtypeStruct((B, CONV_W - 1, D_RNN), F32),
        ],
        scratch_shapes=[
            pltpu.VMEM((nb, SUBLANES, D_RNN), F32),
            pltpu.VMEM((nb, 1, D_RNN), F32),
            pltpu.VMEM((nb * tt, D_RNN), BF16),
        ],
        compiler_params=_params(),
        name="rglru_block",
    )(x, h0.reshape(B, 1, D_RNN), cbuf, row(norm), w_in, cw, row(cb), grw, row(grb), giw, row(gib), row(lam), w_out)


def _ffn_kernel(x_ref, buf_ref, *refs, nb, tt):
    ffn_w = refs[:_FFN_WEIGHTS]
    y_ref, bnew_ref, last_ref, ext_ref, mid_ref = refs[_FFN_WEIGHTS:]
    m = nb * tt

    @pl.when(pl.program_id(1) == 0)
    def _():
        _init_conv_rows(last_ref, buf_ref, FFN_CONV_W)

    def store_out(y):
        y_ref[...] = y.reshape(nb, tt, D_MODEL)

    _ffn_tile(lambda: x_ref[...].reshape(m, D_MODEL), store_out, *ffn_w, bnew_ref,
              last_ref, ext_ref, mid_ref, nb=nb, tt=tt)


def _ffn_tile(load_x, store_y, norm_ref, wup_ref, cw_ref, cb_ref, wdown_ref, bnew_ref,
              last_ref, ext_ref, mid_ref, *, nb, tt):
    m = nb * tt
    tail = SUBLANES - (FFN_CONV_W - 1)
    x = load_x()
    xn = _rmsnorm(x, norm_ref[...]).astype(BF16)
    for c in range(D_FF // FF_CHUNK):
        cs = slice(c * FF_CHUNK, (c + 1) * FF_CHUNK)
        g = jnp.dot(xn, wup_ref[:, cs], preferred_element_type=F32).reshape(nb, tt, FF_CHUNK)
        v = jnp.dot(xn, wup_ref[:, D_FF + c * FF_CHUNK:D_FF + (c + 1) * FF_CHUNK],
                    preferred_element_type=F32)
        if tt == SUBLANES:
            gc, last = _causal_dwconv(g, last_ref[:, :, cs], cw_ref[:, cs], cb_ref[:, cs])
            last_ref[:, :, cs] = last
        else:
            ext_ref[:, tail:SUBLANES, :] = last_ref[:, tail:, cs]
            ext_ref[:, SUBLANES:, :] = g
            gc = cb_ref[:, cs].reshape(1, 1, FF_CHUNK)
            for k in range(FFN_CONV_W):
                gc = gc + (cw_ref[k:k + 1, cs].reshape(1, 1, FF_CHUNK)
                           * ext_ref[:, tail + k:tail + k + tt, :])
            last_ref[:, tail:, cs] = ext_ref[:, tail + tt:SUBLANES + tt, :]
        mid_ref[:, cs] = (jax.nn.gelu(gc).reshape(m, FF_CHUNK) * v).astype(BF16)
    bnew_ref[...] = last_ref[:, tail:, :]
    store_y(x + jnp.dot(mid_ref[...], wdown_ref[...], preferred_element_type=F32))


def _conv_ffn(x, buf, norm, w_up, cw, cb, w_down, *, nb, tt, layer):
    B, T, _ = x.shape
    kern = functools.partial(_ffn_kernel, nb=nb, tt=tt)
    return pl.pallas_call(
        kern,
        grid=(B // nb, T // tt),
        in_specs=[
            pl.BlockSpec((nb, tt, D_MODEL), lambda b, j: (b, j, 0)),
            pl.BlockSpec((None, nb, FFN_CONV_W - 1, D_FF), lambda b, j: (layer, b, 0, 0)),
            _const_spec((1, D_MODEL)),
            _const_spec((D_MODEL, 2 * D_FF), layer),
            _const_spec((FFN_CONV_W, D_FF)),
            _const_spec((1, D_FF)),
            _const_spec((D_FF, D_MODEL), layer),
        ],
        out_specs=[
            pl.BlockSpec((nb, tt, D_MODEL), lambda b, j: (b, j, 0)),
            pl.BlockSpec((nb, FFN_CONV_W - 1, D_FF), lambda b, j: (b, 0, 0)),
        ],
        out_shape=[
            jax.ShapeDtypeStruct((B, T, D_MODEL), F32),
            jax.ShapeDtypeStruct((B, FFN_CONV_W - 1, D_FF), F32),
        ],
        scratch_shapes=[
            pltpu.VMEM((nb, SUBLANES, D_FF), F32),
            pltpu.VMEM((nb, SUBLANES + tt, FF_CHUNK), F32),
            pltpu.VMEM((nb * tt, D_FF), BF16),
        ],
        compiler_params=_params(),
        name="conv_ffn",
    )(x, buf, norm.reshape(1, -1), w_up, cw, cb.reshape(1, -1), w_down)


def _kv_kernel(x_ref, norm_ref, w_ref, b_ref, kn_ref, *rest, nb, tt, key_major_copies):
    m = nb * tt
    x = x_ref[...].reshape(m, D_MODEL)
    xn = _rmsnorm(x, norm_ref[...]).astype(BF16)
    kv = jnp.dot(xn, w_ref[...], preferred_element_type=F32) + b_ref[...]
    k = kv[:, :HKV]
    if key_major_copies:
        wvt_ref, bvt_ref, k_ref, v_ref, kb_ref, vt_ref = rest
        vt = lax.dot_general(wvt_ref[...], xn, (((1,), (1,)), ((), ())),
                             preferred_element_type=F32) + bvt_ref[...]
        for n in range(nb):
            vt_ref[n] = vt[:, n * tt:(n + 1) * tt].astype(BF16)
    else:
        k_ref, v_ref = rest
    rows = lax.broadcasted_iota(jnp.int32, (HKV, HKV), 0) // HEAD_DIM
    cols = lax.broadcasted_iota(jnp.int32, (HKV, HKV), 1) // HEAD_DIM
    avg = jnp.where(rows == cols, 1.0 / HEAD_DIM, 0.0).astype(BF16)
    sq = k * k
    hi = sq.astype(BF16)
    lo = (sq - hi.astype(F32)).astype(BF16)
    ms = (jnp.dot(hi, avg, preferred_element_type=F32)
          + jnp.dot(lo, avg, preferred_element_type=F32))
    kn = (k * lax.rsqrt(ms + EPS) * kn_ref[...]).reshape(nb, tt, HKV)
    k_ref[...] = kn
    v_ref[...] = kv[:, HKV:].reshape(nb, tt, HKV)
    if key_major_copies:
        kb_ref[...] = kn.astype(BF16)


def _shared_kv(x, norm, w_kv, b_kv, k_norm, *, nb, tt, key_major_copies):
    B, T, _ = x.shape
    kern = functools.partial(_kv_kernel, nb=nb, tt=tt, key_major_copies=key_major_copies)
    tok_spec = pl.BlockSpec((nb, tt, HKV), lambda b, j: (b, j, 0))
    in_specs = [
        pl.BlockSpec((nb, tt, D_MODEL), lambda b, j: (b, j, 0)),
        _const_spec((1, D_MODEL)),
        _const_spec((D_MODEL, 2 * HKV)),
        _const_spec((1, 2 * HKV)),
        _const_spec((1, HKV)),
    ]
    args = [x, norm.reshape(1, -1), w_kv, b_kv.reshape(1, -1),
            jnp.tile(k_norm, N_KV_HEADS).reshape(1, -1)]
    out_specs = [tok_spec, tok_spec]
    out_shape = [jax.ShapeDtypeStruct((B, T, HKV), F32), jax.ShapeDtypeStruct((B, T, HKV), F32)]
    if key_major_copies:
        in_specs += [_const_spec((HKV, D_MODEL)), _const_spec((HKV, 1))]
        args += [w_kv[:, HKV:].T, b_kv[HKV:].reshape(-1, 1)]
        out_specs += [tok_spec, pl.BlockSpec((nb, HKV, tt), lambda b, j: (b, 0, j))]
        out_shape += [jax.ShapeDtypeStruct((B, T, HKV), BF16),
                      jax.ShapeDtypeStruct((B, HKV, T), BF16)]
    return pl.pallas_call(
        kern,
        grid=(B // nb, T // tt),
        in_specs=in_specs,
        out_specs=out_specs,
        out_shape=out_shape,
        compiler_params=_params(),
        name="shared_kv",
    )(*args)


def _rel_buckets(tt):
    qi = np.arange(tt)[:, None]
    sj = np.arange(WINDOW + tt)[None, :]
    dist = qi + WINDOW - sj
    max_exact = N_BUCKETS // 2
    d = np.maximum(dist, 0)
    df = np.maximum(d, 1).astype(np.float32)
    large = max_exact + (np.log(df / max_exact) / math.log(MAX_DISTANCE / max_exact)
                         * (N_BUCKETS - max_exact)).astype(np.int32)
    large = np.minimum(large, N_BUCKETS - 1)
    bucket = np.where(d < max_exact, d, large)
    valid = (dist >= 0) & (dist < WINDOW)
    return np.where(valid, bucket, -1).astype(np.int32)


def _bias_of_buckets(rb_ref, bkt, h):
    acc = jnp.full(bkt.shape, NEG_INF, F32)
    for c in range(N_BUCKETS):
        acc = jnp.where(bkt == c, rb_ref[c, h], acc)
    return acc


def _bias_kernel(rb_ref, bkt_ref, out_ref):
    bkt = bkt_ref[...]

    def head(h, carry):
        out_ref[h] = _bias_of_buckets(rb_ref, bkt, h)
        return carry

    lax.fori_loop(0, N_HEADS, head, 0)


def _bias_t_kernel(rb_ref, bkt_ref, out_ref, *, tt):
    bkt = bkt_ref[...]
    before_tile = lax.broadcasted_iota(jnp.int32, bkt.shape, 0) < WINDOW

    def group(g, carry):
        for hh in range(GROUP):
            acc = _bias_of_buckets(rb_ref, bkt, g * GROUP + hh) * LOG2_E
            out_ref[0, g, :, hh * tt:(hh + 1) * tt] = acc
            out_ref[1, g, :, hh * tt:(hh + 1) * tt] = jnp.where(before_tile, NEG_INF, acc)
        return carry

    lax.fori_loop(0, N_KV_HEADS, group, 0)


def _rel_bias_table(rel_bias, tt, *, keys_on_rows):
    L = WINDOW + tt
    bkt = _rel_buckets(tt)
    if keys_on_rows:
        kern = functools.partial(_bias_t_kernel, tt=tt)
        bkt = bkt.T
        out_shape = jax.ShapeDtypeStruct((2, N_KV_HEADS, L, GROUP * tt), F32)
    else:
        kern = _bias_kernel
        out_shape = jax.ShapeDtypeStruct((N_HEADS, tt, L), F32)
    return pl.pallas_call(
        kern,
        in_specs=[
            pl.BlockSpec(memory_space=pltpu.SMEM),
            pl.BlockSpec(memory_space=pltpu.VMEM),
        ],
        out_specs=pl.BlockSpec(memory_space=pltpu.VMEM),
        out_shape=out_shape,
        name="rel_bias_table",
    )(rel_bias, jnp.asarray(bkt))


def _attn_sample_kernel(sinks_ref, x_ref, kp_ref, kc_ref, vp_ref, vc_ref, bias_ref, norm_ref, wq_ref,
                        bq_ref, qn_ref, wo_ref, bo_ref, y_ref, q_ref, o_ref, *, nb, tt):
    m = nb * tt
    L = WINDOW + tt
    sb = SEQS_PER_STEP
    x = x_ref[...].reshape(m, D_MODEL)
    q = _mm(_rmsnorm(x, norm_ref[...]), wq_ref[...]) + bq_ref[...]
    qgain = qn_ref[...] * (HEAD_DIM ** -0.5)
    for h in range(N_HEADS):
        hs = slice(h * HEAD_DIM, (h + 1) * HEAD_DIM)
        qh = q[:, hs]
        q_ref[:, hs] = qh * lax.rsqrt(jnp.mean(qh * qh, axis=-1, keepdims=True) + EPS) * qgain
    sinks = [jnp.concatenate([jnp.full((tt, 1), sinks_ref[g * GROUP + hh], F32)
                              for hh in range(GROUP)], axis=0)[None] for g in range(N_KV_HEADS)]

    def seqs(i, carry):
        n0 = pl.multiple_of(i * sb, sb)
        rows = pl.ds(pl.multiple_of(i * (sb * tt), sb * tt), sb * tt)
        qblk = q_ref[rows, :]
        kall = jnp.concatenate([kp_ref[pl.ds(n0, sb)], kc_ref[pl.ds(n0, sb)]], axis=1)
        vall = jnp.concatenate([vp_ref[pl.ds(n0, sb)], vc_ref[pl.ds(n0, sb)]], axis=1)
        scores = []
        for g in range(N_KV_HEADS):
            gs = slice(g * HEAD_DIM, (g + 1) * HEAD_DIM)
            q4 = jnp.concatenate(
                [qblk[:, h * HEAD_DIM:(h + 1) * HEAD_DIM].reshape(sb, tt, HEAD_DIM)
                 for h in range(g * GROUP, (g + 1) * GROUP)], axis=1).astype(BF16)
            s = jnp.einsum('bqd,bkd->bqk', q4, kall[:, :, gs].astype(BF16),
                           preferred_element_type=F32)
            scores.append(s + bias_ref[g * GROUP:(g + 1) * GROUP].reshape(1, GROUP * tt, L))
        probs, dens = [], []
        for g in range(N_KV_HEADS):
            s = scores[g]
            mx = jnp.maximum(jnp.max(s, axis=-1, keepdims=True), sinks[g])
            p = jnp.exp(s - mx)
            dens.append(jnp.sum(p, axis=-1, keepdims=True) + jnp.exp(sinks[g] - mx))
            probs.append(p.astype(BF16))
        for g in range(N_KV_HEADS):
            gs = slice(g * HEAD_DIM, (g + 1) * HEAD_DIM)
            o = jnp.einsum('bqk,bkd->bqd', probs[g], vall[:, :, gs].astype(BF16),
                           preferred_element_type=F32) / dens[g]
            for hh in range(GROUP):
                h = g * GROUP + hh
                o_ref[rows, h * HEAD_DIM:(h + 1) * HEAD_DIM] = (
                    o[:, hh * tt:(hh + 1) * tt, :].reshape(sb * tt, HEAD_DIM))
        return carry

    lax.fori_loop(0, nb // sb, seqs, 0)
    y = _mm(o_ref[...], wo_ref[...]) + bo_ref[...]
    y_ref[...] = (x + y).reshape(nb, tt, D_MODEL)


def _attn_prompt_kernel(sinks_ref, x_ref, kp_ref, kc_ref, vtp_ref, vtc_ref, bias_ref, norm_ref,
                        wqt_ref, bq_ref, qn_ref, wo_ref, bo_ref, y_ref, qt_ref, ot_ref, s_ref,
                        *, nb, tt):
    m = nb * tt
    x = x_ref[...].reshape(m, D_MODEL)
    xn = _rmsnorm(x, norm_ref[...]).astype(BF16)
    qt = lax.dot_general(wqt_ref[...], xn, (((1,), (1,)), ((), ())),
                         preferred_element_type=F32) + bq_ref[...]
    qgain = qn_ref[...] * (HEAD_DIM ** -0.5 * LOG2_E)
    for h in range(N_HEADS):
        hs = slice(h * HEAD_DIM, (h + 1) * HEAD_DIM)
        qh = qt[hs, :]
        ms = jnp.mean(qh * qh, axis=0, keepdims=True)
        qt_ref[hs, :] = (qh * lax.rsqrt(ms + EPS) * qgain).astype(BF16)

    head_of_lane = lax.broadcasted_iota(jnp.int32, (1, GROUP * tt), 1) // tt
    sinks = []
    for g in range(N_KV_HEADS):
        sink = jnp.full((1, GROUP * tt), sinks_ref[g * GROUP] * LOG2_E, F32)
        for hh in range(1, GROUP):
            sink = jnp.where(head_of_lane == hh, sinks_ref[g * GROUP + hh] * LOG2_E, sink)
        sinks.append(sink)
    ones_rows = jnp.ones((2 * SUBLANES, WINDOW + tt), BF16)

    def scores(g, n):
        gs = slice(g * HEAD_DIM, (g + 1) * HEAD_DIM)
        ts = slice(n * tt, (n + 1) * tt)
        kk = jnp.concatenate([kp_ref[n, :, gs], kc_ref[n, :, gs]], axis=0)
        q4 = jnp.concatenate([qt_ref[h * HEAD_DIM:(h + 1) * HEAD_DIM, ts]
                              for h in range(g * GROUP, (g + 1) * GROUP)], axis=1)
        return jnp.dot(kk, q4, preferred_element_type=F32)

    pairs = [(g, n) for g in range(N_KV_HEADS) for n in range(nb)]
    for idx, (g, n) in enumerate(pairs):
        s_ref[idx] = scores(g, n) + bias_ref[0, g]
    for idx, (g, n) in enumerate(pairs):
        gs = slice(g * HEAD_DIM, (g + 1) * HEAD_DIM)
        ts = slice(n * tt, (n + 1) * tt)
        s = s_ref[idx]
        mx = jnp.maximum(jnp.max(s, axis=0, keepdims=True), sinks[g])
        p = jnp.exp2(s - mx).astype(BF16)
        vt = jnp.concatenate([vtp_ref[n, gs, :], vtc_ref[n, gs, :]], axis=1)
        pv = jnp.dot(jnp.concatenate([vt, ones_rows], axis=0), p,
                     preferred_element_type=F32)
        den = pv[HEAD_DIM:HEAD_DIM + 1, :] + jnp.exp2(sinks[g] - mx)
        o = pv[:HEAD_DIM, :] / den
        for hh in range(GROUP):
            h = g * GROUP + hh
            ot_ref[h * HEAD_DIM:(h + 1) * HEAD_DIM, ts] = o[:, hh * tt:(hh + 1) * tt]
    y = _mm(ot_ref[...].T, wo_ref[...]) + bo_ref[...]
    y_ref[...] = (x + y).reshape(nb, tt, D_MODEL)


def _const_spec3(shape, layer=None):
    nd = len(shape)
    if layer is None:
        return pl.BlockSpec(shape, lambda b, j, s: (0,) * nd, pipeline_mode=pl.Buffered(1))
    return pl.BlockSpec((None,) + shape, lambda b, j, s: (layer,) + (0,) * nd,
                        pipeline_mode=pl.Buffered(1))


def _attn_sample_block(x, k_win, k_new, v_win, v_new, bias, sinks, norm, w_q, b_q, q_norm, w_o, b_o,
                       *, nb, layer):
    B, tt, _ = x.shape
    L = WINDOW + tt
    assert nb % SEQS_PER_STEP == 0 and tt % SUBLANES == 0
    kern = functools.partial(_attn_sample_kernel, nb=nb, tt=tt)
    seq_map = lambda b, j, s: (b, 0, 0)
    grid_spec = pltpu.PrefetchScalarGridSpec(
        num_scalar_prefetch=1,
        grid=(B // nb, 1),
        in_specs=[
            pl.BlockSpec((nb, tt, D_MODEL), seq_map),
            pl.BlockSpec((nb, WINDOW, HKV), seq_map),
            pl.BlockSpec((nb, tt, HKV), seq_map),
            pl.BlockSpec((nb, WINDOW, HKV), seq_map),
            pl.BlockSpec((nb, tt, HKV), seq_map),
            _const_spec3((N_HEADS, tt, L)),
            _const_spec3((1, D_MODEL)),
            _const_spec3((D_MODEL, HQ), layer),
            _const_spec3((1, HQ)),
            _const_spec3((1, HEAD_DIM)),
            _const_spec3((HQ, D_MODEL), layer),
            _const_spec3((1, D_MODEL)),
        ],
        out_specs=pl.BlockSpec((nb, tt, D_MODEL), seq_map),
        scratch_shapes=[
            pltpu.VMEM((nb * tt, HQ), F32),
            pltpu.VMEM((nb * tt, HQ), F32),
        ],
    )
    return pl.pallas_call(
        kern,
        grid_spec=grid_spec,
        out_shape=jax.ShapeDtypeStruct((B, tt, D_MODEL), F32),
        compiler_params=_params(),
        name="swa_sample",
    )(sinks, x, k_win, k_new, v_win, v_new, bias, norm.reshape(1, -1), w_q, b_q.reshape(1, -1),
      q_norm.reshape(1, -1), w_o, b_o.reshape(1, -1))


def _attn_prompt_block(x, kb, vt, bias_t, sinks, norm, w_qt, b_q, q_norm, w_o, b_o, *, nb, layer):
    B, T, _ = x.shape
    tt = WINDOW
    L = WINDOW + tt
    kern = functools.partial(_attn_prompt_kernel, nb=nb, tt=tt)
    cur_map = lambda b, j, s: (b, j, 0)
    prev_map = lambda b, j, s: (b, jnp.maximum(j - 1, 0), 0)
    grid_spec = pltpu.PrefetchScalarGridSpec(
        num_scalar_prefetch=1,
        grid=(B // nb, T // tt),
        in_specs=[
            pl.BlockSpec((nb, tt, D_MODEL), cur_map),
            pl.BlockSpec((nb, tt, HKV), prev_map),
            pl.BlockSpec((nb, tt, HKV), cur_map),
            pl.BlockSpec((nb, HKV, tt), lambda b, j, s: (b, 0, jnp.maximum(j - 1, 0))),
            pl.BlockSpec((nb, HKV, tt), lambda b, j, s: (b, 0, j)),
            pl.BlockSpec((1, N_KV_HEADS, L, GROUP * tt),
                         lambda b, j, s: (jnp.where(j == 0, 1, 0), 0, 0, 0)),
            _const_spec3((1, D_MODEL)),
            _const_spec3((HQ, D_MODEL), layer),
            _const_spec3((HQ, 1)),
            _const_spec3((HEAD_DIM, 1)),
            _const_spec3((HQ, D_MODEL), layer),
            _const_spec3((1, D_MODEL)),
        ],
        out_specs=pl.BlockSpec((nb, tt, D_MODEL), cur_map),
        scratch_shapes=[
            pltpu.VMEM((HQ, nb * tt), BF16),
            pltpu.VMEM((HQ, nb * tt), F32),
            pltpu.VMEM((N_KV_HEADS * nb, L, GROUP * tt), F32),
        ],
    )
    return pl.pallas_call(
        kern,
        grid_spec=grid_spec,
        out_shape=jax.ShapeDtypeStruct((B, T, D_MODEL), F32),
        compiler_params=_params(),
        name="swa_prompt",
    )(sinks, x, kb, kb, vt, vt, bias_t, norm.reshape(1, -1), w_qt, b_q.reshape(-1, 1),
      q_norm.reshape(-1, 1), w_o, b_o.reshape(1, -1))


def _trunk(x, lru_h, lru_conv, ffn_conv, win_k, win_v, P, bias, *, nb, tt, prompt):
    B, T, _ = x.shape
    new_h, new_c, new_f = [], [], []
    k = v = kb = vt = None
    for layer in range(DEPTH):
        if layer < N_A:
            i = layer
            lru_w = (P['a_norm'][i], P['a_w_in'], P['a_conv_w'][i], P['a_conv_b'][i],
                     P['a_gate_r_w'], P['a_gate_r_b'][i], P['a_gate_i_w'], P['a_gate_i_b'][i],
                     P['a_lambda'][i], P['a_w_out'])
            ffn_w = (P['f_norm'][layer], P['f_w_up'], P['f_conv_w'][layer], P['f_conv_b'][layer],
                     P['f_w_down'])
            x, h_i, c_i = _lru_block(x, lru_h[i], lru_conv[i], *lru_w, nb=nb, tt=tt, layer=i)
            x, f_i = _conv_ffn(x, ffn_conv, *ffn_w, nb=nb, tt=tt, layer=layer)
            new_h.append(h_i.reshape(B, D_RNN))
            new_c.append(c_i)
        else:
            j = layer - N_A
            if prompt:
                x = _attn_prompt_block(x, kb, vt, bias, P['sinks'][j], P['b_norm'][j], P['w_qt'],
                                       P['b_q'][j], P['q_norm'][j], P['w_o'], P['b_o'][j],
                                       nb=nb, layer=j)
            else:
                x = _attn_sample_block(x, win_k, k, win_v, v, bias, P['sinks'][j], P['b_norm'][j],
                                       P['w_q'], P['b_q'][j], P['q_norm'][j], P['w_o'], P['b_o'][j],
                                       nb=nb, layer=j)
            x, f_i = _conv_ffn(x, ffn_conv, P['f_norm'][layer], P['f_w_up'],
                               P['f_conv_w'][layer], P['f_conv_b'][layer], P['f_w_down'],
                               nb=nb, tt=min(T, FFN_ROWS // nb), layer=layer)
        new_f.append(f_i)
        if layer == N_A - 1:
            kv = _shared_kv(x, P['kv_norm'], P['w_kv'], P['b_kv'], P['k_norm'], nb=nb, tt=tt,
                            key_major_copies=prompt)
            if prompt:
                k, v, kb, vt = kv
            else:
                k, v = kv
    if prompt:
        new_k, new_v = k[:, -WINDOW:], v[:, -WINDOW:]
    else:
        new_k = jnp.concatenate([win_k, k], axis=1)[:, -WINDOW:]
        new_v = jnp.concatenate([win_v, v], axis=1)[:, -WINDOW:]
    shape4 = (B, WINDOW, N_KV_HEADS, HEAD_DIM)
    return (x, jnp.stack(new_h), jnp.stack(new_c), jnp.stack(new_f),
            new_k.reshape(shape4), new_v.reshape(shape4))


def kernel(x_prompt, x_sample, state_lru_h, state_lru_conv, state_ffn_conv, cache_k_win, cache_v_win,
           a_norm, a_w_in, a_conv_w, a_conv_b, a_gate_r_w, a_gate_r_b, a_gate_i_w, a_gate_i_b,
           a_lambda, a_w_out, kv_norm, w_kv, b_kv, k_norm, b_norm, w_q, b_q, q_norm, sinks,
           w_o, b_o, rel_bias, f_norm, f_w_up, f_conv_w, f_conv_b, f_w_down):
    bf = lambda w: w.astype(BF16)
    P = dict(a_norm=a_norm, a_w_in=bf(a_w_in), a_conv_w=a_conv_w, a_conv_b=a_conv_b,
             a_gate_r_w=bf(a_gate_r_w), a_gate_r_b=a_gate_r_b, a_gate_i_w=bf(a_gate_i_w),
             a_gate_i_b=a_gate_i_b, a_lambda=a_lambda, a_w_out=bf(a_w_out), kv_norm=kv_norm,
             w_kv=bf(w_kv), b_kv=b_kv, k_norm=k_norm, b_norm=b_norm, w_q=bf(w_q), b_q=b_q,
             q_norm=q_norm, sinks=sinks, w_o=bf(w_o), b_o=b_o, f_norm=f_norm, f_w_up=bf(f_w_up),
             f_conv_w=f_conv_w, f_conv_b=f_conv_b, f_w_down=bf(f_w_down))
    P['w_qt'] = jnp.swapaxes(P['w_q'], 1, 2)
    B, S, _ = x_prompt.shape
    DB, DT, _ = x_sample.shape

    zero_h = jnp.zeros((N_A, B, D_RNN), F32)
    zero_c = jnp.zeros((N_A, B, CONV_W - 1, D_RNN), F32)
    zero_f = jnp.zeros((DEPTH, B, FFN_CONV_W - 1, D_FF), F32)
    out_p = _trunk(x_prompt, zero_h, zero_c, zero_f, None, None, P,
                   _rel_bias_table(rel_bias, WINDOW, keys_on_rows=True), nb=B, tt=WINDOW,
                   prompt=True)

    win_k = cache_k_win.reshape(DB, WINDOW, HKV)
    win_v = cache_v_win.reshape(DB, WINDOW, HKV)
    out_s = _trunk(x_sample, state_lru_h, state_lru_conv, state_ffn_conv, win_k, win_v, P,
                   _rel_bias_table(rel_bias, DT, keys_on_rows=False), nb=DB // 2, tt=DT,
                   prompt=False)
    return (out_p[0], out_s[0]) + out_p[1:] + out_s[1:]
```

```python
import functools
import math

import numpy as np
import jax
import jax.numpy as jnp
from jax import lax
from jax.experimental import pallas as pl
from jax.experimental.pallas import tpu as pltpu

D_MODEL = 1024
DEPTH = 4
N_A = DEPTH // 2
D_RNN = D_MODEL
N_LRU_BLOCKS = 4
LRU_BW = D_RNN // N_LRU_BLOCKS
CONV_W = 4
C_GATE = 8.0
N_HEADS = 16
N_KV_HEADS = 4
HEAD_DIM = 64
GROUP = N_HEADS // N_KV_HEADS
HQ = N_HEADS * HEAD_DIM
HKV = N_KV_HEADS * HEAD_DIM
WINDOW = 128
N_BUCKETS = 32
MAX_DISTANCE = 128
D_FF = 3 * D_MODEL
FFN_CONV_W = 3
EPS = 1e-6
NEG_INF = -1e30
LOG2_E = math.log2(math.e)

F32 = jnp.float32
BF16 = jnp.bfloat16

SUBLANES = 8
LANES = 128
HEADS_PER_TILE = LANES // HEAD_DIM
FF_CHUNK = 256
FFN_ROWS = 512
SEQS_PER_STEP = 4
VMEM_LIMIT_BYTES = 56 * 1024 * 1024


def _rmsnorm(x, g):
    return x * lax.rsqrt(jnp.mean(x * x, axis=-1, keepdims=True) + EPS) * g


def _softplus(x):
    return jnp.maximum(x, 0.0) + jnp.log(1.0 + jnp.exp(-jnp.abs(x)))


def _mm(a, b):
    return jnp.dot(a.astype(BF16), b, preferred_element_type=F32)


def _causal_dwconv(x, last_group, w, b):
    nb, tt, c = x.shape
    taps = w.shape[0]
    groups = tt // SUBLANES
    x4 = x.reshape(nb, groups, SUBLANES, c)
    t = lax.broadcasted_iota(jnp.int32, (1, 1, SUBLANES, c), 2)
    tap = lambda k: w[k:k + 1, :].reshape(1, 1, 1, c)
    y = b.reshape(1, 1, 1, c) + tap(taps - 1) * x4
    for s in range(1, taps):
        cur = pltpu.roll(x4.reshape(nb * groups, SUBLANES, c), s, axis=1)
        cur = cur.reshape(nb, groups, SUBLANES, c)
        before = pltpu.roll(last_group, s, axis=1)[:, None]
        if groups > 1:
            before = jnp.concatenate([before, cur[:, :groups - 1]], axis=1)
        y = y + tap(taps - 1 - s) * jnp.where(t >= s, cur, before)
    return y.reshape(nb, tt, c), x4[:, groups - 1]


def _const_spec(shape, layer=None):
    nd = len(shape)
    if layer is None:
        return pl.BlockSpec(shape, lambda b, j: (0,) * nd, pipeline_mode=pl.Buffered(1))
    return pl.BlockSpec((None,) + shape, lambda b, j: (layer,) + (0,) * nd,
                        pipeline_mode=pl.Buffered(1))


def _params(**kwargs):
    return pltpu.CompilerParams(dimension_semantics=("arbitrary", "arbitrary"),
                                vmem_limit_bytes=VMEM_LIMIT_BYTES, **kwargs)


def _lru_tile(load_x, store_y, norm_ref, win_ref, cw_ref, cb_ref, grw_ref, grb_ref, giw_ref, gib_ref,
              lam_ref, wout_ref, hlast_ref, cnew_ref, last_ref, h_ref, mid_ref, *, nb, tt):
    m = nb * tt
    tail = SUBLANES - (CONV_W - 1)
    x = load_x()
    xn = _rmsnorm(x, norm_ref[...]).astype(BF16)
    decay = (-C_GATE * LOG2_E) * _softplus(-lam_ref[...])
    for n in range(N_LRU_BLOCKS):
        cs = slice(n * LRU_BW, (n + 1) * LRU_BW)
        xb = jnp.dot(xn, win_ref[:, D_RNN + n * LRU_BW:D_RNN + (n + 1) * LRU_BW],
                     preferred_element_type=F32)
        xc, last = _causal_dwconv(xb.reshape(nb, tt, LRU_BW), last_ref[:, :, cs],
                                  cw_ref[:, cs], cb_ref[:, cs])
        last_ref[:, :, cs] = last
        xc = xc.reshape(m, LRU_BW)
        xcb = xc.astype(BF16)
        r = jax.nn.sigmoid(jnp.dot(xcb, grw_ref[n], preferred_element_type=F32) + grb_ref[:, cs])
        i = jax.nn.sigmoid(jnp.dot(xcb, giw_ref[n], preferred_element_type=F32) + gib_ref[:, cs])
        a = jnp.exp2(r * decay[:, cs])
        b = jnp.exp2(0.5 * jnp.log2(1.0 - a * a)) * (i * xc)
        hs, h_last = _linear_scan(a.reshape(nb, tt, LRU_BW), b.reshape(nb, tt, LRU_BW),
                                  h_ref[:, :, cs])
        h_ref[:, :, cs] = h_last
        gate = jnp.dot(xn, win_ref[:, cs], preferred_element_type=F32)
        mid_ref[:, cs] = (jax.nn.gelu(gate) * hs.reshape(m, LRU_BW)).astype(BF16)
    cnew_ref[...] = last_ref[:, tail:, :]
    hlast_ref[...] = h_ref[...]
    store_y(x + jnp.dot(mid_ref[...], wout_ref[...], preferred_element_type=F32))


def _linear_scan(a, b, h0):
    nb, tt, c = a.shape
    groups = tt // SUBLANES
    a = a.reshape(nb * groups, SUBLANES, c)
    b = b.reshape(nb * groups, SUBLANES, c)
    t = lax.broadcasted_iota(jnp.int32, a.shape, 1)
    d = 1
    while d < SUBLANES:
        keep = t >= d
        b = jnp.where(keep, a * pltpu.roll(b, d, axis=1) + b, b)
        a = jnp.where(keep, a * pltpu.roll(a, d, axis=1), a)
        d *= 2
    a = a.reshape(nb, groups, SUBLANES, c)
    b = b.reshape(nb, groups, SUBLANES, c)
    h = h0
    out = []
    for g in range(groups):
        hg = a[:, g] * h + b[:, g]
        out.append(hg)
        h = hg[:, SUBLANES - 1:SUBLANES, :]
    return jnp.concatenate(out, axis=1), h


_LRU_WEIGHTS = 10
_FFN_WEIGHTS = 5


def _init_conv_rows(last_ref, buf_ref, taps):
    nb, _, c = last_ref.shape
    tail = SUBLANES - (taps - 1)
    last_ref[:, :tail, :] = jnp.zeros((nb, tail, c), F32)
    last_ref[:, tail:, :] = buf_ref[...]


def _lru_kernel(x_ref, h0_ref, cbuf_ref, *refs, nb, tt):
    lru_w = refs[:_LRU_WEIGHTS]
    y_ref, hlast_ref, cnew_ref, last_ref, h_ref, mid_ref = refs[_LRU_WEIGHTS:]
    m = nb * tt

    @pl.when(pl.program_id(1) == 0)
    def _():
        _init_conv_rows(last_ref, cbuf_ref, CONV_W)
        h_ref[...] = h0_ref[...]

    def store_out(y):
        y_ref[...] = y.reshape(nb, tt, D_MODEL)

    _lru_tile(lambda: x_ref[...].reshape(m, D_MODEL), store_out, *lru_w,
              hlast_ref, cnew_ref, last_ref, h_ref, mid_ref, nb=nb, tt=tt)


def _lru_block(x, h0, cbuf, norm, w_in, cw, cb, grw, grb, giw, gib, lam, w_out, *, nb, tt, layer):
    B, T, _ = x.shape
    kern = functools.partial(_lru_kernel, nb=nb, tt=tt)
    row = lambda v: v.reshape(1, -1)
    return pl.pallas_call(
        kern,
        grid=(B // nb, T // tt),
        in_specs=[
            pl.BlockSpec((nb, tt, D_MODEL), lambda b, j: (b, j, 0)),
            pl.BlockSpec((nb, 1, D_RNN), lambda b, j: (b, 0, 0)),
            pl.BlockSpec((nb, CONV_W - 1, D_RNN), lambda b, j: (b, 0, 0)),
            _const_spec((1, D_MODEL)),
            _const_spec((D_MODEL, 2 * D_RNN), layer),
            _const_spec((CONV_W, D_RNN)),
            _const_spec((1, D_RNN)),
            _const_spec((N_LRU_BLOCKS, LRU_BW, LRU_BW), layer),
            _const_spec((1, D_RNN)),
            _const_spec((N_LRU_BLOCKS, LRU_BW, LRU_BW), layer),
            _const_spec((1, D_RNN)),
            _const_spec((1, D_RNN)),
            _const_spec((D_RNN, D_MODEL), layer),
        ],
        out_specs=[
            pl.BlockSpec((nb, tt, D_MODEL), lambda b, j: (b, j, 0)),
            pl.BlockSpec((nb, 1, D_RNN), lambda b, j: (b, 0, 0)),
            pl.BlockSpec((nb, CONV_W - 1, D_RNN), lambda b, j: (b, 0, 0)),
        ],
        out_shape=[
            jax.ShapeDtypeStruct((B, T, D_MODEL), F32),
            jax.ShapeDtypeStruct((B, 1, D_RNN), F32),
            jax.ShapeDtypeStruct((B, CONV_W - 1, D_RNN), F32),
        ],
        scratch_shapes=[
            pltpu.VMEM((nb, SUBLANES, D_RNN), F32),
            pltpu.VMEM((nb, 1, D_RNN), F32),
            pltpu.VMEM((nb * tt, D_RNN), BF16),
        ],
        compiler_params=_params(),
        name="rglru_block",
    )(x, h0.reshape(B, 1, D_RNN), cbuf, row(norm), w_in, cw, row(cb), grw, row(grb), giw, row(gib), row(lam), w_out)


def _ffn_kernel(x_ref, buf_ref, *refs, nb, tt):
    ffn_w = refs[:_FFN_WEIGHTS]
    y_ref, bnew_ref, last_ref, ext_ref, mid_ref = refs[_FFN_WEIGHTS:]
    m = nb * tt

    @pl.when(pl.program_id(1) == 0)
    def _():
        _init_conv_rows(last_ref, buf_ref, FFN_CONV_W)

    def store_out(y):
        y_ref[...] = y.reshape(nb, tt, D_MODEL)

    _ffn_tile(lambda: x_ref[...].reshape(m, D_MODEL), store_out, *ffn_w, bnew_ref,
              last_ref, ext_ref, mid_ref, nb=nb, tt=tt)


def _ffn_tile(load_x, store_y, norm_ref, wup_ref, cw_ref, cb_ref, wdown_ref, bnew_ref,
              last_ref, ext_ref, mid_ref, *, nb, tt):
    m = nb * tt
    tail = SUBLANES - (FFN_CONV_W - 1)
    x = load_x()
    xn = _rmsnorm(x, norm_ref[...]).astype(BF16)
    for c in range(D_FF // FF_CHUNK):
        cs = slice(c * FF_CHUNK, (c + 1) * FF_CHUNK)
        g = jnp.dot(xn, wup_ref[:, cs], preferred_element_type=F32).reshape(nb, tt, FF_CHUNK)
        v = jnp.dot(xn, wup_ref[:, D_FF + c * FF_CHUNK:D_FF + (c + 1) * FF_CHUNK],
                    preferred_element_type=F32)
        if tt == SUBLANES:
            gc, last = _causal_dwconv(g, last_ref[:, :, cs], cw_ref[:, cs], cb_ref[:, cs])
            last_ref[:, :, cs] = last
        else:
            ext_ref[:, tail:SUBLANES, :] = last_ref[:, tail:, cs]
            ext_ref[:, SUBLANES:, :] = g
            gc = cb_ref[:, cs].reshape(1, 1, FF_CHUNK)
            for k in range(FFN_CONV_W):
                gc = gc + (cw_ref[k:k + 1, cs].reshape(1, 1, FF_CHUNK)
                           * ext_ref[:, tail + k:tail + k + tt, :])
            last_ref[:, tail:, cs] = ext_ref[:, tail + tt:SUBLANES + tt, :]
        mid_ref[:, cs] = (jax.nn.gelu(gc).reshape(m, FF_CHUNK) * v).astype(BF16)
    bnew_ref[...] = last_ref[:, tail:, :]
    store_y(x + jnp.dot(mid_ref[...], wdown_ref[...], preferred_element_type=F32))


def _conv_ffn(x, buf, norm, w_up, cw, cb, w_down, *, nb, tt, layer):
    B, T, _ = x.shape
    kern = functools.partial(_ffn_kernel, nb=nb, tt=tt)
    return pl.pallas_call(
        kern,
        grid=(B // nb, T // tt),
        in_specs=[
            pl.BlockSpec((nb, tt, D_MODEL), lambda b, j: (b, j, 0)),
            pl.BlockSpec((None, nb, FFN_CONV_W - 1, D_FF), lambda b, j: (layer, b, 0, 0)),
            _const_spec((1, D_MODEL)),
            _const_spec((D_MODEL, 2 * D_FF), layer),
            _const_spec((FFN_CONV_W, D_FF)),
            _const_spec((1, D_FF)),
            _const_spec((D_FF, D_MODEL), layer),
        ],
        out_specs=[
            pl.BlockSpec((nb, tt, D_MODEL), lambda b, j: (b, j, 0)),
            pl.BlockSpec((nb, FFN_CONV_W - 1, D_FF), lambda b, j: (b, 0, 0)),
        ],
        out_shape=[
            jax.ShapeDtypeStruct((B, T, D_MODEL), F32),
            jax.ShapeDtypeStruct((B, FFN_CONV_W - 1, D_FF), F32),
        ],
        scratch_shapes=[
            pltpu.VMEM((nb, SUBLANES, D_FF), F32),
            pltpu.VMEM((nb, SUBLANES + tt, FF_CHUNK), F32),
            pltpu.VMEM((nb * tt, D_FF), BF16),
        ],
        compiler_params=_params(),
        name="conv_ffn",
    )(x, buf, norm.reshape(1, -1), w_up, cw, cb.reshape(1, -1), w_down)


def _kv_kernel(x_ref, norm_ref, w_ref, b_ref, kn_ref, *rest, nb, tt, key_major_copies):
    m = nb * tt
    x = x_ref[...].reshape(m, D_MODEL)
    xn = _rmsnorm(x, norm_ref[...]).astype(BF16)
    kv = jnp.dot(xn, w_ref[...], preferred_element_type=F32) + b_ref[...]
    k = kv[:, :HKV]
    if key_major_copies:
        wvt_ref, bvt_ref, k_ref, v_ref, kb_ref, vt_ref = rest
        vt = lax.dot_general(wvt_ref[...], xn, (((1,), (1,)), ((), ())),
                             preferred_element_type=F32) + bvt_ref[...]
        for n in range(nb):
            vt_ref[n] = vt[:, n * tt:(n + 1) * tt].astype(BF16)
    else:
        k_ref, v_ref = rest
    rows = lax.broadcasted_iota(jnp.int32, (HKV, HKV), 0) // HEAD_DIM
    cols = lax.broadcasted_iota(jnp.int32, (HKV, HKV), 1) // HEAD_DIM
    avg = jnp.where(rows == cols, 1.0 / HEAD_DIM, 0.0).astype(BF16)
    sq = k * k
    hi = sq.astype(BF16)
    lo = (sq - hi.astype(F32)).astype(BF16)
    ms = (jnp.dot(hi, avg, preferred_element_type=F32)
          + jnp.dot(lo, avg, preferred_element_type=F32))
    kn = (k * lax.rsqrt(ms + EPS) * kn_ref[...]).reshape(nb, tt, HKV)
    k_ref[...] = kn
    v_ref[...] = kv[:, HKV:].reshape(nb, tt, HKV)
    if key_major_copies:
        kb_ref[...] = kn.astype(BF16)


def _shared_kv(x, norm, w_kv, b_kv, k_norm, *, nb, tt, key_major_copies):
    B, T, _ = x.shape
    kern = functools.partial(_kv_kernel, nb=nb, tt=tt, key_major_copies=key_major_copies)
    tok_spec = pl.BlockSpec((nb, tt, HKV), lambda b, j: (b, j, 0))
    in_specs = [
        pl.BlockSpec((nb, tt, D_MODEL), lambda b, j: (b, j, 0)),
        _const_spec((1, D_MODEL)),
        _const_spec((D_MODEL, 2 * HKV)),
        _const_spec((1, 2 * HKV)),
        _const_spec((1, HKV)),
    ]
    args = [x, norm.reshape(1, -1), w_kv, b_kv.reshape(1, -1),
            jnp.tile(k_norm, N_KV_HEADS).reshape(1, -1)]
    out_specs = [tok_spec, tok_spec]
    out_shape = [jax.ShapeDtypeStruct((B, T, HKV), F32), jax.ShapeDtypeStruct((B, T, HKV), F32)]
    if key_major_copies:
        assert tt == WINDOW
        last_spec = pl.BlockSpec((nb, tt, HKV), lambda b, j: (b, 0, 0))
        in_specs += [_const_spec((HKV, D_MODEL)), _const_spec((HKV, 1))]
        args += [w_kv[:, HKV:].T, b_kv[HKV:].reshape(-1, 1)]
        out_specs = [last_spec, last_spec, tok_spec,
                     pl.BlockSpec((nb, HKV, tt), lambda b, j: (b, 0, j))]
        out_shape = [jax.ShapeDtypeStruct((B, WINDOW, HKV), F32),
                     jax.ShapeDtypeStruct((B, WINDOW, HKV), F32),
                     jax.ShapeDtypeStruct((B, T, HKV), BF16),
                     jax.ShapeDtypeStruct((B, HKV, T), BF16)]
    return pl.pallas_call(
        kern,
        grid=(B // nb, T // tt),
        in_specs=in_specs,
        out_specs=out_specs,
        out_shape=out_shape,
        compiler_params=_params(),
        name="shared_kv",
    )(*args)


def _rel_buckets(tt):
    qi = np.arange(tt)[:, None]
    sj = np.arange(WINDOW + tt)[None, :]
    dist = qi + WINDOW - sj
    max_exact = N_BUCKETS // 2
    d = np.maximum(dist, 0)
    df = np.maximum(d, 1).astype(np.float32)
    large = max_exact + (np.log(df / max_exact) / math.log(MAX_DISTANCE / max_exact)
                         * (N_BUCKETS - max_exact)).astype(np.int32)
    large = np.minimum(large, N_BUCKETS - 1)
    bucket = np.where(d < max_exact, d, large)
    valid = (dist >= 0) & (dist < WINDOW)
    return np.where(valid, bucket, -1).astype(np.int32)


def _bias_of_buckets(rb_ref, bkt, h):
    acc = jnp.full(bkt.shape, NEG_INF, F32)
    for c in range(N_BUCKETS):
        acc = jnp.where(bkt == c, rb_ref[c, h], acc)
    return acc


def _bias_kernel(rb_ref, bkt_ref, out_ref):
    bkt = bkt_ref[...]

    def head(h, carry):
        out_ref[h] = _bias_of_buckets(rb_ref, bkt, h)
        return carry

    lax.fori_loop(0, N_HEADS, head, 0)


def _bias_t_kernel(rb_ref, bkt_ref, out_ref, *, tt):
    bkt = bkt_ref[...]
    before_tile = lax.broadcasted_iota(jnp.int32, bkt.shape, 0) < WINDOW

    def group(g, carry):
        for hh in range(GROUP):
            acc = _bias_of_buckets(rb_ref, bkt, g * GROUP + hh) * LOG2_E
            out_ref[0, g, :, hh * tt:(hh + 1) * tt] = acc
            out_ref[1, g, :, hh * tt:(hh + 1) * tt] = jnp.where(before_tile, NEG_INF, acc)
        return carry

    lax.fori_loop(0, N_KV_HEADS, group, 0)


def _rel_bias_table(rel_bias, tt, *, keys_on_rows):
    L = WINDOW + tt
    bkt = _rel_buckets(tt)
    if keys_on_rows:
        kern = functools.partial(_bias_t_kernel, tt=tt)
        bkt = bkt.T
        out_shape = jax.ShapeDtypeStruct((2, N_KV_HEADS, L, GROUP * tt), F32)
    else:
        kern = _bias_kernel
        out_shape = jax.ShapeDtypeStruct((N_HEADS, tt, L), F32)
    return pl.pallas_call(
        kern,
        in_specs=[
            pl.BlockSpec(memory_space=pltpu.SMEM),
            pl.BlockSpec(memory_space=pltpu.VMEM),
        ],
        out_specs=pl.BlockSpec(memory_space=pltpu.VMEM),
        out_shape=out_shape,
        name="rel_bias_table",
    )(rel_bias, jnp.asarray(bkt))


def _attn_sample_kernel(sinks_ref, x_ref, kp_ref, kc_ref, vp_ref, vc_ref, bias_ref, norm_ref, wq_ref,
                        bq_ref, qn_ref, wo_ref, bo_ref, y_ref, q_ref, o_ref, *, nb, tt):
    m = nb * tt
    L = WINDOW + tt
    sb = SEQS_PER_STEP
    x = x_ref[...].reshape(m, D_MODEL)
    q = _mm(_rmsnorm(x, norm_ref[...]), wq_ref[...]) + bq_ref[...]
    qgain = qn_ref[...] * (HEAD_DIM ** -0.5)
    for h in range(N_HEADS):
        hs = slice(h * HEAD_DIM, (h + 1) * HEAD_DIM)
        qh = q[:, hs]
        q_ref[:, hs] = qh * lax.rsqrt(jnp.mean(qh * qh, axis=-1, keepdims=True) + EPS) * qgain
    sink = jnp.concatenate([jnp.full((tt, 1), sinks_ref[h], F32) for h in range(N_HEADS)],
                           axis=0)[None]
    bias = bias_ref[...].reshape(1, N_HEADS * tt, L)

    half_of_lane = lax.broadcasted_iota(jnp.int32, (1, LANES), 1) // HEAD_DIM
    zeros_tile = jnp.zeros((sb * tt, LANES), F32)

    def in_kv_lanes(qblk, h):
        g = h // GROUP
        qt = qblk[:, (h // HEADS_PER_TILE) * LANES:(h // HEADS_PER_TILE + 1) * LANES]
        qt = jnp.where(half_of_lane == h % HEADS_PER_TILE, qt, 0.0)
        if h % HEADS_PER_TILE != g % HEADS_PER_TILE:
            qt = pltpu.roll(qt, HEAD_DIM, axis=1)
        tiles = [qt if t == g // HEADS_PER_TILE else zeros_tile for t in range(HKV // LANES)]
        return jnp.concatenate(tiles, axis=1).reshape(sb, tt, HKV)

    def seqs(i, carry):
        n0 = pl.multiple_of(i * sb, sb)
        rows = pl.ds(pl.multiple_of(i * (sb * tt), sb * tt), sb * tt)
        qblk = q_ref[rows, :]
        kall = jnp.concatenate([kp_ref[pl.ds(n0, sb)], kc_ref[pl.ds(n0, sb)]], axis=1).astype(BF16)
        vall = jnp.concatenate([vp_ref[pl.ds(n0, sb)], vc_ref[pl.ds(n0, sb)]], axis=1).astype(BF16)
        qp = jnp.concatenate([in_kv_lanes(qblk, h) for h in range(N_HEADS)], axis=1).astype(BF16)
        s = jnp.einsum('bqd,bkd->bqk', qp, kall, preferred_element_type=F32) + bias
        mx = jnp.maximum(jnp.max(s, axis=-1, keepdims=True), sink)
        p = jnp.exp(s - mx)
        den = jnp.sum(p, axis=-1, keepdims=True) + jnp.exp(sink - mx)
        o = jnp.einsum('bqk,bkd->bqd', p.astype(BF16), vall, preferred_element_type=F32) / den
        for h in range(N_HEADS):
            g = h // GROUP
            o_ref[rows, h * HEAD_DIM:(h + 1) * HEAD_DIM] = (
                o[:, h * tt:(h + 1) * tt, g * HEAD_DIM:(g + 1) * HEAD_DIM].reshape(sb * tt, HEAD_DIM))
        return carry

    lax.fori_loop(0, nb // sb, seqs, 0)
    y = _mm(o_ref[...], wo_ref[...]) + bo_ref[...]
    y_ref[...] = (x + y).reshape(nb, tt, D_MODEL)


def _attn_prompt_kernel(sinks_ref, x_ref, kp_ref, kc_ref, vtp_ref, vtc_ref, bias_ref, norm_ref,
                        wqt_ref, bq_ref, qn_ref, wo_ref, bo_ref, y_ref, qt_ref, ot_ref, s_ref,
                        *, nb, tt):
    m = nb * tt
    x = x_ref[...].reshape(m, D_MODEL)
    xn = _rmsnorm(x, norm_ref[...]).astype(BF16)
    qt = lax.dot_general(wqt_ref[...], xn, (((1,), (1,)), ((), ())),
                         preferred_element_type=F32) + bq_ref[...]
    qgain = qn_ref[...] * (HEAD_DIM ** -0.5 * LOG2_E)
    for h in range(N_HEADS):
        hs = slice(h * HEAD_DIM, (h + 1) * HEAD_DIM)
        qh = qt[hs, :]
        ms = jnp.mean(qh * qh, axis=0, keepdims=True)
        qt_ref[hs, :] = (qh * lax.rsqrt(ms + EPS) * qgain).astype(BF16)

    head_of_lane = lax.broadcasted_iota(jnp.int32, (1, GROUP * tt), 1) // tt
    sinks = []
    for g in range(N_KV_HEADS):
        sink = jnp.full((1, GROUP * tt), sinks_ref[g * GROUP] * LOG2_E, F32)
        for hh in range(1, GROUP):
            sink = jnp.where(head_of_lane == hh, sinks_ref[g * GROUP + hh] * LOG2_E, sink)
        sinks.append(sink)
    ones_rows = jnp.ones((2 * SUBLANES, WINDOW + tt), BF16)

    def scores(g, n):
        gs = slice(g * HEAD_DIM, (g + 1) * HEAD_DIM)
        ts = slice(n * tt, (n + 1) * tt)
        kk = jnp.concatenate([kp_ref[n, :, gs], kc_ref[n, :, gs]], axis=0)
        q4 = jnp.concatenate([qt_ref[h * HEAD_DIM:(h + 1) * HEAD_DIM, ts]
                              for h in range(g * GROUP, (g + 1) * GROUP)], axis=1)
        return jnp.dot(kk, q4, preferred_element_type=F32)

    pairs = [(g, n) for g in range(N_KV_HEADS) for n in range(nb)]
    for idx, (g, n) in enumerate(pairs):
        s_ref[idx] = scores(g, n) + bias_ref[0, g]
    for idx, (g, n) in enumerate(pairs):
        gs = slice(g * HEAD_DIM, (g + 1) * HEAD_DIM)
        ts = slice(n * tt, (n + 1) * tt)
        s = s_ref[idx]
        mx = jnp.maximum(jnp.max(s, axis=0, keepdims=True), sinks[g])
        p = jnp.exp2(s - mx).astype(BF16)
        vt = jnp.concatenate([vtp_ref[n, gs, :], vtc_ref[n, gs, :]], axis=1)
        pv = jnp.dot(jnp.concatenate([vt, ones_rows], axis=0), p,
                     preferred_element_type=F32)
        den = pv[HEAD_DIM:HEAD_DIM + 1, :] + jnp.exp2(sinks[g] - mx)
        o = pv[:HEAD_DIM, :] / den
        for hh in range(GROUP):
            h = g * GROUP + hh
            ot_ref[h * HEAD_DIM:(h + 1) * HEAD_DIM, ts] = o[:, hh * tt:(hh + 1) * tt]
    y = _mm(ot_ref[...].T, wo_ref[...]) + bo_ref[...]
    y_ref[...] = (x + y).reshape(nb, tt, D_MODEL)


def _const_spec3(shape, layer=None):
    nd = len(shape)
    if layer is None:
        return pl.BlockSpec(shape, lambda b, j, s: (0,) * nd, pipeline_mode=pl.Buffered(1))
    return pl.BlockSpec((None,) + shape, lambda b, j, s: (layer,) + (0,) * nd,
                        pipeline_mode=pl.Buffered(1))


def _attn_sample_block(x, k_win, k_new, v_win, v_new, bias, sinks, norm, w_q, b_q, q_norm, w_o, b_o,
                       *, nb, layer):
    B, tt, _ = x.shape
    L = WINDOW + tt
    assert nb % SEQS_PER_STEP == 0 and tt % SUBLANES == 0
    kern = functools.partial(_attn_sample_kernel, nb=nb, tt=tt)
    seq_map = lambda b, j, s: (b, 0, 0)
    grid_spec = pltpu.PrefetchScalarGridSpec(
        num_scalar_prefetch=1,
        grid=(B // nb, 1),
        in_specs=[
            pl.BlockSpec((nb, tt, D_MODEL), seq_map),
            pl.BlockSpec((nb, WINDOW, HKV), seq_map),
            pl.BlockSpec((nb, tt, HKV), seq_map),
            pl.BlockSpec((nb, WINDOW, HKV), seq_map),
            pl.BlockSpec((nb, tt, HKV), seq_map),
            _const_spec3((N_HEADS, tt, L)),
            _const_spec3((1, D_MODEL)),
            _const_spec3((D_MODEL, HQ), layer),
            _const_spec3((1, HQ)),
            _const_spec3((1, HEAD_DIM)),
            _const_spec3((HQ, D_MODEL), layer),
            _const_spec3((1, D_MODEL)),
        ],
        out_specs=pl.BlockSpec((nb, tt, D_MODEL), seq_map),
        scratch_shapes=[
            pltpu.VMEM((nb * tt, HQ), F32),
            pltpu.VMEM((nb * tt, HQ), F32),
        ],
    )
    return pl.pallas_call(
        kern,
        grid_spec=grid_spec,
        out_shape=jax.ShapeDtypeStruct((B, tt, D_MODEL), F32),
        compiler_params=_params(),
        name="swa_sample",
    )(sinks, x, k_win, k_new, v_win, v_new, bias, norm.reshape(1, -1), w_q, b_q.reshape(1, -1),
      q_norm.reshape(1, -1), w_o, b_o.reshape(1, -1))


def _attn_prompt_block(x, kb, vt, bias_t, sinks, norm, w_qt, b_q, q_norm, w_o, b_o, *, nb, layer):
    B, T, _ = x.shape
    tt = WINDOW
    L = WINDOW + tt
    kern = functools.partial(_attn_prompt_kernel, nb=nb, tt=tt)
    cur_map = lambda b, j, s: (b, j, 0)
    prev_map = lambda b, j, s: (b, jnp.maximum(j - 1, 0), 0)
    grid_spec = pltpu.PrefetchScalarGridSpec(
        num_scalar_prefetch=1,
        grid=(B // nb, T // tt),
        in_specs=[
            pl.BlockSpec((nb, tt, D_MODEL), cur_map),
            pl.BlockSpec((nb, tt, HKV), prev_map),
            pl.BlockSpec((nb, tt, HKV), cur_map),
            pl.BlockSpec((nb, HKV, tt), lambda b, j, s: (b, 0, jnp.maximum(j - 1, 0))),
            pl.BlockSpec((nb, HKV, tt), lambda b, j, s: (b, 0, j)),
            pl.BlockSpec((1, N_KV_HEADS, L, GROUP * tt),
                         lambda b, j, s: (jnp.where(j == 0, 1, 0), 0, 0, 0)),
            _const_spec3((1, D_MODEL)),
            _const_spec3((HQ, D_MODEL), layer),
            _const_spec3((HQ, 1)),
            _const_spec3((HEAD_DIM, 1)),
            _const_spec3((HQ, D_MODEL), layer),
            _const_spec3((1, D_MODEL)),
        ],
        out_specs=pl.BlockSpec((nb, tt, D_MODEL), cur_map),
        scratch_shapes=[
            pltpu.VMEM((HQ, nb * tt), BF16),
            pltpu.VMEM((HQ, nb * tt), F32),
            pltpu.VMEM((N_KV_HEADS * nb, L, GROUP * tt), F32),
        ],
    )
    return pl.pallas_call(
        kern,
        grid_spec=grid_spec,
        out_shape=jax.ShapeDtypeStruct((B, T, D_MODEL), F32),
        compiler_params=_params(),
        name="swa_prompt",
    )(sinks, x, kb, kb, vt, vt, bias_t, norm.reshape(1, -1), w_qt, b_q.reshape(-1, 1),
      q_norm.reshape(-1, 1), w_o, b_o.reshape(1, -1))


def _trunk(x, lru_h, lru_conv, ffn_conv, win_k, win_v, P, bias, *, nb, tt, prompt):
    B, T, _ = x.shape
    new_h, new_c, new_f = [], [], []
    k = v = kb = vt = None
    for layer in range(DEPTH):
        if layer < N_A:
            i = layer
            lru_w = (P['a_norm'][i], P['a_w_in'], P['a_conv_w'][i], P['a_conv_b'][i],
                     P['a_gate_r_w'], P['a_gate_r_b'][i], P['a_gate_i_w'], P['a_gate_i_b'][i],
                     P['a_lambda'][i], P['a_w_out'])
            ffn_w = (P['f_norm'][layer], P['f_w_up'], P['f_conv_w'][layer], P['f_conv_b'][layer],
                     P['f_w_down'])
            x, h_i, c_i = _lru_block(x, lru_h[i], lru_conv[i], *lru_w, nb=nb, tt=tt, layer=i)
            x, f_i = _conv_ffn(x, ffn_conv, *ffn_w, nb=nb, tt=tt, layer=layer)
            new_h.append(h_i.reshape(B, D_RNN))
            new_c.append(c_i)
        else:
            j = layer - N_A
            if prompt:
                x = _attn_prompt_block(x, kb, vt, bias, P['sinks'][j], P['b_norm'][j], P['w_qt'],
                                       P['b_q'][j], P['q_norm'][j], P['w_o'], P['b_o'][j],
                                       nb=nb, layer=j)
            else:
                x = _attn_sample_block(x, win_k, k, win_v, v, bias, P['sinks'][j], P['b_norm'][j],
                                       P['w_q'], P['b_q'][j], P['q_norm'][j], P['w_o'], P['b_o'][j],
                                       nb=nb, layer=j)
            x, f_i = _conv_ffn(x, ffn_conv, P['f_norm'][layer], P['f_w_up'],
                               P['f_conv_w'][layer], P['f_conv_b'][layer], P['f_w_down'],
                               nb=nb, tt=min(T, FFN_ROWS // nb), layer=layer)
        new_f.append(f_i)
        if layer == N_A - 1:
            kv = _shared_kv(x, P['kv_norm'], P['w_kv'], P['b_kv'], P['k_norm'], nb=nb, tt=tt,
                            key_major_copies=prompt)
            if prompt:
                k, v, kb, vt = kv
            else:
                k, v = kv
    if prompt:
        new_k, new_v = k, v
    else:
        new_k = jnp.concatenate([win_k, k], axis=1)[:, -WINDOW:]
        new_v = jnp.concatenate([win_v, v], axis=1)[:, -WINDOW:]
    shape4 = (B, WINDOW, N_KV_HEADS, HEAD_DIM)
    return (x, jnp.stack(new_h), jnp.stack(new_c), jnp.stack(new_f),
            new_k.reshape(shape4), new_v.reshape(shape4))


def kernel(x_prompt, x_sample, state_lru_h, state_lru_conv, state_ffn_conv, cache_k_win, cache_v_win,
           a_norm, a_w_in, a_conv_w, a_conv_b, a_gate_r_w, a_gate_r_b, a_gate_i_w, a_gate_i_b,
           a_lambda, a_w_out, kv_norm, w_kv, b_kv, k_norm, b_norm, w_q, b_q, q_norm, sinks,
           w_o, b_o, rel_bias, f_norm, f_w_up, f_conv_w, f_conv_b, f_w_down):
    bf = lambda w: w.astype(BF16)
    P = dict(a_norm=a_norm, a_w_in=bf(a_w_in), a_conv_w=a_conv_w, a_conv_b=a_conv_b,
             a_gate_r_w=bf(a_gate_r_w), a_gate_r_b=a_gate_r_b, a_gate_i_w=bf(a_gate_i_w),
             a_gate_i_b=a_gate_i_b, a_lambda=a_lambda, a_w_out=bf(a_w_out), kv_norm=kv_norm,
             w_kv=bf(w_kv), b_kv=b_kv, k_norm=k_norm, b_norm=b_norm, w_q=bf(w_q), b_q=b_q,
             q_norm=q_norm, sinks=sinks, w_o=bf(w_o), b_o=b_o, f_norm=f_norm, f_w_up=bf(f_w_up),
             f_conv_w=f_conv_w, f_conv_b=f_conv_b, f_w_down=bf(f_w_down))
    P['w_qt'] = jnp.swapaxes(P['w_q'], 1, 2)
    B, S, _ = x_prompt.shape
    DB, DT, _ = x_sample.shape

    zero_h = jnp.zeros((N_A, B, D_RNN), F32)
    zero_c = jnp.zeros((N_A, B, CONV_W - 1, D_RNN), F32)
    zero_f = jnp.zeros((DEPTH, B, FFN_CONV_W - 1, D_FF), F32)
    out_p = _trunk(x_prompt, zero_h, zero_c, zero_f, None, None, P,
                   _rel_bias_table(rel_bias, WINDOW, keys_on_rows=True), nb=B, tt=WINDOW,
                   prompt=True)

    win_k = cache_k_win.reshape(DB, WINDOW, HKV)
    win_v = cache_v_win.reshape(DB, WINDOW, HKV)
    out_s = _trunk(x_sample, state_lru_h, state_lru_conv, state_ffn_conv, win_k, win_v, P,
                   _rel_bias_table(rel_bias, DT, keys_on_rows=False), nb=DB // 2, tt=DT,
                   prompt=False)
    return (out_p[0], out_s[0]) + out_p[1:] + out_s[1:]
```

```python
import functools
import math

import numpy as np
import jax
import jax.numpy as jnp
from jax import lax
from jax.experimental import pallas as pl
from jax.experimental.pallas import tpu as pltpu

D_MODEL = 1024
DEPTH = 4
N_A = DEPTH // 2
D_RNN = D_MODEL
N_LRU_BLOCKS = 4
LRU_BW = D_RNN // N_LRU_BLOCKS
CONV_W = 4
C_GATE = 8.0
N_HEADS = 16
N_KV_HEADS = 4
HEAD_DIM = 64
GROUP = N_HEADS // N_KV_HEADS
HQ = N_HEADS * HEAD_DIM
HKV = N_KV_HEADS * HEAD_DIM
WINDOW = 128
N_BUCKETS = 32
MAX_DISTANCE = 128
D_FF = 3 * D_MODEL
FFN_CONV_W = 3
EPS = 1e-6
NEG_INF = -1e30
LOG2_E = math.log2(math.e)

F32 = jnp.float32
BF16 = jnp.bfloat16

SUBLANES = 8
LANES = 128
HEADS_PER_TILE = LANES // HEAD_DIM
FF_CHUNK = 256
FFN_ROWS = 512
LRU_ROWS = 1024
SEQS_PER_STEP = 4
VMEM_LIMIT_BYTES = 56 * 1024 * 1024


def _rmsnorm(x, g):
    return x * lax.rsqrt(jnp.mean(x * x, axis=-1, keepdims=True) + EPS) * g


def _softplus(x):
    return jnp.maximum(x, 0.0) + jnp.log(1.0 + jnp.exp(-jnp.abs(x)))


def _mm(a, b):
    return jnp.dot(a.astype(BF16), b, preferred_element_type=F32)


def _causal_dwconv(x, last_group, w, b):
    nb, tt, c = x.shape
    taps = w.shape[0]
    groups = tt // SUBLANES
    x4 = x.reshape(nb, groups, SUBLANES, c)
    t = lax.broadcasted_iota(jnp.int32, (1, 1, SUBLANES, c), 2)
    tap = lambda k: w[k:k + 1, :].reshape(1, 1, 1, c)
    y = b.reshape(1, 1, 1, c) + tap(taps - 1) * x4
    for s in range(1, taps):
        cur = pltpu.roll(x4.reshape(nb * groups, SUBLANES, c), s, axis=1)
        cur = cur.reshape(nb, groups, SUBLANES, c)
        before = pltpu.roll(last_group, s, axis=1)[:, None]
        if groups > 1:
            before = jnp.concatenate([before, cur[:, :groups - 1]], axis=1)
        y = y + tap(taps - 1 - s) * jnp.where(t >= s, cur, before)
    return y.reshape(nb, tt, c), x4[:, groups - 1]


def _const_spec(shape, layer=None):
    nd = len(shape)
    if layer is None:
        return pl.BlockSpec(shape, lambda b, j: (0,) * nd, pipeline_mode=pl.Buffered(1))
    return pl.BlockSpec((None,) + shape, lambda b, j: (layer,) + (0,) * nd,
                        pipeline_mode=pl.Buffered(1))


def _params(**kwargs):
    return pltpu.CompilerParams(dimension_semantics=("arbitrary", "arbitrary"),
                                vmem_limit_bytes=VMEM_LIMIT_BYTES, **kwargs)


def _lru_tile(load_x, store_y, norm_ref, win_ref, cw_ref, cb_ref, grw_ref, grb_ref, giw_ref, gib_ref,
              lam_ref, wout_ref, hlast_ref, cnew_ref, last_ref, h_ref, mid_ref, *, nb, tt):
    m = nb * tt
    tail = SUBLANES - (CONV_W - 1)
    x = load_x()
    xn = _rmsnorm(x, norm_ref[...]).astype(BF16)
    decay = (-C_GATE * LOG2_E) * _softplus(-lam_ref[...])
    for n in range(N_LRU_BLOCKS):
        cs = slice(n * LRU_BW, (n + 1) * LRU_BW)
        xb = jnp.dot(xn, win_ref[:, D_RNN + n * LRU_BW:D_RNN + (n + 1) * LRU_BW],
                     preferred_element_type=F32)
        xc, last = _causal_dwconv(xb.reshape(nb, tt, LRU_BW), last_ref[:, :, cs],
                                  cw_ref[:, cs], cb_ref[:, cs])
        last_ref[:, :, cs] = last
        xc = xc.reshape(m, LRU_BW)
        xcb = xc.astype(BF16)
        r = jax.nn.sigmoid(jnp.dot(xcb, grw_ref[n], preferred_element_type=F32) + grb_ref[:, cs])
        i = jax.nn.sigmoid(jnp.dot(xcb, giw_ref[n], preferred_element_type=F32) + gib_ref[:, cs])
        a = jnp.exp2(r * decay[:, cs])
        b = jnp.exp2(0.5 * jnp.log2(1.0 - a * a)) * (i * xc)
        hs, h_last = _linear_scan(a.reshape(nb, tt, LRU_BW), b.reshape(nb, tt, LRU_BW),
                                  h_ref[:, :, cs])
        h_ref[:, :, cs] = h_last
        gate = jnp.dot(xn, win_ref[:, cs], preferred_element_type=F32)
        mid_ref[:, cs] = (jax.nn.gelu(gate) * hs.reshape(m, LRU_BW)).astype(BF16)
    cnew_ref[...] = last_ref[:, tail:, :]
    hlast_ref[...] = h_ref[...]
    store_y(x + jnp.dot(mid_ref[...], wout_ref[...], preferred_element_type=F32))


def _linear_scan(a, b, h0):
    nb, tt, c = a.shape
    groups = tt // SUBLANES
    a = a.reshape(nb * groups, SUBLANES, c)
    b = b.reshape(nb * groups, SUBLANES, c)
    t = lax.broadcasted_iota(jnp.int32, a.shape, 1)
    d = 1
    while d < SUBLANES:
        keep = t >= d
        b = jnp.where(keep, a * pltpu.roll(b, d, axis=1) + b, b)
        a = jnp.where(keep, a * pltpu.roll(a, d, axis=1), a)
        d *= 2
    a = a.reshape(nb, groups, SUBLANES, c)
    b = b.reshape(nb, groups, SUBLANES, c)
    h = h0
    out = []
    for g in range(groups):
        hg = a[:, g] * h + b[:, g]
        out.append(hg)
        h = hg[:, SUBLANES - 1:SUBLANES, :]
    return jnp.concatenate(out, axis=1), h


_LRU_WEIGHTS = 10
_FFN_WEIGHTS = 5


def _init_conv_rows(last_ref, buf_ref, taps):
    nb, _, c = last_ref.shape
    tail = SUBLANES - (taps - 1)
    last_ref[:, :tail, :] = jnp.zeros((nb, tail, c), F32)
    last_ref[:, tail:, :] = buf_ref[...]


def _lru_kernel(x_ref, h0_ref, cbuf_ref, *refs, nb, tt):
    lru_w = refs[:_LRU_WEIGHTS]
    y_ref, hlast_ref, cnew_ref, last_ref, h_ref, mid_ref = refs[_LRU_WEIGHTS:]
    m = nb * tt

    @pl.when(pl.program_id(1) == 0)
    def _():
        _init_conv_rows(last_ref, cbuf_ref, CONV_W)
        h_ref[...] = h0_ref[...]

    def store_out(y):
        y_ref[...] = y.reshape(nb, tt, D_MODEL)

    _lru_tile(lambda: x_ref[...].reshape(m, D_MODEL), store_out, *lru_w,
              hlast_ref, cnew_ref, last_ref, h_ref, mid_ref, nb=nb, tt=tt)


def _lru_block(x, h0, cbuf, norm, w_in, cw, cb, grw, grb, giw, gib, lam, w_out, *, nb, tt, layer):
    B, T, _ = x.shape
    kern = functools.partial(_lru_kernel, nb=nb, tt=tt)
    row = lambda v: v.reshape(1, -1)
    return pl.pallas_call(
        kern,
        grid=(B // nb, T // tt),
        in_specs=[
            pl.BlockSpec((nb, tt, D_MODEL), lambda b, j: (b, j, 0)),
            pl.BlockSpec((nb, 1, D_RNN), lambda b, j: (b, 0, 0)),
            pl.BlockSpec((nb, CONV_W - 1, D_RNN), lambda b, j: (b, 0, 0)),
            _const_spec((1, D_MODEL)),
            _const_spec((D_MODEL, 2 * D_RNN), layer),
            _const_spec((CONV_W, D_RNN)),
            _const_spec((1, D_RNN)),
            _const_spec((N_LRU_BLOCKS, LRU_BW, LRU_BW), layer),
            _const_spec((1, D_RNN)),
            _const_spec((N_LRU_BLOCKS, LRU_BW, LRU_BW), layer),
            _const_spec((1, D_RNN)),
            _const_spec((1, D_RNN)),
            _const_spec((D_RNN, D_MODEL), layer),
        ],
        out_specs=[
            pl.BlockSpec((nb, tt, D_MODEL), lambda b, j: (b, j, 0)),
            pl.BlockSpec((nb, 1, D_RNN), lambda b, j: (b, 0, 0)),
            pl.BlockSpec((nb, CONV_W - 1, D_RNN), lambda b, j: (b, 0, 0)),
        ],
        out_shape=[
            jax.ShapeDtypeStruct((B, T, D_MODEL), F32),
            jax.ShapeDtypeStruct((B, 1, D_RNN), F32),
            jax.ShapeDtypeStruct((B, CONV_W - 1, D_RNN), F32),
        ],
        scratch_shapes=[
            pltpu.VMEM((nb, SUBLANES, D_RNN), F32),
            pltpu.VMEM((nb, 1, D_RNN), F32),
            pltpu.VMEM((nb * tt, D_RNN), BF16),
        ],
        compiler_params=_params(),
        name="rglru_block",
    )(x, h0.reshape(B, 1, D_RNN), cbuf, row(norm), w_in, cw, row(cb), grw, row(grb), giw, row(gib), row(lam), w_out)


def _ffn_kernel(x_ref, buf_ref, *refs, nb, tt):
    ffn_w = refs[:_FFN_WEIGHTS]
    y_ref, bnew_ref, last_ref, ext_ref, mid_ref = refs[_FFN_WEIGHTS:]
    m = nb * tt

    @pl.when(pl.program_id(1) == 0)
    def _():
        _init_conv_rows(last_ref, buf_ref, FFN_CONV_W)

    def store_out(y):
        y_ref[...] = y.reshape(nb, tt, D_MODEL)

    _ffn_tile(lambda: x_ref[...].reshape(m, D_MODEL), store_out, *ffn_w, bnew_ref,
              last_ref, ext_ref, mid_ref, nb=nb, tt=tt)


def _ffn_tile(load_x, store_y, norm_ref, wup_ref, cw_ref, cb_ref, wdown_ref, bnew_ref,
              last_ref, ext_ref, mid_ref, *, nb, tt):
    m = nb * tt
    tail = SUBLANES - (FFN_CONV_W - 1)
    x = load_x()
    xn = _rmsnorm(x, norm_ref[...]).astype(BF16)
    for c in range(D_FF // FF_CHUNK):
        cs = slice(c * FF_CHUNK, (c + 1) * FF_CHUNK)
        g = jnp.dot(xn, wup_ref[:, cs], preferred_element_type=F32).reshape(nb, tt, FF_CHUNK)
        v = jnp.dot(xn, wup_ref[:, D_FF + c * FF_CHUNK:D_FF + (c + 1) * FF_CHUNK],
                    preferred_element_type=F32)
        if tt == SUBLANES:
            gc, last = _causal_dwconv(g, last_ref[:, :, cs], cw_ref[:, cs], cb_ref[:, cs])
            last_ref[:, :, cs] = last
        else:
            ext_ref[:, tail:SUBLANES, :] = last_ref[:, tail:, cs]
            ext_ref[:, SUBLANES:, :] = g
            gc = cb_ref[:, cs].reshape(1, 1, FF_CHUNK)
            for k in range(FFN_CONV_W):
                gc = gc + (cw_ref[k:k + 1, cs].reshape(1, 1, FF_CHUNK)
                           * ext_ref[:, tail + k:tail + k + tt, :])
            last_ref[:, tail:, cs] = ext_ref[:, tail + tt:SUBLANES + tt, :]
        mid_ref[:, cs] = (jax.nn.gelu(gc).reshape(m, FF_CHUNK) * v).astype(BF16)
    bnew_ref[...] = last_ref[:, tail:, :]
    store_y(x + jnp.dot(mid_ref[...], wdown_ref[...], preferred_element_type=F32))


def _conv_ffn(x, buf, norm, w_up, cw, cb, w_down, *, nb, tt, layer):
    B, T, _ = x.shape
    kern = functools.partial(_ffn_kernel, nb=nb, tt=tt)
    return pl.pallas_call(
        kern,
        grid=(B // nb, T // tt),
        in_specs=[
            pl.BlockSpec((nb, tt, D_MODEL), lambda b, j: (b, j, 0)),
            pl.BlockSpec((None, nb, FFN_CONV_W - 1, D_FF), lambda b, j: (layer, b, 0, 0)),
            _const_spec((1, D_MODEL)),
            _const_spec((D_MODEL, 2 * D_FF), layer),
            _const_spec((FFN_CONV_W, D_FF)),
            _const_spec((1, D_FF)),
            _const_spec((D_FF, D_MODEL), layer),
        ],
        out_specs=[
            pl.BlockSpec((nb, tt, D_MODEL), lambda b, j: (b, j, 0)),
            pl.BlockSpec((nb, FFN_CONV_W - 1, D_FF), lambda b, j: (b, 0, 0)),
        ],
        out_shape=[
            jax.ShapeDtypeStruct((B, T, D_MODEL), F32),
            jax.ShapeDtypeStruct((B, FFN_CONV_W - 1, D_FF), F32),
        ],
        scratch_shapes=[
            pltpu.VMEM((nb, SUBLANES, D_FF), F32),
            pltpu.VMEM((nb, SUBLANES + tt, FF_CHUNK), F32),
            pltpu.VMEM((nb * tt, D_FF), BF16),
        ],
        compiler_params=_params(),
        name="conv_ffn",
    )(x, buf, norm.reshape(1, -1), w_up, cw, cb.reshape(1, -1), w_down)


def _kv_kernel(x_ref, norm_ref, w_ref, b_ref, kn_ref, *rest, nb, tt, key_major_copies):
    m = nb * tt
    x = x_ref[...].reshape(m, D_MODEL)
    xn = _rmsnorm(x, norm_ref[...]).astype(BF16)
    kv = jnp.dot(xn, w_ref[...], preferred_element_type=F32) + b_ref[...]
    k = kv[:, :HKV]
    if key_major_copies:
        wvt_ref, bvt_ref, k_ref, v_ref, kb_ref, vt_ref = rest
        vt = lax.dot_general(wvt_ref[...], xn, (((1,), (1,)), ((), ())),
                             preferred_element_type=F32) + bvt_ref[...]
        for n in range(nb):
            vt_ref[n] = vt[:, n * tt:(n + 1) * tt].astype(BF16)
    else:
        k_ref, v_ref = rest
    rows = lax.broadcasted_iota(jnp.int32, (HKV, HKV), 0) // HEAD_DIM
    cols = lax.broadcasted_iota(jnp.int32, (HKV, HKV), 1) // HEAD_DIM
    avg = jnp.where(rows == cols, 1.0 / HEAD_DIM, 0.0).astype(BF16)
    sq = k * k
    hi = sq.astype(BF16)
    lo = (sq - hi.astype(F32)).astype(BF16)
    ms = (jnp.dot(hi, avg, preferred_element_type=F32)
          + jnp.dot(lo, avg, preferred_element_type=F32))
    kn = (k * lax.rsqrt(ms + EPS) * kn_ref[...]).reshape(nb, tt, HKV)
    v = kv[:, HKV:].reshape(nb, tt, HKV)
    if key_major_copies:
        k_ref[...] = kn[:, tt - WINDOW:, :]
        v_ref[...] = v[:, tt - WINDOW:, :]
        kb_ref[...] = kn.astype(BF16)
    else:
        k_ref[...] = kn
        v_ref[...] = v


def _shared_kv(x, norm, w_kv, b_kv, k_norm, *, nb, tt, key_major_copies):
    B, T, _ = x.shape
    kern = functools.partial(_kv_kernel, nb=nb, tt=tt, key_major_copies=key_major_copies)
    tok_spec = pl.BlockSpec((nb, tt, HKV), lambda b, j: (b, j, 0))
    in_specs = [
        pl.BlockSpec((nb, tt, D_MODEL), lambda b, j: (b, j, 0)),
        _const_spec((1, D_MODEL)),
        _const_spec((D_MODEL, 2 * HKV)),
        _const_spec((1, 2 * HKV)),
        _const_spec((1, HKV)),
    ]
    args = [x, norm.reshape(1, -1), w_kv, b_kv.reshape(1, -1),
            jnp.tile(k_norm, N_KV_HEADS).reshape(1, -1)]
    out_specs = [tok_spec, tok_spec]
    out_shape = [jax.ShapeDtypeStruct((B, T, HKV), F32), jax.ShapeDtypeStruct((B, T, HKV), F32)]
    if key_major_copies:
        assert tt % WINDOW == 0
        last_spec = pl.BlockSpec((nb, WINDOW, HKV), lambda b, j: (b, 0, 0))
        in_specs += [_const_spec((HKV, D_MODEL)), _const_spec((HKV, 1))]
        args += [w_kv[:, HKV:].T, b_kv[HKV:].reshape(-1, 1)]
        out_specs = [last_spec, last_spec, tok_spec,
                     pl.BlockSpec((nb, HKV, tt), lambda b, j: (b, 0, j))]
        out_shape = [jax.ShapeDtypeStruct((B, WINDOW, HKV), F32),
                     jax.ShapeDtypeStruct((B, WINDOW, HKV), F32),
                     jax.ShapeDtypeStruct((B, T, HKV), BF16),
                     jax.ShapeDtypeStruct((B, HKV, T), BF16)]
    return pl.pallas_call(
        kern,
        grid=(B // nb, T // tt),
        in_specs=in_specs,
        out_specs=out_specs,
        out_shape=out_shape,
        compiler_params=_params(),
        name="shared_kv",
    )(*args)


def _rel_buckets(tt):
    qi = np.arange(tt)[:, None]
    sj = np.arange(WINDOW + tt)[None, :]
    dist = qi + WINDOW - sj
    max_exact = N_BUCKETS // 2
    d = np.maximum(dist, 0)
    df = np.maximum(d, 1).astype(np.float32)
    large = max_exact + (np.log(df / max_exact) / math.log(MAX_DISTANCE / max_exact)
                         * (N_BUCKETS - max_exact)).astype(np.int32)
    large = np.minimum(large, N_BUCKETS - 1)
    bucket = np.where(d < max_exact, d, large)
    valid = (dist >= 0) & (dist < WINDOW)
    return np.where(valid, bucket, -1).astype(np.int32)


def _bias_of_buckets(rb_ref, bkt, h):
    acc = jnp.full(bkt.shape, NEG_INF, F32)
    for c in range(N_BUCKETS):
        acc = jnp.where(bkt == c, rb_ref[c, h], acc)
    return acc


def _bias_kernel(rb_ref, bkt_ref, out_ref):
    bkt = bkt_ref[...]

    def head(h, carry):
        out_ref[h] = _bias_of_buckets(rb_ref, bkt, h)
        return carry

    lax.fori_loop(0, N_HEADS, head, 0)


def _bias_t_kernel(rb_ref, bkt_ref, out_ref, *, tt):
    bkt = bkt_ref[...]
    before_tile = lax.broadcasted_iota(jnp.int32, bkt.shape, 0) < WINDOW

    def group(g, carry):
        for hh in range(GROUP):
            acc = _bias_of_buckets(rb_ref, bkt, g * GROUP + hh) * LOG2_E
            out_ref[0, g, :, hh * tt:(hh + 1) * tt] = acc
            out_ref[1, g, :, hh * tt:(hh + 1) * tt] = jnp.where(before_tile, NEG_INF, acc)
        return carry

    lax.fori_loop(0, N_KV_HEADS, group, 0)


def _rel_bias_table(rel_bias, tt, *, keys_on_rows):
    L = WINDOW + tt
    bkt = _rel_buckets(tt)
    if keys_on_rows:
        kern = functools.partial(_bias_t_kernel, tt=tt)
        bkt = bkt.T
        out_shape = jax.ShapeDtypeStruct((2, N_KV_HEADS, L, GROUP * tt), F32)
    else:
        kern = _bias_kernel
        out_shape = jax.ShapeDtypeStruct((N_HEADS, tt, L), F32)
    return pl.pallas_call(
        kern,
        in_specs=[
            pl.BlockSpec(memory_space=pltpu.SMEM),
            pl.BlockSpec(memory_space=pltpu.VMEM),
        ],
        out_specs=pl.BlockSpec(memory_space=pltpu.VMEM),
        out_shape=out_shape,
        name="rel_bias_table",
    )(rel_bias, jnp.asarray(bkt))


def _attn_sample_kernel(sinks_ref, x_ref, kp_ref, kc_ref, vp_ref, vc_ref, bias_ref, norm_ref, wq_ref,
                        bq_ref, qn_ref, wo_ref, bo_ref, y_ref, q_ref, o_ref, *, nb, tt):
    m = nb * tt
    L = WINDOW + tt
    sb = SEQS_PER_STEP
    x = x_ref[...].reshape(m, D_MODEL)
    q = _mm(_rmsnorm(x, norm_ref[...]), wq_ref[...]) + bq_ref[...]
    qgain = qn_ref[...] * (HEAD_DIM ** -0.5)
    for h in range(N_HEADS):
        hs = slice(h * HEAD_DIM, (h + 1) * HEAD_DIM)
        qh = q[:, hs]
        q_ref[:, hs] = qh * lax.rsqrt(jnp.mean(qh * qh, axis=-1, keepdims=True) + EPS) * qgain
    sink = jnp.concatenate([jnp.full((tt, 1), sinks_ref[h], F32) for h in range(N_HEADS)],
                           axis=0)[None]
    bias = bias_ref[...].reshape(1, N_HEADS * tt, L)

    half_of_lane = lax.broadcasted_iota(jnp.int32, (1, LANES), 1) // HEAD_DIM
    zeros_tile = jnp.zeros((sb * tt, LANES), F32)

    def in_kv_lanes(qblk, h):
        g = h // GROUP
        qt = qblk[:, (h // HEADS_PER_TILE) * LANES:(h // HEADS_PER_TILE + 1) * LANES]
        qt = jnp.where(half_of_lane == h % HEADS_PER_TILE, qt, 0.0)
        if h % HEADS_PER_TILE != g % HEADS_PER_TILE:
            qt = pltpu.roll(qt, HEAD_DIM, axis=1)
        tiles = [qt if t == g // HEADS_PER_TILE else zeros_tile for t in range(HKV // LANES)]
        return jnp.concatenate(tiles, axis=1).reshape(sb, tt, HKV)

    def seqs(i, carry):
        n0 = pl.multiple_of(i * sb, sb)
        rows = pl.ds(pl.multiple_of(i * (sb * tt), sb * tt), sb * tt)
        qblk = q_ref[rows, :]
        kall = jnp.concatenate([kp_ref[pl.ds(n0, sb)], kc_ref[pl.ds(n0, sb)]], axis=1).astype(BF16)
        vall = jnp.concatenate([vp_ref[pl.ds(n0, sb)], vc_ref[pl.ds(n0, sb)]], axis=1).astype(BF16)
        qp = jnp.concatenate([in_kv_lanes(qblk, h) for h in range(N_HEADS)], axis=1).astype(BF16)
        s = jnp.einsum('bqd,bkd->bqk', qp, kall, preferred_element_type=F32) + bias
        mx = jnp.maximum(jnp.max(s, axis=-1, keepdims=True), sink)
        p = jnp.exp(s - mx)
        den = jnp.sum(p, axis=-1, keepdims=True) + jnp.exp(sink - mx)
        o = jnp.einsum('bqk,bkd->bqd', p.astype(BF16), vall, preferred_element_type=F32) / den
        for h in range(N_HEADS):
            g = h // GROUP
            o_ref[rows, h * HEAD_DIM:(h + 1) * HEAD_DIM] = (
                o[:, h * tt:(h + 1) * tt, g * HEAD_DIM:(g + 1) * HEAD_DIM].reshape(sb * tt, HEAD_DIM))
        return carry

    lax.fori_loop(0, nb // sb, seqs, 0)
    y = _mm(o_ref[...], wo_ref[...]) + bo_ref[...]
    y_ref[...] = (x + y).reshape(nb, tt, D_MODEL)


def _attn_prompt_kernel(sinks_ref, x_ref, kp_ref, kc_ref, vtp_ref, vtc_ref, bias_ref, norm_ref,
                        wqt_ref, bq_ref, qn_ref, wo_ref, bo_ref, y_ref, qt_ref, ot_ref, s_ref,
                        *, nb, tt):
    m = nb * tt
    x = x_ref[...].reshape(m, D_MODEL)
    xn = _rmsnorm(x, norm_ref[...]).astype(BF16)
    qt = lax.dot_general(wqt_ref[...], xn, (((1,), (1,)), ((), ())),
                         preferred_element_type=F32) + bq_ref[...]
    qgain = qn_ref[...] * (HEAD_DIM ** -0.5 * LOG2_E)
    for h in range(N_HEADS):
        hs = slice(h * HEAD_DIM, (h + 1) * HEAD_DIM)
        qh = qt[hs, :]
        ms = jnp.mean(qh * qh, axis=0, keepdims=True)
        qt_ref[hs, :] = (qh * lax.rsqrt(ms + EPS) * qgain).astype(BF16)

    head_of_lane = lax.broadcasted_iota(jnp.int32, (1, GROUP * tt), 1) // tt
    sinks = []
    for g in range(N_KV_HEADS):
        sink = jnp.full((1, GROUP * tt), sinks_ref[g * GROUP] * LOG2_E, F32)
        for hh in range(1, GROUP):
            sink = jnp.where(head_of_lane == hh, sinks_ref[g * GROUP + hh] * LOG2_E, sink)
        sinks.append(sink)
    ones_rows = jnp.ones((2 * SUBLANES, WINDOW + tt), BF16)

    def scores(g, n):
        gs = slice(g * HEAD_DIM, (g + 1) * HEAD_DIM)
        ts = slice(n * tt, (n + 1) * tt)
        kk = jnp.concatenate([kp_ref[n, :, gs], kc_ref[n, :, gs]], axis=0)
        q4 = jnp.concatenate([qt_ref[h * HEAD_DIM:(h + 1) * HEAD_DIM, ts]
                              for h in range(g * GROUP, (g + 1) * GROUP)], axis=1)
        return jnp.dot(kk, q4, preferred_element_type=F32)

    pairs = [(g, n) for g in range(N_KV_HEADS) for n in range(nb)]
    for idx, (g, n) in enumerate(pairs):
        s_ref[idx] = scores(g, n) + bias_ref[0, g]
    for idx, (g, n) in enumerate(pairs):
        gs = slice(g * HEAD_DIM, (g + 1) * HEAD_DIM)
        ts = slice(n * tt, (n + 1) * tt)
        s = s_ref[idx]
        mx = jnp.maximum(jnp.max(s, axis=0, keepdims=True), sinks[g])
        p = jnp.exp2(s - mx).astype(BF16)
        vt = jnp.concatenate([vtp_ref[n, gs, :], vtc_ref[n, gs, :]], axis=1)
        pv = jnp.dot(jnp.concatenate([vt, ones_rows], axis=0), p,
                     preferred_element_type=F32)
        den = pv[HEAD_DIM:HEAD_DIM + 1, :] + jnp.exp2(sinks[g] - mx)
        o = pv[:HEAD_DIM, :] / den
        for hh in range(GROUP):
            h = g * GROUP + hh
            ot_ref[h * HEAD_DIM:(h + 1) * HEAD_DIM, ts] = o[:, hh * tt:(hh + 1) * tt]
    y = _mm(ot_ref[...].T, wo_ref[...]) + bo_ref[...]
    y_ref[...] = (x + y).reshape(nb, tt, D_MODEL)


def _const_spec3(shape, layer=None):
    nd = len(shape)
    if layer is None:
        return pl.BlockSpec(shape, lambda b, j, s: (0,) * nd, pipeline_mode=pl.Buffered(1))
    return pl.BlockSpec((None,) + shape, lambda b, j, s: (layer,) + (0,) * nd,
                        pipeline_mode=pl.Buffered(1))


def _attn_sample_block(x, k_win, k_new, v_win, v_new, bias, sinks, norm, w_q, b_q, q_norm, w_o, b_o,
                       *, nb, layer):
    B, tt, _ = x.shape
    L = WINDOW + tt
    assert nb % SEQS_PER_STEP == 0 and tt % SUBLANES == 0
    kern = functools.partial(_attn_sample_kernel, nb=nb, tt=tt)
    seq_map = lambda b, j, s: (b, 0, 0)
    grid_spec = pltpu.PrefetchScalarGridSpec(
        num_scalar_prefetch=1,
        grid=(B // nb, 1),
        in_specs=[
            pl.BlockSpec((nb, tt, D_MODEL), seq_map),
            pl.BlockSpec((nb, WINDOW, HKV), seq_map),
            pl.BlockSpec((nb, tt, HKV), seq_map),
            pl.BlockSpec((nb, WINDOW, HKV), seq_map),
            pl.BlockSpec((nb, tt, HKV), seq_map),
            _const_spec3((N_HEADS, tt, L)),
            _const_spec3((1, D_MODEL)),
            _const_spec3((D_MODEL, HQ), layer),
            _const_spec3((1, HQ)),
            _const_spec3((1, HEAD_DIM)),
            _const_spec3((HQ, D_MODEL), layer),
            _const_spec3((1, D_MODEL)),
        ],
        out_specs=pl.BlockSpec((nb, tt, D_MODEL), seq_map),
        scratch_shapes=[
            pltpu.VMEM((nb * tt, HQ), F32),
            pltpu.VMEM((nb * tt, HQ), F32),
        ],
    )
    return pl.pallas_call(
        kern,
        grid_spec=grid_spec,
        out_shape=jax.ShapeDtypeStruct((B, tt, D_MODEL), F32),
        compiler_params=_params(),
        name="swa_sample",
    )(sinks, x, k_win, k_new, v_win, v_new, bias, norm.reshape(1, -1), w_q, b_q.reshape(1, -1),
      q_norm.reshape(1, -1), w_o, b_o.reshape(1, -1))


def _attn_prompt_block(x, kb, vt, bias_t, sinks, norm, w_qt, b_q, q_norm, w_o, b_o, *, nb, layer):
    B, T, _ = x.shape
    tt = WINDOW
    L = WINDOW + tt
    kern = functools.partial(_attn_prompt_kernel, nb=nb, tt=tt)
    cur_map = lambda b, j, s: (b, j, 0)
    prev_map = lambda b, j, s: (b, jnp.maximum(j - 1, 0), 0)
    grid_spec = pltpu.PrefetchScalarGridSpec(
        num_scalar_prefetch=1,
        grid=(B // nb, T // tt),
        in_specs=[
            pl.BlockSpec((nb, tt, D_MODEL), cur_map),
            pl.BlockSpec((nb, tt, HKV), prev_map),
            pl.BlockSpec((nb, tt, HKV), cur_map),
            pl.BlockSpec((nb, HKV, tt), lambda b, j, s: (b, 0, jnp.maximum(j - 1, 0))),
            pl.BlockSpec((nb, HKV, tt), lambda b, j, s: (b, 0, j)),
            pl.BlockSpec((1, N_KV_HEADS, L, GROUP * tt),
                         lambda b, j, s: (jnp.where(j == 0, 1, 0), 0, 0, 0)),
            _const_spec3((1, D_MODEL)),
            _const_spec3((HQ, D_MODEL), layer),
            _const_spec3((HQ, 1)),
            _const_spec3((HEAD_DIM, 1)),
            _const_spec3((HQ, D_MODEL), layer),
            _const_spec3((1, D_MODEL)),
        ],
        out_specs=pl.BlockSpec((nb, tt, D_MODEL), cur_map),
        scratch_shapes=[
            pltpu.VMEM((HQ, nb * tt), BF16),
            pltpu.VMEM((HQ, nb * tt), F32),
            pltpu.VMEM((N_KV_HEADS * nb, L, GROUP * tt), F32),
        ],
    )
    return pl.pallas_call(
        kern,
        grid_spec=grid_spec,
        out_shape=jax.ShapeDtypeStruct((B, T, D_MODEL), F32),
        compiler_params=_params(),
        name="swa_prompt",
    )(sinks, x, kb, kb, vt, vt, bias_t, norm.reshape(1, -1), w_qt, b_q.reshape(-1, 1),
      q_norm.reshape(-1, 1), w_o, b_o.reshape(1, -1))


def _trunk(x, lru_h, lru_conv, ffn_conv, win_k, win_v, P, bias, *, nb, tt, prompt):
    B, T, _ = x.shape
    new_h, new_c, new_f = [], [], []
    k = v = kb = vt = None
    for layer in range(DEPTH):
        if layer < N_A:
            i = layer
            lru_w = (P['a_norm'][i], P['a_w_in'], P['a_conv_w'][i], P['a_conv_b'][i],
                     P['a_gate_r_w'], P['a_gate_r_b'][i], P['a_gate_i_w'], P['a_gate_i_b'][i],
                     P['a_lambda'][i], P['a_w_out'])
            ffn_w = (P['f_norm'][layer], P['f_w_up'], P['f_conv_w'][layer], P['f_conv_b'][layer],
                     P['f_w_down'])
            x, h_i, c_i = _lru_block(x, lru_h[i], lru_conv[i], *lru_w, nb=nb,
                                     tt=min(T, LRU_ROWS // nb), layer=i)
            x, f_i = _conv_ffn(x, ffn_conv, *ffn_w, nb=nb, tt=tt, layer=layer)
            new_h.append(h_i.reshape(B, D_RNN))
            new_c.append(c_i)
        else:
            j = layer - N_A
            if prompt:
                x = _attn_prompt_block(x, kb, vt, bias, P['sinks'][j], P['b_norm'][j], P['w_qt'],
                                       P['b_q'][j], P['q_norm'][j], P['w_o'], P['b_o'][j],
                                       nb=nb, layer=j)
            else:
                x = _attn_sample_block(x, win_k, k, win_v, v, bias, P['sinks'][j], P['b_norm'][j],
                                       P['w_q'], P['b_q'][j], P['q_norm'][j], P['w_o'], P['b_o'][j],
                                       nb=nb, layer=j)
            x, f_i = _conv_ffn(x, ffn_conv, P['f_norm'][layer], P['f_w_up'],
                               P['f_conv_w'][layer], P['f_conv_b'][layer], P['f_w_down'],
                               nb=nb, tt=min(T, FFN_ROWS // nb), layer=layer)
        new_f.append(f_i)
        if layer == N_A - 1:
            kv = _shared_kv(x, P['kv_norm'], P['w_kv'], P['b_kv'], P['k_norm'], nb=nb,
                            tt=min(T, LRU_ROWS // nb), key_major_copies=prompt)
            if prompt:
                k, v, kb, vt = kv
            else:
                k, v = kv
    if prompt:
        new_k, new_v = k, v
    else:
        new_k = jnp.concatenate([win_k, k], axis=1)[:, -WINDOW:]
        new_v = jnp.concatenate([win_v, v], axis=1)[:, -WINDOW:]
    shape4 = (B, WINDOW, N_KV_HEADS, HEAD_DIM)
    return (x, jnp.stack(new_h), jnp.stack(new_c), jnp.stack(new_f),
            new_k.reshape(shape4), new_v.reshape(shape4))


def kernel(x_prompt, x_sample, state_lru_h, state_lru_conv, state_ffn_conv, cache_k_win, cache_v_win,
           a_norm, a_w_in, a_conv_w, a_conv_b, a_gate_r_w, a_gate_r_b, a_gate_i_w, a_gate_i_b,
           a_lambda, a_w_out, kv_norm, w_kv, b_kv, k_norm, b_norm, w_q, b_q, q_norm, sinks,
           w_o, b_o, rel_bias, f_norm, f_w_up, f_conv_w, f_conv_b, f_w_down):
    bf = lambda w: w.astype(BF16)
    P = dict(a_norm=a_norm, a_w_in=bf(a_w_in), a_conv_w=a_conv_w, a_conv_b=a_conv_b,
             a_gate_r_w=bf(a_gate_r_w), a_gate_r_b=a_gate_r_b, a_gate_i_w=bf(a_gate_i_w),
             a_gate_i_b=a_gate_i_b, a_lambda=a_lambda, a_w_out=bf(a_w_out), kv_norm=kv_norm,
             w_kv=bf(w_kv), b_kv=b_kv, k_norm=k_norm, b_norm=b_norm, w_q=bf(w_q), b_q=b_q,
             q_norm=q_norm, sinks=sinks, w_o=bf(w_o), b_o=b_o, f_norm=f_norm, f_w_up=bf(f_w_up),
             f_conv_w=f_conv_w, f_conv_b=f_conv_b, f_w_down=bf(f_w_down))
    P['w_qt'] = jnp.swapaxes(P['w_q'], 1, 2)
    B, S, _ = x_prompt.shape
    DB, DT, _ = x_sample.shape

    zero_h = jnp.zeros((N_A, B, D_RNN), F32)
    zero_c = jnp.zeros((N_A, B, CONV_W - 1, D_RNN), F32)
    zero_f = jnp.zeros((DEPTH, B, FFN_CONV_W - 1, D_FF), F32)
    out_p = _trunk(x_prompt, zero_h, zero_c, zero_f, None, None, P,
                   _rel_bias_table(rel_bias, WINDOW, keys_on_rows=True), nb=B, tt=WINDOW,
                   prompt=True)

    win_k = cache_k_win.reshape(DB, WINDOW, HKV)
    win_v = cache_v_win.reshape(DB, WINDOW, HKV)
    out_s = _trunk(x_sample, state_lru_h, state_lru_conv, state_ffn_conv, win_k, win_v, P,
                   _rel_bias_table(rel_bias, DT, keys_on_rows=False), nb=DB // 2, tt=DT,
                   prompt=False)
    return (out_p[0], out_s[0]) + out_p[1:] + out_s[1:]
```

```python
import functools
import math

import numpy as np
import jax
import jax.numpy as jnp
from jax import lax
from jax.experimental import pallas as pl
from jax.experimental.pallas import tpu as pltpu

D_MODEL = 1024
DEPTH = 4
N_A = DEPTH // 2
D_RNN = D_MODEL
N_LRU_BLOCKS = 4
LRU_BW = D_RNN // N_LRU_BLOCKS
CONV_W = 4
C_GATE = 8.0
N_HEADS = 16
N_KV_HEADS = 4
HEAD_DIM = 64
GROUP = N_HEADS // N_KV_HEADS
HQ = N_HEADS * HEAD_DIM
HKV = N_KV_HEADS * HEAD_DIM
WINDOW = 128
N_BUCKETS = 32
MAX_DISTANCE = 128
D_FF = 3 * D_MODEL
FFN_CONV_W = 3
EPS = 1e-6
NEG_INF = -1e30
LOG2_E = math.log2(math.e)

F32 = jnp.float32
BF16 = jnp.bfloat16

SUBLANES = 8
LANES = 128
HEADS_PER_TILE = LANES // HEAD_DIM
FF_CHUNK = 256
FFN_ROWS = 512
LRU_ROWS = 1024
ATTN_SEQS = 64
SEQS_PER_STEP = 4
V7X_VMEM_BYTES = 64 * 1024 * 1024
VMEM_LIMIT_BYTES = V7X_VMEM_BYTES - 8 * 1024 * 1024


def _rmsnorm(x, g):
    return x * lax.rsqrt(jnp.mean(x * x, axis=-1, keepdims=True) + EPS) * g


def _softplus(x):
    return jnp.maximum(x, 0.0) + jnp.log(1.0 + jnp.exp(-jnp.abs(x)))


def _mm(a, b):
    return jnp.dot(a.astype(BF16), b, preferred_element_type=F32)


def _causal_dwconv(x, last_group, w, b):
    nb, tt, c = x.shape
    taps = w.shape[0]
    groups = tt // SUBLANES
    x4 = x.reshape(nb, groups, SUBLANES, c)
    t = lax.broadcasted_iota(jnp.int32, (1, 1, SUBLANES, c), 2)
    tap = lambda k: w[k:k + 1, :].reshape(1, 1, 1, c)
    y = b.reshape(1, 1, 1, c) + tap(taps - 1) * x4
    for s in range(1, taps):
        cur = pltpu.roll(x4.reshape(nb * groups, SUBLANES, c), s, axis=1)
        cur = cur.reshape(nb, groups, SUBLANES, c)
        before = pltpu.roll(last_group, s, axis=1)[:, None]
        if groups > 1:
            before = jnp.concatenate([before, cur[:, :groups - 1]], axis=1)
        y = y + tap(taps - 1 - s) * jnp.where(t >= s, cur, before)
    return y.reshape(nb, tt, c), x4[:, groups - 1]


def _tile(batch, seq_len, rows):
    tt = min(seq_len, max(SUBLANES, rows // batch))
    return dict(nb=min(batch, rows // tt), tt=tt)


def _const_spec(shape, layer=None):
    nd = len(shape)
    if layer is None:
        return pl.BlockSpec(shape, lambda b, j: (0,) * nd, pipeline_mode=pl.Buffered(1))
    return pl.BlockSpec((None,) + shape, lambda b, j: (layer,) + (0,) * nd,
                        pipeline_mode=pl.Buffered(1))


def _params():
    return pltpu.CompilerParams(dimension_semantics=("arbitrary", "arbitrary"),
                                vmem_limit_bytes=VMEM_LIMIT_BYTES)


def _lru_kernel(x_ref, h0_ref, cbuf_ref, norm_ref, win_ref, cw_ref, cb_ref, grw_ref, grb_ref,
                giw_ref, gib_ref, lam_ref, wout_ref, y_ref, hlast_ref, cnew_ref,
                last_ref, h_ref, mid_ref, *, nb, tt):
    m = nb * tt
    tail = SUBLANES - (CONV_W - 1)

    @pl.when(pl.program_id(1) == 0)
    def _():
        _init_conv_rows(last_ref, cbuf_ref, CONV_W)
        h_ref[...] = h0_ref[...]

    x = x_ref[...].reshape(m, D_MODEL)
    xn = _rmsnorm(x, norm_ref[...]).astype(BF16)
    decay = (-C_GATE * LOG2_E) * _softplus(-lam_ref[...])
    for n in range(N_LRU_BLOCKS):
        cs = slice(n * LRU_BW, (n + 1) * LRU_BW)
        xb = jnp.dot(xn, win_ref[:, D_RNN + n * LRU_BW:D_RNN + (n + 1) * LRU_BW],
                     preferred_element_type=F32)
        xc, last = _causal_dwconv(xb.reshape(nb, tt, LRU_BW), last_ref[:, :, cs],
                                  cw_ref[:, cs], cb_ref[:, cs])
        last_ref[:, :, cs] = last
        xc = xc.reshape(m, LRU_BW)
        xcb = xc.astype(BF16)
        r = jax.nn.sigmoid(jnp.dot(xcb, grw_ref[n], preferred_element_type=F32) + grb_ref[:, cs])
        i = jax.nn.sigmoid(jnp.dot(xcb, giw_ref[n], preferred_element_type=F32) + gib_ref[:, cs])
        a = jnp.exp2(r * decay[:, cs])
        b = jnp.exp2(0.5 * jnp.log2(1.0 - a * a)) * (i * xc)
        hs, h_last = _linear_scan(a.reshape(nb, tt, LRU_BW), b.reshape(nb, tt, LRU_BW),
                                  h_ref[:, :, cs])
        h_ref[:, :, cs] = h_last
        gate = jnp.dot(xn, win_ref[:, cs], preferred_element_type=F32)
        mid_ref[:, cs] = (jax.nn.gelu(gate) * hs.reshape(m, LRU_BW)).astype(BF16)
    cnew_ref[...] = last_ref[:, tail:, :]
    hlast_ref[...] = h_ref[...]
    y = jnp.dot(mid_ref[...], wout_ref[...], preferred_element_type=F32)
    y_ref[...] = (x + y).reshape(nb, tt, D_MODEL)


def _linear_scan(a, b, h0):
    nb, tt, c = a.shape
    groups = tt // SUBLANES
    a = a.reshape(nb * groups, SUBLANES, c)
    b = b.reshape(nb * groups, SUBLANES, c)
    t = lax.broadcasted_iota(jnp.int32, a.shape, 1)
    d = 1
    while d < SUBLANES:
        keep = t >= d
        b = jnp.where(keep, a * pltpu.roll(b, d, axis=1) + b, b)
        a = jnp.where(keep, a * pltpu.roll(a, d, axis=1), a)
        d *= 2
    a = a.reshape(nb, groups, SUBLANES, c)
    b = b.reshape(nb, groups, SUBLANES, c)
    h = h0
    out = []
    for g in range(groups):
        hg = a[:, g] * h + b[:, g]
        out.append(hg)
        h = hg[:, SUBLANES - 1:SUBLANES, :]
    return jnp.concatenate(out, axis=1), h


def _init_conv_rows(last_ref, buf_ref, taps):
    nb, _, c = last_ref.shape
    tail = SUBLANES - (taps - 1)
    last_ref[:, :tail, :] = jnp.zeros((nb, tail, c), F32)
    last_ref[:, tail:, :] = buf_ref[...]


def _lru_block(x, h0, cbuf, norm, w_in, cw, cb, grw, grb, giw, gib, lam, w_out, *, nb, tt, layer):
    B, T, _ = x.shape
    kern = functools.partial(_lru_kernel, nb=nb, tt=tt)
    row = lambda v: v.reshape(1, -1)
    return pl.pallas_call(
        kern,
        grid=(B // nb, T // tt),
        in_specs=[
            pl.BlockSpec((nb, tt, D_MODEL), lambda b, j: (b, j, 0)),
            pl.BlockSpec((nb, 1, D_RNN), lambda b, j: (b, 0, 0)),
            pl.BlockSpec((nb, CONV_W - 1, D_RNN), lambda b, j: (b, 0, 0)),
            _const_spec((1, D_MODEL)),
            _const_spec((D_MODEL, 2 * D_RNN), layer),
            _const_spec((CONV_W, D_RNN)),
            _const_spec((1, D_RNN)),
            _const_spec((N_LRU_BLOCKS, LRU_BW, LRU_BW), layer),
            _const_spec((1, D_RNN)),
            _const_spec((N_LRU_BLOCKS, LRU_BW, LRU_BW), layer),
            _const_spec((1, D_RNN)),
            _const_spec((1, D_RNN)),
            _const_spec((D_RNN, D_MODEL), layer),
        ],
        out_specs=[
            pl.BlockSpec((nb, tt, D_MODEL), lambda b, j: (b, j, 0)),
            pl.BlockSpec((nb, 1, D_RNN), lambda b, j: (b, 0, 0)),
            pl.BlockSpec((nb, CONV_W - 1, D_RNN), lambda b, j: (b, 0, 0)),
        ],
        out_shape=[
            jax.ShapeDtypeStruct((B, T, D_MODEL), F32),
            jax.ShapeDtypeStruct((B, 1, D_RNN), F32),
            jax.ShapeDtypeStruct((B, CONV_W - 1, D_RNN), F32),
        ],
        scratch_shapes=[
            pltpu.VMEM((nb, SUBLANES, D_RNN), F32),
            pltpu.VMEM((nb, 1, D_RNN), F32),
            pltpu.VMEM((nb * tt, D_RNN), BF16),
        ],
        compiler_params=_params(),
        name="rglru_block",
    )(x, h0.reshape(B, 1, D_RNN), cbuf, row(norm), w_in, cw, row(cb), grw, row(grb), giw, row(gib), row(lam), w_out)


def _ffn_kernel(x_ref, buf_ref, norm_ref, wup_ref, cw_ref, cb_ref, wdown_ref, y_ref, bnew_ref,
                last_ref, ext_ref, mid_ref, *, nb, tt):
    m = nb * tt
    tail = SUBLANES - (FFN_CONV_W - 1)

    @pl.when(pl.program_id(1) == 0)
    def _():
        _init_conv_rows(last_ref, buf_ref, FFN_CONV_W)

    x = x_ref[...].reshape(m, D_MODEL)
    xn = _rmsnorm(x, norm_ref[...]).astype(BF16)
    for c in range(D_FF // FF_CHUNK):
        cs = slice(c * FF_CHUNK, (c + 1) * FF_CHUNK)
        g = jnp.dot(xn, wup_ref[:, cs], preferred_element_type=F32).reshape(nb, tt, FF_CHUNK)
        v = jnp.dot(xn, wup_ref[:, D_FF + c * FF_CHUNK:D_FF + (c + 1) * FF_CHUNK],
                    preferred_element_type=F32)
        if tt == SUBLANES:
            gc, last = _causal_dwconv(g, last_ref[:, :, cs], cw_ref[:, cs], cb_ref[:, cs])
            last_ref[:, :, cs] = last
        else:
            ext_ref[:, tail:SUBLANES, :] = last_ref[:, tail:, cs]
            ext_ref[:, SUBLANES:, :] = g
            gc = cb_ref[:, cs].reshape(1, 1, FF_CHUNK)
            for k in range(FFN_CONV_W):
                gc = gc + (cw_ref[k:k + 1, cs].reshape(1, 1, FF_CHUNK)
                           * ext_ref[:, tail + k:tail + k + tt, :])
            last_ref[:, tail:, cs] = ext_ref[:, tail + tt:SUBLANES + tt, :]
        mid_ref[:, cs] = (jax.nn.gelu(gc).reshape(m, FF_CHUNK) * v).astype(BF16)
    bnew_ref[...] = last_ref[:, tail:, :]
    y = jnp.dot(mid_ref[...], wdown_ref[...], preferred_element_type=F32)
    y_ref[...] = (x + y).reshape(nb, tt, D_MODEL)


def _conv_ffn(x, buf, norm, w_up, cw, cb, w_down, *, nb, tt, layer):
    B, T, _ = x.shape
    kern = functools.partial(_ffn_kernel, nb=nb, tt=tt)
    return pl.pallas_call(
        kern,
        grid=(B // nb, T // tt),
        in_specs=[
            pl.BlockSpec((nb, tt, D_MODEL), lambda b, j: (b, j, 0)),
            pl.BlockSpec((None, nb, FFN_CONV_W - 1, D_FF), lambda b, j: (layer, b, 0, 0)),
            _const_spec((1, D_MODEL)),
            _const_spec((D_MODEL, 2 * D_FF), layer),
            _const_spec((FFN_CONV_W, D_FF)),
            _const_spec((1, D_FF)),
            _const_spec((D_FF, D_MODEL), layer),
        ],
        out_specs=[
            pl.BlockSpec((nb, tt, D_MODEL), lambda b, j: (b, j, 0)),
            pl.BlockSpec((nb, FFN_CONV_W - 1, D_FF), lambda b, j: (b, 0, 0)),
        ],
        out_shape=[
            jax.ShapeDtypeStruct((B, T, D_MODEL), F32),
            jax.ShapeDtypeStruct((B, FFN_CONV_W - 1, D_FF), F32),
        ],
        scratch_shapes=[
            pltpu.VMEM((nb, SUBLANES, D_FF), F32),
            pltpu.VMEM((nb, SUBLANES + tt, FF_CHUNK), F32),
            pltpu.VMEM((nb * tt, D_FF), BF16),
        ],
        compiler_params=_params(),
        name="conv_ffn",
    )(x, buf, norm.reshape(1, -1), w_up, cw, cb.reshape(1, -1), w_down)


def _kv_kernel(x_ref, norm_ref, w_ref, b_ref, kn_ref, *rest, nb, tt, key_major_copies):
    m = nb * tt
    x = x_ref[...].reshape(m, D_MODEL)
    xn = _rmsnorm(x, norm_ref[...]).astype(BF16)
    kv = jnp.dot(xn, w_ref[...], preferred_element_type=F32) + b_ref[...]
    k = kv[:, :HKV]
    if key_major_copies:
        wvt_ref, bvt_ref, k_ref, v_ref, kb_ref, vt_ref = rest
        vt = lax.dot_general(wvt_ref[...], xn, (((1,), (1,)), ((), ())),
                             preferred_element_type=F32) + bvt_ref[...]
        for n in range(nb):
            vt_ref[n] = vt[:, n * tt:(n + 1) * tt].astype(BF16)
    else:
        k_ref, v_ref = rest
    rows = lax.broadcasted_iota(jnp.int32, (HKV, HKV), 0) // HEAD_DIM
    cols = lax.broadcasted_iota(jnp.int32, (HKV, HKV), 1) // HEAD_DIM
    avg = jnp.where(rows == cols, 1.0 / HEAD_DIM, 0.0).astype(BF16)
    sq = k * k
    hi = sq.astype(BF16)
    lo = (sq - hi.astype(F32)).astype(BF16)
    ms = (jnp.dot(hi, avg, preferred_element_type=F32)
          + jnp.dot(lo, avg, preferred_element_type=F32))
    kn = (k * lax.rsqrt(ms + EPS) * kn_ref[...]).reshape(nb, tt, HKV)
    v = kv[:, HKV:].reshape(nb, tt, HKV)
    if key_major_copies:
        k_ref[...] = kn[:, tt - WINDOW:, :]
        v_ref[...] = v[:, tt - WINDOW:, :]
        kb_ref[...] = kn.astype(BF16)
    else:
        k_ref[...] = kn
        v_ref[...] = v


def _shared_kv(x, norm, w_kv, b_kv, k_norm, *, nb, tt, key_major_copies):
    B, T, _ = x.shape
    kern = functools.partial(_kv_kernel, nb=nb, tt=tt, key_major_copies=key_major_copies)
    tok_spec = pl.BlockSpec((nb, tt, HKV), lambda b, j: (b, j, 0))
    in_specs = [
        pl.BlockSpec((nb, tt, D_MODEL), lambda b, j: (b, j, 0)),
        _const_spec((1, D_MODEL)),
        _const_spec((D_MODEL, 2 * HKV)),
        _const_spec((1, 2 * HKV)),
        _const_spec((1, HKV)),
    ]
    args = [x, norm.reshape(1, -1), w_kv, b_kv.reshape(1, -1),
            jnp.tile(k_norm, N_KV_HEADS).reshape(1, -1)]
    out_specs = [tok_spec, tok_spec]
    out_shape = [jax.ShapeDtypeStruct((B, T, HKV), F32), jax.ShapeDtypeStruct((B, T, HKV), F32)]
    if key_major_copies:
        assert tt % WINDOW == 0
        last_spec = pl.BlockSpec((nb, WINDOW, HKV), lambda b, j: (b, 0, 0))
        in_specs += [_const_spec((HKV, D_MODEL)), _const_spec((HKV, 1))]
        args += [w_kv[:, HKV:].T, b_kv[HKV:].reshape(-1, 1)]
        out_specs = [last_spec, last_spec, tok_spec,
                     pl.BlockSpec((nb, HKV, tt), lambda b, j: (b, 0, j))]
        out_shape = [jax.ShapeDtypeStruct((B, WINDOW, HKV), F32),
                     jax.ShapeDtypeStruct((B, WINDOW, HKV), F32),
                     jax.ShapeDtypeStruct((B, T, HKV), BF16),
                     jax.ShapeDtypeStruct((B, HKV, T), BF16)]
    return pl.pallas_call(
        kern,
        grid=(B // nb, T // tt),
        in_specs=in_specs,
        out_specs=out_specs,
        out_shape=out_shape,
        compiler_params=_params(),
        name="shared_kv",
    )(*args)


def _rel_buckets(tt):
    qi = np.arange(tt)[:, None]
    sj = np.arange(WINDOW + tt)[None, :]
    dist = qi + WINDOW - sj
    max_exact = N_BUCKETS // 2
    d = np.maximum(dist, 0)
    df = np.maximum(d, 1).astype(np.float32)
    large = max_exact + (np.log(df / max_exact) / math.log(MAX_DISTANCE / max_exact)
                         * (N_BUCKETS - max_exact)).astype(np.int32)
    large = np.minimum(large, N_BUCKETS - 1)
    bucket = np.where(d < max_exact, d, large)
    valid = (dist >= 0) & (dist < WINDOW)
    return np.where(valid, bucket, -1).astype(np.int32)


def _bias_of_buckets(rb_ref, bkt, h):
    acc = jnp.full(bkt.shape, NEG_INF, F32)
    for c in range(N_BUCKETS):
        acc = jnp.where(bkt == c, rb_ref[c, h], acc)
    return acc


def _bias_kernel(rb_ref, bkt_ref, out_ref):
    bkt = bkt_ref[...]

    def head(h, carry):
        out_ref[h] = _bias_of_buckets(rb_ref, bkt, h)
        return carry

    lax.fori_loop(0, N_HEADS, head, 0)


def _bias_t_kernel(rb_ref, bkt_ref, out_ref, *, tt):
    bkt = bkt_ref[...]
    before_tile = lax.broadcasted_iota(jnp.int32, bkt.shape, 0) < WINDOW

    def group(g, carry):
        for hh in range(GROUP):
            acc = _bias_of_buckets(rb_ref, bkt, g * GROUP + hh) * LOG2_E
            out_ref[0, g, :, hh * tt:(hh + 1) * tt] = acc
            out_ref[1, g, :, hh * tt:(hh + 1) * tt] = jnp.where(before_tile, NEG_INF, acc)
        return carry

    lax.fori_loop(0, N_KV_HEADS, group, 0)


def _rel_bias_table(rel_bias, tt, *, keys_on_rows):
    L = WINDOW + tt
    bkt = _rel_buckets(tt)
    if keys_on_rows:
        kern = functools.partial(_bias_t_kernel, tt=tt)
        bkt = bkt.T
        out_shape = jax.ShapeDtypeStruct((2, N_KV_HEADS, L, GROUP * tt), F32)
    else:
        kern = _bias_kernel
        out_shape = jax.ShapeDtypeStruct((N_HEADS, tt, L), F32)
    return pl.pallas_call(
        kern,
        in_specs=[
            pl.BlockSpec(memory_space=pltpu.SMEM),
            pl.BlockSpec(memory_space=pltpu.VMEM),
        ],
        out_specs=pl.BlockSpec(memory_space=pltpu.VMEM),
        out_shape=out_shape,
        name="rel_bias_table",
    )(rel_bias, jnp.asarray(bkt))


def _attn_sample_kernel(sinks_ref, x_ref, kp_ref, kc_ref, vp_ref, vc_ref, bias_ref, norm_ref, wq_ref,
                        bq_ref, qn_ref, wo_ref, bo_ref, y_ref, q_ref, o_ref, *, nb, tt):
    m = nb * tt
    L = WINDOW + tt
    sb = SEQS_PER_STEP
    x = x_ref[...].reshape(m, D_MODEL)
    q = _mm(_rmsnorm(x, norm_ref[...]), wq_ref[...]) + bq_ref[...]
    qgain = qn_ref[...] * (HEAD_DIM ** -0.5)
    for h in range(N_HEADS):
        hs = slice(h * HEAD_DIM, (h + 1) * HEAD_DIM)
        qh = q[:, hs]
        q_ref[:, hs] = qh * lax.rsqrt(jnp.mean(qh * qh, axis=-1, keepdims=True) + EPS) * qgain
    sink = jnp.concatenate([jnp.full((tt, 1), sinks_ref[h], F32) for h in range(N_HEADS)],
                           axis=0)[None]
    bias = bias_ref[...].reshape(1, N_HEADS * tt, L)

    half_of_lane = lax.broadcasted_iota(jnp.int32, (1, LANES), 1) // HEAD_DIM
    zeros_tile = jnp.zeros((sb * tt, LANES), F32)

    def in_kv_lanes(qblk, h):
        g = h // GROUP
        qt = qblk[:, (h // HEADS_PER_TILE) * LANES:(h // HEADS_PER_TILE + 1) * LANES]
        qt = jnp.where(half_of_lane == h % HEADS_PER_TILE, qt, 0.0)
        if h % HEADS_PER_TILE != g % HEADS_PER_TILE:
            qt = pltpu.roll(qt, HEAD_DIM, axis=1)
        tiles = [qt if t == g // HEADS_PER_TILE else zeros_tile for t in range(HKV // LANES)]
        return jnp.concatenate(tiles, axis=1).reshape(sb, tt, HKV)

    def seqs(i, carry):
        n0 = pl.multiple_of(i * sb, sb)
        rows = pl.ds(pl.multiple_of(i * (sb * tt), sb * tt), sb * tt)
        qblk = q_ref[rows, :]
        kall = jnp.concatenate([kp_ref[pl.ds(n0, sb)], kc_ref[pl.ds(n0, sb)]], axis=1).astype(BF16)
        vall = jnp.concatenate([vp_ref[pl.ds(n0, sb)], vc_ref[pl.ds(n0, sb)]], axis=1).astype(BF16)
        qp = jnp.concatenate([in_kv_lanes(qblk, h) for h in range(N_HEADS)], axis=1).astype(BF16)
        s = jnp.einsum('bqd,bkd->bqk', qp, kall, preferred_element_type=F32) + bias
        mx = jnp.maximum(jnp.max(s, axis=-1, keepdims=True), sink)
        p = jnp.exp(s - mx)
        den = jnp.sum(p, axis=-1, keepdims=True) + jnp.exp(sink - mx)
        o = jnp.einsum('bqk,bkd->bqd', p.astype(BF16), vall, preferred_element_type=F32) / den
        for h in range(N_HEADS):
            g = h // GROUP
            o_ref[rows, h * HEAD_DIM:(h + 1) * HEAD_DIM] = (
                o[:, h * tt:(h + 1) * tt, g * HEAD_DIM:(g + 1) * HEAD_DIM].reshape(sb * tt, HEAD_DIM))
        return carry

    lax.fori_loop(0, nb // sb, seqs, 0)
    y = _mm(o_ref[...], wo_ref[...]) + bo_ref[...]
    y_ref[...] = (x + y).reshape(nb, tt, D_MODEL)


def _attn_prompt_kernel(sinks_ref, x_ref, kp_ref, kc_ref, vtp_ref, vtc_ref, bias_ref, norm_ref,
                        wqt_ref, bq_ref, qn_ref, wo_ref, bo_ref, y_ref, qt_ref, ot_ref, s_ref,
                        *, nb, tt):
    m = nb * tt
    x = x_ref[...].reshape(m, D_MODEL)
    xn = _rmsnorm(x, norm_ref[...]).astype(BF16)
    qt = lax.dot_general(wqt_ref[...], xn, (((1,), (1,)), ((), ())),
                         preferred_element_type=F32) + bq_ref[...]
    qgain = qn_ref[...] * (HEAD_DIM ** -0.5 * LOG2_E)
    for h in range(N_HEADS):
        hs = slice(h * HEAD_DIM, (h + 1) * HEAD_DIM)
        qh = qt[hs, :]
        ms = jnp.mean(qh * qh, axis=0, keepdims=True)
        qt_ref[hs, :] = (qh * lax.rsqrt(ms + EPS) * qgain).astype(BF16)

    head_of_lane = lax.broadcasted_iota(jnp.int32, (1, GROUP * tt), 1) // tt
    sinks = []
    for g in range(N_KV_HEADS):
        sink = jnp.full((1, GROUP * tt), sinks_ref[g * GROUP] * LOG2_E, F32)
        for hh in range(1, GROUP):
            sink = jnp.where(head_of_lane == hh, sinks_ref[g * GROUP + hh] * LOG2_E, sink)
        sinks.append(sink)
    ones_rows = jnp.ones((2 * SUBLANES, WINDOW + tt), BF16)

    def scores(g, n):
        gs = slice(g * HEAD_DIM, (g + 1) * HEAD_DIM)
        ts = slice(n * tt, (n + 1) * tt)
        kk = jnp.concatenate([kp_ref[n, :, gs], kc_ref[n, :, gs]], axis=0)
        q4 = jnp.concatenate([qt_ref[h * HEAD_DIM:(h + 1) * HEAD_DIM, ts]
                              for h in range(g * GROUP, (g + 1) * GROUP)], axis=1)
        return jnp.dot(kk, q4, preferred_element_type=F32)

    pairs = [(g, n) for g in range(N_KV_HEADS) for n in range(nb)]
    for idx, (g, n) in enumerate(pairs):
        s_ref[idx] = scores(g, n) + bias_ref[0, g]
    for idx, (g, n) in enumerate(pairs):
        gs = slice(g * HEAD_DIM, (g + 1) * HEAD_DIM)
        ts = slice(n * tt, (n + 1) * tt)
        s = s_ref[idx]
        mx = jnp.maximum(jnp.max(s, axis=0, keepdims=True), sinks[g])
        p = jnp.exp2(s - mx).astype(BF16)
        vt = jnp.concatenate([vtp_ref[n, gs, :], vtc_ref[n, gs, :]], axis=1)
        pv = jnp.dot(jnp.concatenate([vt, ones_rows], axis=0), p,
                     preferred_element_type=F32)
        den = pv[HEAD_DIM:HEAD_DIM + 1, :] + jnp.exp2(sinks[g] - mx)
        o = pv[:HEAD_DIM, :] / den
        for hh in range(GROUP):
            h = g * GROUP + hh
            ot_ref[h * HEAD_DIM:(h + 1) * HEAD_DIM, ts] = o[:, hh * tt:(hh + 1) * tt]
    y = _mm(ot_ref[...].T, wo_ref[...]) + bo_ref[...]
    y_ref[...] = (x + y).reshape(nb, tt, D_MODEL)


def _const_spec3(shape, layer=None):
    nd = len(shape)
    if layer is None:
        return pl.BlockSpec(shape, lambda b, j, s: (0,) * nd, pipeline_mode=pl.Buffered(1))
    return pl.BlockSpec((None,) + shape, lambda b, j, s: (layer,) + (0,) * nd,
                        pipeline_mode=pl.Buffered(1))


def _attn_sample_block(x, k_win, k_new, v_win, v_new, bias, sinks, norm, w_q, b_q, q_norm, w_o, b_o,
                       *, nb, layer):
    B, tt, _ = x.shape
    L = WINDOW + tt
    assert nb % SEQS_PER_STEP == 0 and tt % SUBLANES == 0
    kern = functools.partial(_attn_sample_kernel, nb=nb, tt=tt)
    seq_map = lambda b, j, s: (b, 0, 0)
    grid_spec = pltpu.PrefetchScalarGridSpec(
        num_scalar_prefetch=1,
        grid=(B // nb, 1),
        in_specs=[
            pl.BlockSpec((nb, tt, D_MODEL), seq_map),
            pl.BlockSpec((nb, WINDOW, HKV), seq_map),
            pl.BlockSpec((nb, tt, HKV), seq_map),
            pl.BlockSpec((nb, WINDOW, HKV), seq_map),
            pl.BlockSpec((nb, tt, HKV), seq_map),
            _const_spec3((N_HEADS, tt, L)),
            _const_spec3((1, D_MODEL)),
            _const_spec3((D_MODEL, HQ), layer),
            _const_spec3((1, HQ)),
            _const_spec3((1, HEAD_DIM)),
            _const_spec3((HQ, D_MODEL), layer),
            _const_spec3((1, D_MODEL)),
        ],
        out_specs=pl.BlockSpec((nb, tt, D_MODEL), seq_map),
        scratch_shapes=[
            pltpu.VMEM((nb * tt, HQ), F32),
            pltpu.VMEM((nb * tt, HQ), F32),
        ],
    )
    return pl.pallas_call(
        kern,
        grid_spec=grid_spec,
        out_shape=jax.ShapeDtypeStruct((B, tt, D_MODEL), F32),
        compiler_params=_params(),
        name="swa_sample",
    )(sinks, x, k_win, k_new, v_win, v_new, bias, norm.reshape(1, -1), w_q, b_q.reshape(1, -1),
      q_norm.reshape(1, -1), w_o, b_o.reshape(1, -1))


def _attn_prompt_block(x, kb, vt, bias_t, sinks, norm, w_qt, b_q, q_norm, w_o, b_o, *, nb, layer):
    B, T, _ = x.shape
    tt = WINDOW
    L = WINDOW + tt
    kern = functools.partial(_attn_prompt_kernel, nb=nb, tt=tt)
    cur_map = lambda b, j, s: (b, j, 0)
    prev_map = lambda b, j, s: (b, jnp.maximum(j - 1, 0), 0)
    grid_spec = pltpu.PrefetchScalarGridSpec(
        num_scalar_prefetch=1,
        grid=(B // nb, T // tt),
        in_specs=[
            pl.BlockSpec((nb, tt, D_MODEL), cur_map),
            pl.BlockSpec((nb, tt, HKV), prev_map),
            pl.BlockSpec((nb, tt, HKV), cur_map),
            pl.BlockSpec((nb, HKV, tt), lambda b, j, s: (b, 0, jnp.maximum(j - 1, 0))),
            pl.BlockSpec((nb, HKV, tt), lambda b, j, s: (b, 0, j)),
            pl.BlockSpec((1, N_KV_HEADS, L, GROUP * tt),
                         lambda b, j, s: (jnp.where(j == 0, 1, 0), 0, 0, 0)),
            _const_spec3((1, D_MODEL)),
            _const_spec3((HQ, D_MODEL), layer),
            _const_spec3((HQ, 1)),
            _const_spec3((HEAD_DIM, 1)),
            _const_spec3((HQ, D_MODEL), layer),
            _const_spec3((1, D_MODEL)),
        ],
        out_specs=pl.BlockSpec((nb, tt, D_MODEL), cur_map),
        scratch_shapes=[
            pltpu.VMEM((HQ, nb * tt), BF16),
            pltpu.VMEM((HQ, nb * tt), F32),
            pltpu.VMEM((N_KV_HEADS * nb, L, GROUP * tt), F32),
        ],
    )
    return pl.pallas_call(
        kern,
        grid_spec=grid_spec,
        out_shape=jax.ShapeDtypeStruct((B, T, D_MODEL), F32),
        compiler_params=_params(),
        name="swa_prompt",
    )(sinks, x, kb, kb, vt, vt, bias_t, norm.reshape(1, -1), w_qt, b_q.reshape(-1, 1),
      q_norm.reshape(-1, 1), w_o, b_o.reshape(1, -1))


def kernel(x_prompt, x_sample, state_lru_h, state_lru_conv, state_ffn_conv, cache_k_win, cache_v_win,
           a_norm, a_w_in, a_conv_w, a_conv_b, a_gate_r_w, a_gate_r_b, a_gate_i_w, a_gate_i_b,
           a_lambda, a_w_out, kv_norm, w_kv, b_kv, k_norm, b_norm, w_q, b_q, q_norm, sinks,
           w_o, b_o, rel_bias, f_norm, f_w_up, f_conv_w, f_conv_b, f_w_down):
    bf = lambda w: w.astype(BF16)
    a_w_in, a_gate_r_w, a_gate_i_w, a_w_out = bf(a_w_in), bf(a_gate_r_w), bf(a_gate_i_w), bf(a_w_out)
    w_kv, w_q, w_o, f_w_up, f_w_down = bf(w_kv), bf(w_q), bf(w_o), bf(f_w_up), bf(f_w_down)
    w_qt = jnp.swapaxes(w_q, 1, 2)
    B, S, _ = x_prompt.shape
    DB, DT, _ = x_sample.shape
    tile_p, tile_s = _tile(B, S, LRU_ROWS), _tile(DB, DT, LRU_ROWS)

    zero_h = jnp.zeros((N_A, B, D_RNN), F32)
    zero_c = jnp.zeros((N_A, B, CONV_W - 1, D_RNN), F32)
    zero_f = jnp.zeros((DEPTH, B, FFN_CONV_W - 1, D_FF), F32)
    win_k = cache_k_win.reshape(DB, WINDOW, HKV)
    win_v = cache_v_win.reshape(DB, WINDOW, HKV)
    bias_p = _rel_bias_table(rel_bias, WINDOW, keys_on_rows=True)
    bias_s = _rel_bias_table(rel_bias, DT, keys_on_rows=False)

    xp, xs = x_prompt, x_sample
    h_p, c_p, f_p, h_s, c_s, f_s = [], [], [], [], [], []
    for layer in range(DEPTH):
        if layer < N_A:
            i = layer
            lru_w = (a_norm[i], a_w_in, a_conv_w[i], a_conv_b[i], a_gate_r_w, a_gate_r_b[i],
                     a_gate_i_w, a_gate_i_b[i], a_lambda[i], a_w_out)
            xp, h, c = _lru_block(xp, zero_h[i], zero_c[i], *lru_w, **tile_p, layer=i)
            h_p.append(h.reshape(B, D_RNN))
            c_p.append(c)
            xs, h, c = _lru_block(xs, state_lru_h[i], state_lru_conv[i], *lru_w, **tile_s, layer=i)
            h_s.append(h.reshape(DB, D_RNN))
            c_s.append(c)
        else:
            j = layer - N_A
            xp = _attn_prompt_block(xp, kb_p, vt_p, bias_p, sinks[j], b_norm[j], w_qt, b_q[j],
                                    q_norm[j], w_o, b_o[j], nb=B, layer=j)
            xs = _attn_sample_block(xs, win_k, k_s, win_v, v_s, bias_s, sinks[j], b_norm[j], w_q,
                                    b_q[j], q_norm[j], w_o, b_o[j], nb=min(DB, ATTN_SEQS), layer=j)
        ffn_w = (f_norm[layer], f_w_up, f_conv_w[layer], f_conv_b[layer], f_w_down)
        xp, f = _conv_ffn(xp, zero_f, *ffn_w, **_tile(B, S, FFN_ROWS), layer=layer)
        f_p.append(f)
        xs, f = _conv_ffn(xs, state_ffn_conv, *ffn_w, **_tile(DB, DT, FFN_ROWS), layer=layer)
        f_s.append(f)
        if layer == N_A - 1:
            k_p, v_p, kb_p, vt_p = _shared_kv(xp, kv_norm, w_kv, b_kv, k_norm, **tile_p,
                                              key_major_copies=True)
            k_s, v_s = _shared_kv(xs, kv_norm, w_kv, b_kv, k_norm, **tile_s,
                                  key_major_copies=False)
    k_s = jnp.concatenate([win_k, k_s], axis=1)[:, -WINDOW:]
    v_s = jnp.concatenate([win_v, v_s], axis=1)[:, -WINDOW:]
    heads = lambda kv: kv.reshape(kv.shape[0], WINDOW, N_KV_HEADS, HEAD_DIM)
    return (xp, xs,
            jnp.stack(h_p), jnp.stack(c_p), jnp.stack(f_p), heads(k_p), heads(v_p),
            jnp.stack(h_s), jnp.stack(c_s), jnp.stack(f_s), heads(k_s), heads(v_s))
```

```python
import functools
import math

import numpy as np
import jax
import jax.numpy as jnp
from jax import lax
from jax.experimental import pallas as pl
from jax.experimental.pallas import tpu as pltpu

D_MODEL = 1024
DEPTH = 4
N_A = DEPTH // 2
D_RNN = D_MODEL
N_LRU_BLOCKS = 4
LRU_BW = D_RNN // N_LRU_BLOCKS
CONV_W = 4
C_GATE = 8.0
N_HEADS = 16
N_KV_HEADS = 4
HEAD_DIM = 64
GROUP = N_HEADS // N_KV_HEADS
HQ = N_HEADS * HEAD_DIM
HKV = N_KV_HEADS * HEAD_DIM
WINDOW = 128
N_BUCKETS = 32
MAX_DISTANCE = 128
D_FF = 3 * D_MODEL
FFN_CONV_W = 3
EPS = 1e-6
NEG_INF = -1e30
LOG2_E = math.log2(math.e)

F32 = jnp.float32
BF16 = jnp.bfloat16

SUBLANES = 8
LANES = 128
HEADS_PER_TILE = LANES // HEAD_DIM
FF_CHUNK = 256
FFN_ROWS = 512
LRU_ROWS = 1024
ATTN_SEQS = 64
SEQS_PER_STEP = 4
V7X_VMEM_BYTES = 64 * 1024 * 1024
VMEM_LIMIT_BYTES = V7X_VMEM_BYTES - 8 * 1024 * 1024


def _rmsnorm(x, g):
    return x * lax.rsqrt(jnp.mean(x * x, axis=-1, keepdims=True) + EPS) * g


def _softplus(x):
    return jnp.maximum(x, 0.0) + jnp.log(1.0 + jnp.exp(-jnp.abs(x)))


def _mm(a, b):
    return jnp.dot(a.astype(BF16), b, preferred_element_type=F32)


def _causal_dwconv(x, last_group, w, b):
    nb, tt, c = x.shape
    taps = w.shape[0]
    groups = tt // SUBLANES
    x4 = x.reshape(nb, groups, SUBLANES, c)
    t = lax.broadcasted_iota(jnp.int32, (1, 1, SUBLANES, c), 2)
    tap = lambda k: w[k:k + 1, :].reshape(1, 1, 1, c)
    y = b.reshape(1, 1, 1, c) + tap(taps - 1) * x4
    for s in range(1, taps):
        cur = pltpu.roll(x4.reshape(nb * groups, SUBLANES, c), s, axis=1)
        cur = cur.reshape(nb, groups, SUBLANES, c)
        before = pltpu.roll(last_group, s, axis=1)[:, None]
        if groups > 1:
            before = jnp.concatenate([before, cur[:, :groups - 1]], axis=1)
        y = y + tap(taps - 1 - s) * jnp.where(t >= s, cur, before)
    return y.reshape(nb, tt, c), x4[:, groups - 1]


def _tile(batch, seq_len, rows):
    tt = min(seq_len, max(SUBLANES, rows // batch))
    return dict(nb=min(batch, rows // tt), tt=tt)


def _const_spec(shape, layer=None):
    nd = len(shape)
    if layer is None:
        return pl.BlockSpec(shape, lambda b, j: (0,) * nd, pipeline_mode=pl.Buffered(1))
    return pl.BlockSpec((None,) + shape, lambda b, j: (layer,) + (0,) * nd,
                        pipeline_mode=pl.Buffered(1))


def _params():
    return pltpu.CompilerParams(dimension_semantics=("arbitrary", "arbitrary"),
                                vmem_limit_bytes=VMEM_LIMIT_BYTES)


def _lru_kernel(x_ref, h0_ref, cbuf_ref, norm_ref, win_ref, cw_ref, cb_ref, grw_ref, grb_ref,
                giw_ref, gib_ref, lam_ref, wout_ref, y_ref, hlast_ref, cnew_ref,
                last_ref, h_ref, mid_ref, *, nb, tt):
    m = nb * tt
    tail = SUBLANES - (CONV_W - 1)

    @pl.when(pl.program_id(1) == 0)
    def _():
        _init_conv_rows(last_ref, cbuf_ref, CONV_W)
        h_ref[...] = h0_ref[...]

    x = x_ref[...].reshape(m, D_MODEL)
    xn = _rmsnorm(x, norm_ref[...]).astype(BF16)
    decay = (-C_GATE * LOG2_E) * _softplus(-lam_ref[...])
    for n in range(N_LRU_BLOCKS):
        cs = slice(n * LRU_BW, (n + 1) * LRU_BW)
        xb = jnp.dot(xn, win_ref[:, D_RNN + n * LRU_BW:D_RNN + (n + 1) * LRU_BW],
                     preferred_element_type=F32)
        xc, last = _causal_dwconv(xb.reshape(nb, tt, LRU_BW), last_ref[:, :, cs],
                                  cw_ref[:, cs], cb_ref[:, cs])
        last_ref[:, :, cs] = last
        xc = xc.reshape(m, LRU_BW)
        xcb = xc.astype(BF16)
        r = jax.nn.sigmoid(jnp.dot(xcb, grw_ref[n], preferred_element_type=F32) + grb_ref[:, cs])
        i = jax.nn.sigmoid(jnp.dot(xcb, giw_ref[n], preferred_element_type=F32) + gib_ref[:, cs])
        a = jnp.exp2(r * decay[:, cs])
        b = jnp.exp2(0.5 * jnp.log2(1.0 - a * a)) * (i * xc)
        hs, h_last = _linear_scan(a.reshape(nb, tt, LRU_BW), b.reshape(nb, tt, LRU_BW),
                                  h_ref[:, :, cs])
        h_ref[:, :, cs] = h_last
        gate = jnp.dot(xn, win_ref[:, cs], preferred_element_type=F32)
        mid_ref[:, cs] = (jax.nn.gelu(gate) * hs.reshape(m, LRU_BW)).astype(BF16)
    cnew_ref[...] = last_ref[:, tail:, :]
    hlast_ref[...] = h_ref[...]
    y = jnp.dot(mid_ref[...], wout_ref[...], preferred_element_type=F32)
    y_ref[...] = (x + y).reshape(nb, tt, D_MODEL)


def _linear_scan(a, b, h0):
    nb, tt, c = a.shape
    groups = tt // SUBLANES
    a = a.reshape(nb * groups, SUBLANES, c)
    b = b.reshape(nb * groups, SUBLANES, c)
    t = lax.broadcasted_iota(jnp.int32, a.shape, 1)
    d = 1
    while d < SUBLANES:
        keep = t >= d
        b = jnp.where(keep, a * pltpu.roll(b, d, axis=1) + b, b)
        a = jnp.where(keep, a * pltpu.roll(a, d, axis=1), a)
        d *= 2
    a = a.reshape(nb, groups, SUBLANES, c)
    b = b.reshape(nb, groups, SUBLANES, c)
    h = h0
    out = []
    for g in range(groups):
        hg = a[:, g] * h + b[:, g]
        out.append(hg)
        h = hg[:, SUBLANES - 1:SUBLANES, :]
    return jnp.concatenate(out, axis=1), h


def _init_conv_rows(last_ref, buf_ref, taps):
    nb, _, c = last_ref.shape
    tail = SUBLANES - (taps - 1)
    last_ref[:, :tail, :] = jnp.zeros((nb, tail, c), F32)
    last_ref[:, tail:, :] = buf_ref[...]


def _lru_block(x, h0, cbuf, norm, w_in, cw, cb, grw, grb, giw, gib, lam, w_out, *, nb, tt, layer):
    B, T, _ = x.shape
    kern = functools.partial(_lru_kernel, nb=nb, tt=tt)
    row = lambda v: v.reshape(1, -1)
    return pl.pallas_call(
        kern,
        grid=(B // nb, T // tt),
        in_specs=[
            pl.BlockSpec((nb, tt, D_MODEL), lambda b, j: (b, j, 0)),
            pl.BlockSpec((nb, 1, D_RNN), lambda b, j: (b, 0, 0)),
            pl.BlockSpec((nb, CONV_W - 1, D_RNN), lambda b, j: (b, 0, 0)),
            _const_spec((1, D_MODEL)),
            _const_spec((D_MODEL, 2 * D_RNN), layer),
            _const_spec((CONV_W, D_RNN)),
            _const_spec((1, D_RNN)),
            _const_spec((N_LRU_BLOCKS, LRU_BW, LRU_BW), layer),
            _const_spec((1, D_RNN)),
            _const_spec((N_LRU_BLOCKS, LRU_BW, LRU_BW), layer),
            _const_spec((1, D_RNN)),
            _const_spec((1, D_RNN)),
            _const_spec((D_RNN, D_MODEL), layer),
        ],
        out_specs=[
            pl.BlockSpec((nb, tt, D_MODEL), lambda b, j: (b, j, 0)),
            pl.BlockSpec((nb, 1, D_RNN), lambda b, j: (b, 0, 0)),
            pl.BlockSpec((nb, CONV_W - 1, D_RNN), lambda b, j: (b, 0, 0)),
        ],
        out_shape=[
            jax.ShapeDtypeStruct((B, T, D_MODEL), F32),
            jax.ShapeDtypeStruct((B, 1, D_RNN), F32),
            jax.ShapeDtypeStruct((B, CONV_W - 1, D_RNN), F32),
        ],
        scratch_shapes=[
            pltpu.VMEM((nb, SUBLANES, D_RNN), F32),
            pltpu.VMEM((nb, 1, D_RNN), F32),
            pltpu.VMEM((nb * tt, D_RNN), BF16),
        ],
        compiler_params=_params(),
        name="rglru_block",
    )(x, h0.reshape(B, 1, D_RNN), cbuf, row(norm), w_in, cw, row(cb), grw, row(grb), giw, row(gib), row(lam), w_out)


def _ffn_kernel(x_ref, buf_ref, norm_ref, wup_ref, cw_ref, cb_ref, wdown_ref, y_ref, bnew_ref,
                last_ref, ext_ref, mid_ref, *, nb, tt):
    m = nb * tt
    tail = SUBLANES - (FFN_CONV_W - 1)

    @pl.when(pl.program_id(1) == 0)
    def _():
        _init_conv_rows(last_ref, buf_ref, FFN_CONV_W)

    x = x_ref[...].reshape(m, D_MODEL)
    xn = _rmsnorm(x, norm_ref[...]).astype(BF16)
    for c in range(D_FF // FF_CHUNK):
        cs = slice(c * FF_CHUNK, (c + 1) * FF_CHUNK)
        g = jnp.dot(xn, wup_ref[:, cs], preferred_element_type=F32).reshape(nb, tt, FF_CHUNK)
        v = jnp.dot(xn, wup_ref[:, D_FF + c * FF_CHUNK:D_FF + (c + 1) * FF_CHUNK],
                    preferred_element_type=F32)
        if tt == SUBLANES:
            gc, last = _causal_dwconv(g, last_ref[:, :, cs], cw_ref[:, cs], cb_ref[:, cs])
            last_ref[:, :, cs] = last
        else:
            ext_ref[:, tail:SUBLANES, :] = last_ref[:, tail:, cs]
            ext_ref[:, SUBLANES:, :] = g
            gc = cb_ref[:, cs].reshape(1, 1, FF_CHUNK)
            for k in range(FFN_CONV_W):
                gc = gc + (cw_ref[k:k + 1, cs].reshape(1, 1, FF_CHUNK)
                           * ext_ref[:, tail + k:tail + k + tt, :])
            last_ref[:, tail:, cs] = ext_ref[:, tail + tt:SUBLANES + tt, :]
        mid_ref[:, cs] = (jax.nn.gelu(gc).reshape(m, FF_CHUNK) * v).astype(BF16)
    bnew_ref[...] = last_ref[:, tail:, :]
    y = jnp.dot(mid_ref[...], wdown_ref[...], preferred_element_type=F32)
    y_ref[...] = (x + y).reshape(nb, tt, D_MODEL)


def _conv_ffn(x, buf, norm, w_up, cw, cb, w_down, *, nb, tt, layer):
    B, T, _ = x.shape
    kern = functools.partial(_ffn_kernel, nb=nb, tt=tt)
    return pl.pallas_call(
        kern,
        grid=(B // nb, T // tt),
        in_specs=[
            pl.BlockSpec((nb, tt, D_MODEL), lambda b, j: (b, j, 0)),
            pl.BlockSpec((None, nb, FFN_CONV_W - 1, D_FF), lambda b, j: (layer, b, 0, 0)),
            _const_spec((1, D_MODEL)),
            _const_spec((D_MODEL, 2 * D_FF), layer),
            _const_spec((FFN_CONV_W, D_FF)),
            _const_spec((1, D_FF)),
            _const_spec((D_FF, D_MODEL), layer),
        ],
        out_specs=[
            pl.BlockSpec((nb, tt, D_MODEL), lambda b, j: (b, j, 0)),
            pl.BlockSpec((nb, FFN_CONV_W - 1, D_FF), lambda b, j: (b, 0, 0)),
        ],
        out_shape=[
            jax.ShapeDtypeStruct((B, T, D_MODEL), F32),
            jax.ShapeDtypeStruct((B, FFN_CONV_W - 1, D_FF), F32),
        ],
        scratch_shapes=[
            pltpu.VMEM((nb, SUBLANES, D_FF), F32),
            pltpu.VMEM((nb, SUBLANES + tt, FF_CHUNK), F32),
            pltpu.VMEM((nb * tt, D_FF), BF16),
        ],
        compiler_params=_params(),
        name="conv_ffn",
    )(x, buf, norm.reshape(1, -1), w_up, cw, cb.reshape(1, -1), w_down)


def _kv_kernel(x_ref, norm_ref, w_ref, b_ref, kn_ref, *rest, nb, tt, key_major_copies):
    m = nb * tt
    x = x_ref[...].reshape(m, D_MODEL)
    xn = _rmsnorm(x, norm_ref[...]).astype(BF16)
    kv = jnp.dot(xn, w_ref[...], preferred_element_type=F32) + b_ref[...]
    k = kv[:, :HKV]
    if key_major_copies:
        wvt_ref, bvt_ref, k_ref, v_ref, kb_ref, vt_ref = rest
        vt = lax.dot_general(wvt_ref[...], xn, (((1,), (1,)), ((), ())),
                             preferred_element_type=F32) + bvt_ref[...]
        for n in range(nb):
            vt_ref[n] = vt[:, n * tt:(n + 1) * tt].astype(BF16)
    else:
        k_ref, v_ref = rest
    rows = lax.broadcasted_iota(jnp.int32, (HKV, HKV), 0) // HEAD_DIM
    cols = lax.broadcasted_iota(jnp.int32, (HKV, HKV), 1) // HEAD_DIM
    avg = jnp.where(rows == cols, 1.0 / HEAD_DIM, 0.0).astype(BF16)
    sq = k * k
    hi = sq.astype(BF16)
    lo = (sq - hi.astype(F32)).astype(BF16)
    ms = (jnp.dot(hi, avg, preferred_element_type=F32)
          + jnp.dot(lo, avg, preferred_element_type=F32))
    kn = (k * lax.rsqrt(ms + EPS) * kn_ref[...]).reshape(nb, tt, HKV)
    v = kv[:, HKV:].reshape(nb, tt, HKV)
    if key_major_copies:
        k_ref[...] = kn[:, tt - WINDOW:, :]
        v_ref[...] = v[:, tt - WINDOW:, :]
        kb_ref[...] = kn.astype(BF16)
    else:
        k_ref[...] = kn
        v_ref[...] = v


def _shared_kv(x, norm, w_kv, b_kv, k_norm, *, nb, tt, key_major_copies):
    B, T, _ = x.shape
    kern = functools.partial(_kv_kernel, nb=nb, tt=tt, key_major_copies=key_major_copies)
    tok_spec = pl.BlockSpec((nb, tt, HKV), lambda b, j: (b, j, 0))
    in_specs = [
        pl.BlockSpec((nb, tt, D_MODEL), lambda b, j: (b, j, 0)),
        _const_spec((1, D_MODEL)),
        _const_spec((D_MODEL, 2 * HKV)),
        _const_spec((1, 2 * HKV)),
        _const_spec((1, HKV)),
    ]
    args = [x, norm.reshape(1, -1), w_kv, b_kv.reshape(1, -1),
            jnp.tile(k_norm, N_KV_HEADS).reshape(1, -1)]
    out_specs = [tok_spec, tok_spec]
    out_shape = [jax.ShapeDtypeStruct((B, T, HKV), F32), jax.ShapeDtypeStruct((B, T, HKV), F32)]
    if key_major_copies:
        assert tt % WINDOW == 0
        last_spec = pl.BlockSpec((nb, WINDOW, HKV), lambda b, j: (b, 0, 0))
        in_specs += [_const_spec((HKV, D_MODEL)), _const_spec((HKV, 1))]
        args += [w_kv[:, HKV:].T, b_kv[HKV:].reshape(-1, 1)]
        out_specs = [last_spec, last_spec, tok_spec,
                     pl.BlockSpec((nb, HKV, tt), lambda b, j: (b, 0, j))]
        out_shape = [jax.ShapeDtypeStruct((B, WINDOW, HKV), F32),
                     jax.ShapeDtypeStruct((B, WINDOW, HKV), F32),
                     jax.ShapeDtypeStruct((B, T, HKV), BF16),
                     jax.ShapeDtypeStruct((B, HKV, T), BF16)]
    return pl.pallas_call(
        kern,
        grid=(B // nb, T // tt),
        in_specs=in_specs,
        out_specs=out_specs,
        out_shape=out_shape,
        compiler_params=_params(),
        name="shared_kv",
    )(*args)


def _rel_buckets(tt):
    qi = np.arange(tt)[:, None]
    sj = np.arange(WINDOW + tt)[None, :]
    dist = qi + WINDOW - sj
    max_exact = N_BUCKETS // 2
    d = np.maximum(dist, 0)
    df = np.maximum(d, 1).astype(np.float32)
    large = max_exact + (np.log(df / max_exact) / math.log(MAX_DISTANCE / max_exact)
                         * (N_BUCKETS - max_exact)).astype(np.int32)
    large = np.minimum(large, N_BUCKETS - 1)
    bucket = np.where(d < max_exact, d, large)
    valid = (dist >= 0) & (dist < WINDOW)
    return np.where(valid, bucket, -1).astype(np.int32)


def _bias_of_buckets(rb_ref, bkt, h):
    acc = jnp.full(bkt.shape, NEG_INF, F32)
    for c in range(N_BUCKETS):
        acc = jnp.where(bkt == c, rb_ref[c, h], acc)
    return acc


def _bias_kernel(rb_ref, bkt_ref, out_ref):
    bkt = bkt_ref[...]

    def head(h, carry):
        out_ref[h] = _bias_of_buckets(rb_ref, bkt, h)
        return carry

    lax.fori_loop(0, N_HEADS, head, 0)


def _bias_t_kernel(rb_ref, bkt_ref, out_ref, *, tt):
    bkt = bkt_ref[...]
    before_tile = lax.broadcasted_iota(jnp.int32, bkt.shape, 0) < WINDOW

    def group(g, carry):
        for hh in range(GROUP):
            acc = _bias_of_buckets(rb_ref, bkt, g * GROUP + hh) * LOG2_E
            out_ref[0, g, :, hh * tt:(hh + 1) * tt] = acc
            out_ref[1, g, :, hh * tt:(hh + 1) * tt] = jnp.where(before_tile, NEG_INF, acc)
        return carry

    lax.fori_loop(0, N_KV_HEADS, group, 0)


def _rel_bias_table(rel_bias, tt, *, keys_on_rows):
    L = WINDOW + tt
    bkt = _rel_buckets(tt)
    if keys_on_rows:
        kern = functools.partial(_bias_t_kernel, tt=tt)
        bkt = bkt.T
        out_shape = jax.ShapeDtypeStruct((2, N_KV_HEADS, L, GROUP * tt), F32)
    else:
        kern = _bias_kernel
        out_shape = jax.ShapeDtypeStruct((N_HEADS, tt, L), F32)
    return pl.pallas_call(
        kern,
        in_specs=[
            pl.BlockSpec(memory_space=pltpu.SMEM),
            pl.BlockSpec(memory_space=pltpu.VMEM),
        ],
        out_specs=pl.BlockSpec(memory_space=pltpu.VMEM),
        out_shape=out_shape,
        name="rel_bias_table",
    )(rel_bias, jnp.asarray(bkt))


def _attn_sample_kernel(sinks_ref, x_ref, kp_ref, kc_ref, vp_ref, vc_ref, bias_ref, norm_ref, wq_ref,
                        bq_ref, qn_ref, wo_ref, bo_ref, y_ref, q_ref, o_ref, *, nb, tt):
    m = nb * tt
    L = WINDOW + tt
    sb = SEQS_PER_STEP
    x = x_ref[...].reshape(m, D_MODEL)
    q = _mm(_rmsnorm(x, norm_ref[...]), wq_ref[...]) + bq_ref[...]
    qgain = qn_ref[...] * (HEAD_DIM ** -0.5)
    for h in range(N_HEADS):
        hs = slice(h * HEAD_DIM, (h + 1) * HEAD_DIM)
        qh = q[:, hs]
        q_ref[:, hs] = qh * lax.rsqrt(jnp.mean(qh * qh, axis=-1, keepdims=True) + EPS) * qgain
    sink = jnp.concatenate([jnp.full((tt, 1), sinks_ref[h], F32) for h in range(N_HEADS)],
                           axis=0)[None]
    bias = bias_ref[...].reshape(1, N_HEADS * tt, L)

    half_of_lane = lax.broadcasted_iota(jnp.int32, (1, LANES), 1) // HEAD_DIM
    zeros_tile = jnp.zeros((sb * tt, LANES), F32)

    def in_kv_lanes(qblk, h):
        g = h // GROUP
        qt = qblk[:, (h // HEADS_PER_TILE) * LANES:(h // HEADS_PER_TILE + 1) * LANES]
        qt = jnp.where(half_of_lane == h % HEADS_PER_TILE, qt, 0.0)
        if h % HEADS_PER_TILE != g % HEADS_PER_TILE:
            qt = pltpu.roll(qt, HEAD_DIM, axis=1)
        tiles = [qt if t == g // HEADS_PER_TILE else zeros_tile for t in range(HKV // LANES)]
        return jnp.concatenate(tiles, axis=1).reshape(sb, tt, HKV)

    def seqs(i, carry):
        n0 = pl.multiple_of(i * sb, sb)
        rows = pl.ds(pl.multiple_of(i * (sb * tt), sb * tt), sb * tt)
        qblk = q_ref[rows, :]
        kall = jnp.concatenate([kp_ref[pl.ds(n0, sb)], kc_ref[pl.ds(n0, sb)].astype(BF16)], axis=1)
        vall = jnp.concatenate([vp_ref[pl.ds(n0, sb)], vc_ref[pl.ds(n0, sb)].astype(BF16)], axis=1)
        qp = jnp.concatenate([in_kv_lanes(qblk, h) for h in range(N_HEADS)], axis=1).astype(BF16)
        s = jnp.einsum('bqd,bkd->bqk', qp, kall, preferred_element_type=F32) + bias
        mx = jnp.maximum(jnp.max(s, axis=-1, keepdims=True), sink)
        p = jnp.exp(s - mx)
        den = jnp.sum(p, axis=-1, keepdims=True) + jnp.exp(sink - mx)
        o = jnp.einsum('bqk,bkd->bqd', p.astype(BF16), vall, preferred_element_type=F32) / den
        for h in range(N_HEADS):
            g = h // GROUP
            o_ref[rows, h * HEAD_DIM:(h + 1) * HEAD_DIM] = (
                o[:, h * tt:(h + 1) * tt, g * HEAD_DIM:(g + 1) * HEAD_DIM].reshape(sb * tt, HEAD_DIM))
        return carry

    lax.fori_loop(0, nb // sb, seqs, 0)
    y = _mm(o_ref[...], wo_ref[...]) + bo_ref[...]
    y_ref[...] = (x + y).reshape(nb, tt, D_MODEL)


def _attn_prompt_kernel(sinks_ref, x_ref, kp_ref, kc_ref, vtp_ref, vtc_ref, bias_ref, norm_ref,
                        wqt_ref, bq_ref, qn_ref, wo_ref, bo_ref, y_ref, qt_ref, ot_ref, s_ref,
                        *, nb, tt):
    m = nb * tt
    x = x_ref[...].reshape(m, D_MODEL)
    xn = _rmsnorm(x, norm_ref[...]).astype(BF16)
    qt = lax.dot_general(wqt_ref[...], xn, (((1,), (1,)), ((), ())),
                         preferred_element_type=F32) + bq_ref[...]
    qgain = qn_ref[...] * (HEAD_DIM ** -0.5 * LOG2_E)
    for h in range(N_HEADS):
        hs = slice(h * HEAD_DIM, (h + 1) * HEAD_DIM)
        qh = qt[hs, :]
        ms = jnp.mean(qh * qh, axis=0, keepdims=True)
        qt_ref[hs, :] = (qh * lax.rsqrt(ms + EPS) * qgain).astype(BF16)

    head_of_lane = lax.broadcasted_iota(jnp.int32, (1, GROUP * tt), 1) // tt
    sinks = []
    for g in range(N_KV_HEADS):
        sink = jnp.full((1, GROUP * tt), sinks_ref[g * GROUP] * LOG2_E, F32)
        for hh in range(1, GROUP):
            sink = jnp.where(head_of_lane == hh, sinks_ref[g * GROUP + hh] * LOG2_E, sink)
        sinks.append(sink)
    ones_rows = jnp.ones((2 * SUBLANES, WINDOW + tt), BF16)

    def scores(g, n):
        gs = slice(g * HEAD_DIM, (g + 1) * HEAD_DIM)
        ts = slice(n * tt, (n + 1) * tt)
        kk = jnp.concatenate([kp_ref[n, :, gs], kc_ref[n, :, gs]], axis=0)
        q4 = jnp.concatenate([qt_ref[h * HEAD_DIM:(h + 1) * HEAD_DIM, ts]
                              for h in range(g * GROUP, (g + 1) * GROUP)], axis=1)
        return jnp.dot(kk, q4, preferred_element_type=F32)

    pairs = [(g, n) for g in range(N_KV_HEADS) for n in range(nb)]
    for idx, (g, n) in enumerate(pairs):
        s_ref[idx] = scores(g, n) + bias_ref[0, g]
    for idx, (g, n) in enumerate(pairs):
        gs = slice(g * HEAD_DIM, (g + 1) * HEAD_DIM)
        ts = slice(n * tt, (n + 1) * tt)
        s = s_ref[idx]
        mx = jnp.maximum(jnp.max(s, axis=0, keepdims=True), sinks[g])
        p = jnp.exp2(s - mx).astype(BF16)
        vt = jnp.concatenate([vtp_ref[n, gs, :], vtc_ref[n, gs, :]], axis=1)
        pv = jnp.dot(jnp.concatenate([vt, ones_rows], axis=0), p,
                     preferred_element_type=F32)
        den = pv[HEAD_DIM:HEAD_DIM + 1, :] + jnp.exp2(sinks[g] - mx)
        o = pv[:HEAD_DIM, :] / den
        for hh in range(GROUP):
            h = g * GROUP + hh
            ot_ref[h * HEAD_DIM:(h + 1) * HEAD_DIM, ts] = o[:, hh * tt:(hh + 1) * tt]
    y = _mm(ot_ref[...].T, wo_ref[...]) + bo_ref[...]
    y_ref[...] = (x + y).reshape(nb, tt, D_MODEL)


def _const_spec3(shape, layer=None):
    nd = len(shape)
    if layer is None:
        return pl.BlockSpec(shape, lambda b, j, s: (0,) * nd, pipeline_mode=pl.Buffered(1))
    return pl.BlockSpec((None,) + shape, lambda b, j, s: (layer,) + (0,) * nd,
                        pipeline_mode=pl.Buffered(1))


def _attn_sample_block(x, k_win, k_new, v_win, v_new, bias, sinks, norm, w_q, b_q, q_norm, w_o, b_o,
                       *, nb, layer):
    B, tt, _ = x.shape
    L = WINDOW + tt
    assert nb % SEQS_PER_STEP == 0 and tt % SUBLANES == 0
    kern = functools.partial(_attn_sample_kernel, nb=nb, tt=tt)
    seq_map = lambda b, j, s: (b, 0, 0)
    grid_spec = pltpu.PrefetchScalarGridSpec(
        num_scalar_prefetch=1,
        grid=(B // nb, 1),
        in_specs=[
            pl.BlockSpec((nb, tt, D_MODEL), seq_map),
            pl.BlockSpec((nb, WINDOW, HKV), seq_map),
            pl.BlockSpec((nb, tt, HKV), seq_map),
            pl.BlockSpec((nb, WINDOW, HKV), seq_map),
            pl.BlockSpec((nb, tt, HKV), seq_map),
            _const_spec3((N_HEADS, tt, L)),
            _const_spec3((1, D_MODEL)),
            _const_spec3((D_MODEL, HQ), layer),
            _const_spec3((1, HQ)),
            _const_spec3((1, HEAD_DIM)),
            _const_spec3((HQ, D_MODEL), layer),
            _const_spec3((1, D_MODEL)),
        ],
        out_specs=pl.BlockSpec((nb, tt, D_MODEL), seq_map),
        scratch_shapes=[
            pltpu.VMEM((nb * tt, HQ), F32),
            pltpu.VMEM((nb * tt, HQ), F32),
        ],
    )
    return pl.pallas_call(
        kern,
        grid_spec=grid_spec,
        out_shape=jax.ShapeDtypeStruct((B, tt, D_MODEL), F32),
        compiler_params=_params(),
        name="swa_sample",
    )(sinks, x, k_win, k_new, v_win, v_new, bias, norm.reshape(1, -1), w_q, b_q.reshape(1, -1),
      q_norm.reshape(1, -1), w_o, b_o.reshape(1, -1))


def _attn_prompt_block(x, kb, vt, bias_t, sinks, norm, w_qt, b_q, q_norm, w_o, b_o, *, nb, layer):
    B, T, _ = x.shape
    tt = WINDOW
    L = WINDOW + tt
    kern = functools.partial(_attn_prompt_kernel, nb=nb, tt=tt)
    cur_map = lambda b, j, s: (b, j, 0)
    prev_map = lambda b, j, s: (b, jnp.maximum(j - 1, 0), 0)
    grid_spec = pltpu.PrefetchScalarGridSpec(
        num_scalar_prefetch=1,
        grid=(B // nb, T // tt),
        in_specs=[
            pl.BlockSpec((nb, tt, D_MODEL), cur_map),
            pl.BlockSpec((nb, tt, HKV), prev_map),
            pl.BlockSpec((nb, tt, HKV), cur_map),
            pl.BlockSpec((nb, HKV, tt), lambda b, j, s: (b, 0, jnp.maximum(j - 1, 0))),
            pl.BlockSpec((nb, HKV, tt), lambda b, j, s: (b, 0, j)),
            pl.BlockSpec((1, N_KV_HEADS, L, GROUP * tt),
                         lambda b, j, s: (jnp.where(j == 0, 1, 0), 0, 0, 0)),
            _const_spec3((1, D_MODEL)),
            _const_spec3((HQ, D_MODEL), layer),
            _const_spec3((HQ, 1)),
            _const_spec3((HEAD_DIM, 1)),
            _const_spec3((HQ, D_MODEL), layer),
            _const_spec3((1, D_MODEL)),
        ],
        out_specs=pl.BlockSpec((nb, tt, D_MODEL), cur_map),
        scratch_shapes=[
            pltpu.VMEM((HQ, nb * tt), BF16),
            pltpu.VMEM((HQ, nb * tt), F32),
            pltpu.VMEM((N_KV_HEADS * nb, L, GROUP * tt), F32),
        ],
    )
    return pl.pallas_call(
        kern,
        grid_spec=grid_spec,
        out_shape=jax.ShapeDtypeStruct((B, T, D_MODEL), F32),
        compiler_params=_params(),
        name="swa_prompt",
    )(sinks, x, kb, kb, vt, vt, bias_t, norm.reshape(1, -1), w_qt, b_q.reshape(-1, 1),
      q_norm.reshape(-1, 1), w_o, b_o.reshape(1, -1))


def kernel(x_prompt, x_sample, state_lru_h, state_lru_conv, state_ffn_conv, cache_k_win, cache_v_win,
           a_norm, a_w_in, a_conv_w, a_conv_b, a_gate_r_w, a_gate_r_b, a_gate_i_w, a_gate_i_b,
           a_lambda, a_w_out, kv_norm, w_kv, b_kv, k_norm, b_norm, w_q, b_q, q_norm, sinks,
           w_o, b_o, rel_bias, f_norm, f_w_up, f_conv_w, f_conv_b, f_w_down):
    bf = lambda w: w.astype(BF16)
    a_w_in, a_gate_r_w, a_gate_i_w, a_w_out = bf(a_w_in), bf(a_gate_r_w), bf(a_gate_i_w), bf(a_w_out)
    w_kv, w_q, w_o, f_w_up, f_w_down = bf(w_kv), bf(w_q), bf(w_o), bf(f_w_up), bf(f_w_down)
    w_qt = jnp.swapaxes(w_q, 1, 2)
    B, S, _ = x_prompt.shape
    DB, DT, _ = x_sample.shape
    tile_p, tile_s = _tile(B, S, LRU_ROWS), _tile(DB, DT, LRU_ROWS)

    zero_h = jnp.zeros((N_A, B, D_RNN), F32)
    zero_c = jnp.zeros((N_A, B, CONV_W - 1, D_RNN), F32)
    zero_f = jnp.zeros((DEPTH, B, FFN_CONV_W - 1, D_FF), F32)
    win_k = cache_k_win.reshape(DB, WINDOW, HKV).astype(BF16)
    win_v = cache_v_win.reshape(DB, WINDOW, HKV).astype(BF16)
    bias_p = _rel_bias_table(rel_bias, WINDOW, keys_on_rows=True)
    bias_s = _rel_bias_table(rel_bias, DT, keys_on_rows=False)

    xp, xs = x_prompt, x_sample
    h_p, c_p, f_p, h_s, c_s, f_s = [], [], [], [], [], []
    for layer in range(DEPTH):
        if layer < N_A:
            i = layer
            lru_w = (a_norm[i], a_w_in, a_conv_w[i], a_conv_b[i], a_gate_r_w, a_gate_r_b[i],
                     a_gate_i_w, a_gate_i_b[i], a_lambda[i], a_w_out)
            xp, h, c = _lru_block(xp, zero_h[i], zero_c[i], *lru_w, **tile_p, layer=i)
            h_p.append(h.reshape(B, D_RNN))
            c_p.append(c)
            xs, h, c = _lru_block(xs, state_lru_h[i], state_lru_conv[i], *lru_w, **tile_s, layer=i)
            h_s.append(h.reshape(DB, D_RNN))
            c_s.append(c)
        else:
            j = layer - N_A
            xp = _attn_prompt_block(xp, kb_p, vt_p, bias_p, sinks[j], b_norm[j], w_qt, b_q[j],
                                    q_norm[j], w_o, b_o[j], nb=B, layer=j)
            xs = _attn_sample_block(xs, win_k, k_s, win_v, v_s, bias_s, sinks[j], b_norm[j], w_q,
                                    b_q[j], q_norm[j], w_o, b_o[j], nb=min(DB, ATTN_SEQS), layer=j)
        ffn_w = (f_norm[layer], f_w_up, f_conv_w[layer], f_conv_b[layer], f_w_down)
        xp, f = _conv_ffn(xp, zero_f, *ffn_w, **_tile(B, S, FFN_ROWS), layer=layer)
        f_p.append(f)
        xs, f = _conv_ffn(xs, state_ffn_conv, *ffn_w, **_tile(DB, DT, FFN_ROWS), layer=layer)
        f_s.append(f)
        if layer == N_A - 1:
            k_p, v_p, kb_p, vt_p = _shared_kv(xp, kv_norm, w_kv, b_kv, k_norm, **tile_p,
                                              key_major_copies=True)
            k_s, v_s = _shared_kv(xs, kv_norm, w_kv, b_kv, k_norm, **tile_s,
                                  key_major_copies=False)
    heads = lambda kv: kv.reshape(kv.shape[0], -1, N_KV_HEADS, HEAD_DIM)
    k_s = jnp.concatenate([cache_k_win[:, DT:], heads(k_s)], axis=1)
    v_s = jnp.concatenate([cache_v_win[:, DT:], heads(v_s)], axis=1)
    return (xp, xs,
            jnp.stack(h_p), jnp.stack(c_p), jnp.stack(f_p), heads(k_p), heads(v_p),
            jnp.stack(h_s), jnp.stack(c_s), jnp.stack(f_s), k_s, v_s)
```

```python
import functools
import math

import numpy as np
import jax
import jax.numpy as jnp
from jax import lax
from jax.experimental import pallas as pl
from jax.experimental.pallas import tpu as pltpu

D_MODEL = 1024
DEPTH = 4
N_A = DEPTH // 2
D_RNN = D_MODEL
N_LRU_BLOCKS = 4
LRU_BW = D_RNN // N_LRU_BLOCKS
CONV_W = 4
C_GATE = 8.0
N_HEADS = 16
N_KV_HEADS = 4
HEAD_DIM = 64
GROUP = N_HEADS // N_KV_HEADS
HQ = N_HEADS * HEAD_DIM
HKV = N_KV_HEADS * HEAD_DIM
WINDOW = 128
N_BUCKETS = 32
MAX_DISTANCE = 128
D_FF = 3 * D_MODEL
FFN_CONV_W = 3
EPS = 1e-6
NEG_INF = -1e30
LOG2_E = math.log2(math.e)
LN_2 = math.log(2.0)

F32 = jnp.float32
BF16 = jnp.bfloat16

SUBLANES = 8
LANES = 128
HEADS_PER_TILE = LANES // HEAD_DIM
FF_CHUNK = 256
FFN_ROWS = 512
LRU_ROWS = 1024
ATTN_SEQS = 64
SEQS_PER_STEP = 4
V7X_VMEM_BYTES = 64 * 1024 * 1024
VMEM_LIMIT_BYTES = V7X_VMEM_BYTES - 8 * 1024 * 1024


def _rmsnorm(x, g):
    return x * lax.rsqrt(jnp.mean(x * x, axis=-1, keepdims=True) + EPS) * g


def _softplus(x):
    e = jnp.exp(-jnp.abs(x))
    u = 1.0 + e
    exact = u == 1.0
    log1p_e = jnp.where(exact, e, jnp.log(u) * e / jnp.where(exact, 1.0, u - 1.0))
    return jnp.maximum(x, 0.0) + log1p_e


def _mm(a, b):
    return jnp.dot(a.astype(BF16), b, preferred_element_type=F32)


def _causal_dwconv(x, last_group, w, b):
    nb, tt, c = x.shape
    taps = w.shape[0]
    groups = tt // SUBLANES
    x4 = x.reshape(nb, groups, SUBLANES, c)
    t = lax.broadcasted_iota(jnp.int32, (1, 1, SUBLANES, c), 2)
    tap = lambda k: w[k:k + 1, :].reshape(1, 1, 1, c)
    y = b.reshape(1, 1, 1, c) + tap(taps - 1) * x4
    for s in range(1, taps):
        cur = pltpu.roll(x4.reshape(nb * groups, SUBLANES, c), s, axis=1)
        cur = cur.reshape(nb, groups, SUBLANES, c)
        before = pltpu.roll(last_group, s, axis=1)[:, None]
        if groups > 1:
            before = jnp.concatenate([before, cur[:, :groups - 1]], axis=1)
        y = y + tap(taps - 1 - s) * jnp.where(t >= s, cur, before)
    return y.reshape(nb, tt, c), x4[:, groups - 1]


def _tile(batch, seq_len, rows):
    tt = min(seq_len, max(SUBLANES, rows // batch))
    return dict(nb=min(batch, rows // tt), tt=tt)


def _const_spec(shape, layer=None):
    nd = len(shape)
    if layer is None:
        return pl.BlockSpec(shape, lambda b, j: (0,) * nd, pipeline_mode=pl.Buffered(1))
    return pl.BlockSpec((None,) + shape, lambda b, j: (layer,) + (0,) * nd,
                        pipeline_mode=pl.Buffered(1))


def _params():
    return pltpu.CompilerParams(dimension_semantics=("arbitrary", "arbitrary"),
                                vmem_limit_bytes=VMEM_LIMIT_BYTES)


def _lru_kernel(x_ref, h0_ref, cbuf_ref, norm_ref, win_ref, cw_ref, cb_ref, grw_ref, grb_ref,
                giw_ref, gib_ref, lam_ref, wout_ref, y_ref, hlast_ref, cnew_ref,
                last_ref, h_ref, mid_ref, *, nb, tt):
    m = nb * tt
    tail = SUBLANES - (CONV_W - 1)

    @pl.when(pl.program_id(1) == 0)
    def _():
        _init_conv_rows(last_ref, cbuf_ref, CONV_W)
        h_ref[...] = h0_ref[...]

    x = x_ref[...].reshape(m, D_MODEL)
    xn = _rmsnorm(x, norm_ref[...]).astype(BF16)
    decay = (-C_GATE * LOG2_E) * _softplus(-lam_ref[...])
    for n in range(N_LRU_BLOCKS):
        cs = slice(n * LRU_BW, (n + 1) * LRU_BW)
        xb = jnp.dot(xn, win_ref[:, D_RNN + n * LRU_BW:D_RNN + (n + 1) * LRU_BW],
                     preferred_element_type=F32)
        xc, last = _causal_dwconv(xb.reshape(nb, tt, LRU_BW), last_ref[:, :, cs],
                                  cw_ref[:, cs], cb_ref[:, cs])
        last_ref[:, :, cs] = last
        xc = xc.reshape(m, LRU_BW)
        xcb = xc.astype(BF16)
        r = jax.nn.sigmoid(jnp.dot(xcb, grw_ref[n], preferred_element_type=F32) + grb_ref[:, cs])
        i = jax.nn.sigmoid(jnp.dot(xcb, giw_ref[n], preferred_element_type=F32) + gib_ref[:, cs])
        log2_a = r * decay[:, cs]
        a = jnp.exp2(log2_a)
        y = jnp.tanh(log2_a * (-LN_2)) * (1.0 + a * a)
        b = jnp.exp2(0.5 * jnp.log2(y)) * (i * xc)
        hs, h_last = _linear_scan(a.reshape(nb, tt, LRU_BW), b.reshape(nb, tt, LRU_BW),
                                  h_ref[:, :, cs])
        h_ref[:, :, cs] = h_last
        gate = jnp.dot(xn, win_ref[:, cs], preferred_element_type=F32)
        mid_ref[:, cs] = (jax.nn.gelu(gate) * hs.reshape(m, LRU_BW)).astype(BF16)
    cnew_ref[...] = last_ref[:, tail:, :]
    hlast_ref[...] = h_ref[...]
    y = jnp.dot(mid_ref[...], wout_ref[...], preferred_element_type=F32)
    y_ref[...] = (x + y).reshape(nb, tt, D_MODEL)


def _linear_scan(a, b, h0):
    nb, tt, c = a.shape
    groups = tt // SUBLANES
    a = a.reshape(nb * groups, SUBLANES, c)
    b = b.reshape(nb * groups, SUBLANES, c)
    t = lax.broadcasted_iota(jnp.int32, a.shape, 1)
    d = 1
    while d < SUBLANES:
        keep = t >= d
        b = jnp.where(keep, a * pltpu.roll(b, d, axis=1) + b, b)
        a = jnp.where(keep, a * pltpu.roll(a, d, axis=1), a)
        d *= 2
    a = a.reshape(nb, groups, SUBLANES, c)
    b = b.reshape(nb, groups, SUBLANES, c)
    h = h0
    out = []
    for g in range(groups):
        hg = a[:, g] * h + b[:, g]
        out.append(hg)
        h = hg[:, SUBLANES - 1:SUBLANES, :]
    return jnp.concatenate(out, axis=1), h


def _init_conv_rows(last_ref, buf_ref, taps):
    nb, _, c = last_ref.shape
    tail = SUBLANES - (taps - 1)
    last_ref[:, :tail, :] = jnp.zeros((nb, tail, c), F32)
    last_ref[:, tail:, :] = buf_ref[...]


def _lru_block(x, h0, cbuf, norm, w_in, cw, cb, grw, grb, giw, gib, lam, w_out, *, nb, tt, layer):
    B, T, _ = x.shape
    kern = functools.partial(_lru_kernel, nb=nb, tt=tt)
    row = lambda v: v.reshape(1, -1)
    return pl.pallas_call(
        kern,
        grid=(B // nb, T // tt),
        in_specs=[
            pl.BlockSpec((nb, tt, D_MODEL), lambda b, j: (b, j, 0)),
            pl.BlockSpec((nb, 1, D_RNN), lambda b, j: (b, 0, 0)),
            pl.BlockSpec((nb, CONV_W - 1, D_RNN), lambda b, j: (b, 0, 0)),
            _const_spec((1, D_MODEL)),
            _const_spec((D_MODEL, 2 * D_RNN), layer),
            _const_spec((CONV_W, D_RNN)),
            _const_spec((1, D_RNN)),
            _const_spec((N_LRU_BLOCKS, LRU_BW, LRU_BW), layer),
            _const_spec((1, D_RNN)),
            _const_spec((N_LRU_BLOCKS, LRU_BW, LRU_BW), layer),
            _const_spec((1, D_RNN)),
            _const_spec((1, D_RNN)),
            _const_spec((D_RNN, D_MODEL), layer),
        ],
        out_specs=[
            pl.BlockSpec((nb, tt, D_MODEL), lambda b, j: (b, j, 0)),
            pl.BlockSpec((nb, 1, D_RNN), lambda b, j: (b, 0, 0)),
            pl.BlockSpec((nb, CONV_W - 1, D_RNN), lambda b, j: (b, 0, 0)),
        ],
        out_shape=[
            jax.ShapeDtypeStruct((B, T, D_MODEL), F32),
            jax.ShapeDtypeStruct((B, 1, D_RNN), F32),
            jax.ShapeDtypeStruct((B, CONV_W - 1, D_RNN), F32),
        ],
        scratch_shapes=[
            pltpu.VMEM((nb, SUBLANES, D_RNN), F32),
            pltpu.VMEM((nb, 1, D_RNN), F32),
            pltpu.VMEM((nb * tt, D_RNN), BF16),
        ],
        compiler_params=_params(),
        name="rglru_block",
    )(x, h0.reshape(B, 1, D_RNN), cbuf, row(norm), w_in, cw, row(cb), grw, row(grb), giw, row(gib), row(lam), w_out)


def _ffn_kernel(x_ref, buf_ref, norm_ref, wup_ref, cw_ref, cb_ref, wdown_ref, y_ref, bnew_ref,
                last_ref, ext_ref, mid_ref, *, nb, tt):
    m = nb * tt
    tail = SUBLANES - (FFN_CONV_W - 1)

    @pl.when(pl.program_id(1) == 0)
    def _():
        _init_conv_rows(last_ref, buf_ref, FFN_CONV_W)

    x = x_ref[...].reshape(m, D_MODEL)
    xn = _rmsnorm(x, norm_ref[...]).astype(BF16)
    for c in range(D_FF // FF_CHUNK):
        cs = slice(c * FF_CHUNK, (c + 1) * FF_CHUNK)
        g = jnp.dot(xn, wup_ref[:, cs], preferred_element_type=F32).reshape(nb, tt, FF_CHUNK)
        v = jnp.dot(xn, wup_ref[:, D_FF + c * FF_CHUNK:D_FF + (c + 1) * FF_CHUNK],
                    preferred_element_type=F32)
        if tt == SUBLANES:
            gc, last = _causal_dwconv(g, last_ref[:, :, cs], cw_ref[:, cs], cb_ref[:, cs])
            last_ref[:, :, cs] = last
        else:
            ext_ref[:, tail:SUBLANES, :] = last_ref[:, tail:, cs]
            ext_ref[:, SUBLANES:, :] = g
            gc = cb_ref[:, cs].reshape(1, 1, FF_CHUNK)
            for k in range(FFN_CONV_W):
                gc = gc + (cw_ref[k:k + 1, cs].reshape(1, 1, FF_CHUNK)
                           * ext_ref[:, tail + k:tail + k + tt, :])
            last_ref[:, tail:, cs] = ext_ref[:, tail + tt:SUBLANES + tt, :]
        mid_ref[:, cs] = (jax.nn.gelu(gc).reshape(m, FF_CHUNK) * v).astype(BF16)
    bnew_ref[...] = last_ref[:, tail:, :]
    y = jnp.dot(mid_ref[...], wdown_ref[...], preferred_element_type=F32)
    y_ref[...] = (x + y).reshape(nb, tt, D_MODEL)


def _conv_ffn(x, buf, norm, w_up, cw, cb, w_down, *, nb, tt, layer):
    B, T, _ = x.shape
    kern = functools.partial(_ffn_kernel, nb=nb, tt=tt)
    return pl.pallas_call(
        kern,
        grid=(B // nb, T // tt),
        in_specs=[
            pl.BlockSpec((nb, tt, D_MODEL), lambda b, j: (b, j, 0)),
            pl.BlockSpec((None, nb, FFN_CONV_W - 1, D_FF), lambda b, j: (layer, b, 0, 0)),
            _const_spec((1, D_MODEL)),
            _const_spec((D_MODEL, 2 * D_FF), layer),
            _const_spec((FFN_CONV_W, D_FF)),
            _const_spec((1, D_FF)),
            _const_spec((D_FF, D_MODEL), layer),
        ],
        out_specs=[
            pl.BlockSpec((nb, tt, D_MODEL), lambda b, j: (b, j, 0)),
            pl.BlockSpec((nb, FFN_CONV_W - 1, D_FF), lambda b, j: (b, 0, 0)),
        ],
        out_shape=[
            jax.ShapeDtypeStruct((B, T, D_MODEL), F32),
            jax.ShapeDtypeStruct((B, FFN_CONV_W - 1, D_FF), F32),
        ],
        scratch_shapes=[
            pltpu.VMEM((nb, SUBLANES, D_FF), F32),
            pltpu.VMEM((nb, SUBLANES + tt, FF_CHUNK), F32),
            pltpu.VMEM((nb * tt, D_FF), BF16),
        ],
        compiler_params=_params(),
        name="conv_ffn",
    )(x, buf, norm.reshape(1, -1), w_up, cw, cb.reshape(1, -1), w_down)


def _kv_kernel(x_ref, norm_ref, w_ref, b_ref, kn_ref, *rest, nb, tt, key_major_copies):
    m = nb * tt
    x = x_ref[...].reshape(m, D_MODEL)
    xn = _rmsnorm(x, norm_ref[...]).astype(BF16)
    kv = jnp.dot(xn, w_ref[...], preferred_element_type=F32) + b_ref[...]
    k = kv[:, :HKV]
    if key_major_copies:
        wvt_ref, bvt_ref, k_ref, v_ref, kb_ref, vt_ref = rest
        vt = lax.dot_general(wvt_ref[...], xn, (((1,), (1,)), ((), ())),
                             preferred_element_type=F32) + bvt_ref[...]
        for n in range(nb):
            vt_ref[n] = vt[:, n * tt:(n + 1) * tt].astype(BF16)
    else:
        k_ref, v_ref = rest
    rows = lax.broadcasted_iota(jnp.int32, (HKV, HKV), 0) // HEAD_DIM
    cols = lax.broadcasted_iota(jnp.int32, (HKV, HKV), 1) // HEAD_DIM
    avg = jnp.where(rows == cols, 1.0 / HEAD_DIM, 0.0).astype(BF16)
    sq = k * k
    hi = sq.astype(BF16)
    lo = (sq - hi.astype(F32)).astype(BF16)
    ms = (jnp.dot(hi, avg, preferred_element_type=F32)
          + jnp.dot(lo, avg, preferred_element_type=F32))
    kn = (k * lax.rsqrt(ms + EPS) * kn_ref[...]).reshape(nb, tt, HKV)
    v = kv[:, HKV:].reshape(nb, tt, HKV)
    if key_major_copies:
        k_ref[...] = kn[:, tt - WINDOW:, :]
        v_ref[...] = v[:, tt - WINDOW:, :]
        kb_ref[...] = kn.astype(BF16)
    else:
        k_ref[...] = kn
        v_ref[...] = v


def _shared_kv(x, norm, w_kv, b_kv, k_norm, *, nb, tt, key_major_copies):
    B, T, _ = x.shape
    kern = functools.partial(_kv_kernel, nb=nb, tt=tt, key_major_copies=key_major_copies)
    tok_spec = pl.BlockSpec((nb, tt, HKV), lambda b, j: (b, j, 0))
    in_specs = [
        pl.BlockSpec((nb, tt, D_MODEL), lambda b, j: (b, j, 0)),
        _const_spec((1, D_MODEL)),
        _const_spec((D_MODEL, 2 * HKV)),
        _const_spec((1, 2 * HKV)),
        _const_spec((1, HKV)),
    ]
    args = [x, norm.reshape(1, -1), w_kv, b_kv.reshape(1, -1),
            jnp.tile(k_norm, N_KV_HEADS).reshape(1, -1)]
    out_specs = [tok_spec, tok_spec]
    out_shape = [jax.ShapeDtypeStruct((B, T, HKV), F32), jax.ShapeDtypeStruct((B, T, HKV), F32)]
    if key_major_copies:
        assert tt % WINDOW == 0
        last_spec = pl.BlockSpec((nb, WINDOW, HKV), lambda b, j: (b, 0, 0))
        in_specs += [_const_spec((HKV, D_MODEL)), _const_spec((HKV, 1))]
        args += [w_kv[:, HKV:].T, b_kv[HKV:].reshape(-1, 1)]
        out_specs = [last_spec, last_spec, tok_spec,
                     pl.BlockSpec((nb, HKV, tt), lambda b, j: (b, 0, j))]
        out_shape = [jax.ShapeDtypeStruct((B, WINDOW, HKV), F32),
                     jax.ShapeDtypeStruct((B, WINDOW, HKV), F32),
                     jax.ShapeDtypeStruct((B, T, HKV), BF16),
                     jax.ShapeDtypeStruct((B, HKV, T), BF16)]
    return pl.pallas_call(
        kern,
        grid=(B // nb, T // tt),
        in_specs=in_specs,
        out_specs=out_specs,
        out_shape=out_shape,
        compiler_params=_params(),
        name="shared_kv",
    )(*args)


def _rel_buckets(tt):
    qi = np.arange(tt)[:, None]
    sj = np.arange(WINDOW + tt)[None, :]
    dist = qi + WINDOW - sj
    max_exact = N_BUCKETS // 2
    d = np.maximum(dist, 0)
    df = np.maximum(d, 1).astype(np.float32)
    large = max_exact + (np.log(df / max_exact) / math.log(MAX_DISTANCE / max_exact)
                         * (N_BUCKETS - max_exact)).astype(np.int32)
    large = np.minimum(large, N_BUCKETS - 1)
    bucket = np.where(d < max_exact, d, large)
    valid = (dist >= 0) & (dist < WINDOW)
    return np.where(valid, bucket, -1).astype(np.int32)


def _bias_of_buckets(rb_ref, bkt, h):
    acc = jnp.full(bkt.shape, NEG_INF, F32)
    for c in range(N_BUCKETS):
        acc = jnp.where(bkt == c, rb_ref[c, h], acc)
    return acc


def _bias_kernel(rb_ref, bkt_ref, out_ref):
    bkt = bkt_ref[...]

    def head(h, carry):
        out_ref[h] = _bias_of_buckets(rb_ref, bkt, h)
        return carry

    lax.fori_loop(0, N_HEADS, head, 0)


def _bias_t_kernel(rb_ref, bkt_ref, out_ref, *, tt):
    bkt = bkt_ref[...]
    before_tile = lax.broadcasted_iota(jnp.int32, bkt.shape, 0) < WINDOW

    def group(g, carry):
        for hh in range(GROUP):
            acc = _bias_of_buckets(rb_ref, bkt, g * GROUP + hh) * LOG2_E
            out_ref[0, g, :, hh * tt:(hh + 1) * tt] = acc
            out_ref[1, g, :, hh * tt:(hh + 1) * tt] = jnp.where(before_tile, NEG_INF, acc)
        return carry

    lax.fori_loop(0, N_KV_HEADS, group, 0)


def _rel_bias_table(rel_bias, tt, *, keys_on_rows):
    L = WINDOW + tt
    bkt = _rel_buckets(tt)
    if keys_on_rows:
        kern = functools.partial(_bias_t_kernel, tt=tt)
        bkt = bkt.T
        out_shape = jax.ShapeDtypeStruct((2, N_KV_HEADS, L, GROUP * tt), F32)
    else:
        kern = _bias_kernel
        out_shape = jax.ShapeDtypeStruct((N_HEADS, tt, L), F32)
    return pl.pallas_call(
        kern,
        in_specs=[
            pl.BlockSpec(memory_space=pltpu.SMEM),
            pl.BlockSpec(memory_space=pltpu.VMEM),
        ],
        out_specs=pl.BlockSpec(memory_space=pltpu.VMEM),
        out_shape=out_shape,
        name="rel_bias_table",
    )(rel_bias, jnp.asarray(bkt))


def _attn_sample_kernel(sinks_ref, x_ref, kp_ref, kc_ref, vp_ref, vc_ref, bias_ref, norm_ref, wq_ref,
                        bq_ref, qn_ref, wo_ref, bo_ref, y_ref, q_ref, o_ref, *, nb, tt):
    m = nb * tt
    L = WINDOW + tt
    sb = SEQS_PER_STEP
    x = x_ref[...].reshape(m, D_MODEL)
    q = _mm(_rmsnorm(x, norm_ref[...]), wq_ref[...]) + bq_ref[...]
    qgain = qn_ref[...] * (HEAD_DIM ** -0.5)
    for h in range(N_HEADS):
        hs = slice(h * HEAD_DIM, (h + 1) * HEAD_DIM)
        qh = q[:, hs]
        q_ref[:, hs] = qh * lax.rsqrt(jnp.mean(qh * qh, axis=-1, keepdims=True) + EPS) * qgain
    sink = jnp.concatenate([jnp.full((tt, 1), sinks_ref[h], F32) for h in range(N_HEADS)],
                           axis=0)[None]
    bias = bias_ref[...].reshape(1, N_HEADS * tt, L)

    half_of_lane = lax.broadcasted_iota(jnp.int32, (1, LANES), 1) // HEAD_DIM
    zeros_tile = jnp.zeros((sb * tt, LANES), F32)

    def in_kv_lanes(qblk, h):
        g = h // GROUP
        qt = qblk[:, (h // HEADS_PER_TILE) * LANES:(h // HEADS_PER_TILE + 1) * LANES]
        qt = jnp.where(half_of_lane == h % HEADS_PER_TILE, qt, 0.0)
        if h % HEADS_PER_TILE != g % HEADS_PER_TILE:
            qt = pltpu.roll(qt, HEAD_DIM, axis=1)
        tiles = [qt if t == g // HEADS_PER_TILE else zeros_tile for t in range(HKV // LANES)]
        return jnp.concatenate(tiles, axis=1).reshape(sb, tt, HKV)

    def seqs(i, carry):
        n0 = pl.multiple_of(i * sb, sb)
        rows = pl.ds(pl.multiple_of(i * (sb * tt), sb * tt), sb * tt)
        qblk = q_ref[rows, :]
        kall = jnp.concatenate([kp_ref[pl.ds(n0, sb)], kc_ref[pl.ds(n0, sb)]], axis=1).astype(BF16)
        vall = jnp.concatenate([vp_ref[pl.ds(n0, sb)], vc_ref[pl.ds(n0, sb)]], axis=1).astype(BF16)
        qp = jnp.concatenate([in_kv_lanes(qblk, h) for h in range(N_HEADS)], axis=1).astype(BF16)
        s = jnp.einsum('bqd,bkd->bqk', qp, kall, preferred_element_type=F32) + bias
        mx = jnp.maximum(jnp.max(s, axis=-1, keepdims=True), sink)
        p = jnp.exp(s - mx)
        den = jnp.sum(p, axis=-1, keepdims=True) + jnp.exp(sink - mx)
        o = jnp.einsum('bqk,bkd->bqd', p.astype(BF16), vall, preferred_element_type=F32) / den
        for h in range(N_HEADS):
            g = h // GROUP
            o_ref[rows, h * HEAD_DIM:(h + 1) * HEAD_DIM] = (
                o[:, h * tt:(h + 1) * tt, g * HEAD_DIM:(g + 1) * HEAD_DIM].reshape(sb * tt, HEAD_DIM))
        return carry

    lax.fori_loop(0, nb // sb, seqs, 0)
    y = _mm(o_ref[...], wo_ref[...]) + bo_ref[...]
    y_ref[...] = (x + y).reshape(nb, tt, D_MODEL)


def _attn_prompt_kernel(sinks_ref, x_ref, kp_ref, kc_ref, vtp_ref, vtc_ref, bias_ref, norm_ref,
                        wqt_ref, bq_ref, qn_ref, wo_ref, bo_ref, y_ref, qt_ref, ot_ref, s_ref,
                        *, nb, tt):
    m = nb * tt
    x = x_ref[...].reshape(m, D_MODEL)
    xn = _rmsnorm(x, norm_ref[...]).astype(BF16)
    qt = lax.dot_general(wqt_ref[...], xn, (((1,), (1,)), ((), ())),
                         preferred_element_type=F32) + bq_ref[...]
    qgain = qn_ref[...] * (HEAD_DIM ** -0.5 * LOG2_E)
    for h in range(N_HEADS):
        hs = slice(h * HEAD_DIM, (h + 1) * HEAD_DIM)
        qh = qt[hs, :]
        ms = jnp.mean(qh * qh, axis=0, keepdims=True)
        qt_ref[hs, :] = (qh * lax.rsqrt(ms + EPS) * qgain).astype(BF16)

    head_of_lane = lax.broadcasted_iota(jnp.int32, (1, GROUP * tt), 1) // tt
    sinks = []
    for g in range(N_KV_HEADS):
        sink = jnp.full((1, GROUP * tt), sinks_ref[g * GROUP] * LOG2_E, F32)
        for hh in range(1, GROUP):
            sink = jnp.where(head_of_lane == hh, sinks_ref[g * GROUP + hh] * LOG2_E, sink)
        sinks.append(sink)
    ones_rows = jnp.ones((2 * SUBLANES, WINDOW + tt), BF16)

    def scores(g, n):
        gs = slice(g * HEAD_DIM, (g + 1) * HEAD_DIM)
        ts = slice(n * tt, (n + 1) * tt)
        kk = jnp.concatenate([kp_ref[n, :, gs], kc_ref[n, :, gs]], axis=0)
        q4 = jnp.concatenate([qt_ref[h * HEAD_DIM:(h + 1) * HEAD_DIM, ts]
                              for h in range(g * GROUP, (g + 1) * GROUP)], axis=1)
        return jnp.dot(kk, q4, preferred_element_type=F32)

    pairs = [(g, n) for g in range(N_KV_HEADS) for n in range(nb)]
    for idx, (g, n) in enumerate(pairs):
        s_ref[idx] = scores(g, n) + bias_ref[0, g]
    for idx, (g, n) in enumerate(pairs):
        gs = slice(g * HEAD_DIM, (g + 1) * HEAD_DIM)
        ts = slice(n * tt, (n + 1) * tt)
        s = s_ref[idx]
        mx = jnp.maximum(jnp.max(s, axis=0, keepdims=True), sinks[g])
        p = jnp.exp2(s - mx).astype(BF16)
        vt = jnp.concatenate([vtp_ref[n, gs, :], vtc_ref[n, gs, :]], axis=1)
        pv = jnp.dot(jnp.concatenate([vt, ones_rows], axis=0), p,
                     preferred_element_type=F32)
        den = pv[HEAD_DIM:HEAD_DIM + 1, :] + jnp.exp2(sinks[g] - mx)
        o = pv[:HEAD_DIM, :] / den
        for hh in range(GROUP):
            h = g * GROUP + hh
            ot_ref[h * HEAD_DIM:(h + 1) * HEAD_DIM, ts] = o[:, hh * tt:(hh + 1) * tt]
    y = _mm(ot_ref[...].T, wo_ref[...]) + bo_ref[...]
    y_ref[...] = (x + y).reshape(nb, tt, D_MODEL)


def _const_spec3(shape, layer=None):
    nd = len(shape)
    if layer is None:
        return pl.BlockSpec(shape, lambda b, j, s: (0,) * nd, pipeline_mode=pl.Buffered(1))
    return pl.BlockSpec((None,) + shape, lambda b, j, s: (layer,) + (0,) * nd,
                        pipeline_mode=pl.Buffered(1))


def _attn_sample_block(x, k_win, k_new, v_win, v_new, bias, sinks, norm, w_q, b_q, q_norm, w_o, b_o,
                       *, nb, layer):
    B, tt, _ = x.shape
    L = WINDOW + tt
    assert nb % SEQS_PER_STEP == 0 and tt % SUBLANES == 0
    kern = functools.partial(_attn_sample_kernel, nb=nb, tt=tt)
    seq_map = lambda b, j, s: (b, 0, 0)
    grid_spec = pltpu.PrefetchScalarGridSpec(
        num_scalar_prefetch=1,
        grid=(B // nb, 1),
        in_specs=[
            pl.BlockSpec((nb, tt, D_MODEL), seq_map),
            pl.BlockSpec((nb, WINDOW, HKV), seq_map),
            pl.BlockSpec((nb, tt, HKV), seq_map),
            pl.BlockSpec((nb, WINDOW, HKV), seq_map),
            pl.BlockSpec((nb, tt, HKV), seq_map),
            _const_spec3((N_HEADS, tt, L)),
            _const_spec3((1, D_MODEL)),
            _const_spec3((D_MODEL, HQ), layer),
            _const_spec3((1, HQ)),
            _const_spec3((1, HEAD_DIM)),
            _const_spec3((HQ, D_MODEL), layer),
            _const_spec3((1, D_MODEL)),
        ],
        out_specs=pl.BlockSpec((nb, tt, D_MODEL), seq_map),
        scratch_shapes=[
            pltpu.VMEM((nb * tt, HQ), F32),
            pltpu.VMEM((nb * tt, HQ), F32),
        ],
    )
    return pl.pallas_call(
        kern,
        grid_spec=grid_spec,
        out_shape=jax.ShapeDtypeStruct((B, tt, D_MODEL), F32),
        compiler_params=_params(),
        name="swa_sample",
    )(sinks, x, k_win, k_new, v_win, v_new, bias, norm.reshape(1, -1), w_q, b_q.reshape(1, -1),
      q_norm.reshape(1, -1), w_o, b_o.reshape(1, -1))


def _attn_prompt_block(x, kb, vt, bias_t, sinks, norm, w_qt, b_q, q_norm, w_o, b_o, *, nb, layer):
    B, T, _ = x.shape
    tt = WINDOW
    L = WINDOW + tt
    kern = functools.partial(_attn_prompt_kernel, nb=nb, tt=tt)
    cur_map = lambda b, j, s: (b, j, 0)
    prev_map = lambda b, j, s: (b, jnp.maximum(j - 1, 0), 0)
    grid_spec = pltpu.PrefetchScalarGridSpec(
        num_scalar_prefetch=1,
        grid=(B // nb, T // tt),
        in_specs=[
            pl.BlockSpec((nb, tt, D_MODEL), cur_map),
            pl.BlockSpec((nb, tt, HKV), prev_map),
            pl.BlockSpec((nb, tt, HKV), cur_map),
            pl.BlockSpec((nb, HKV, tt), lambda b, j, s: (b, 0, jnp.maximum(j - 1, 0))),
            pl.BlockSpec((nb, HKV, tt), lambda b, j, s: (b, 0, j)),
            pl.BlockSpec((1, N_KV_HEADS, L, GROUP * tt),
                         lambda b, j, s: (jnp.where(j == 0, 1, 0), 0, 0, 0)),
            _const_spec3((1, D_MODEL)),
            _const_spec3((HQ, D_MODEL), layer),
            _const_spec3((HQ, 1)),
            _const_spec3((HEAD_DIM, 1)),
            _const_spec3((HQ, D_MODEL), layer),
            _const_spec3((1, D_MODEL)),
        ],
        out_specs=pl.BlockSpec((nb, tt, D_MODEL), cur_map),
        scratch_shapes=[
            pltpu.VMEM((HQ, nb * tt), BF16),
            pltpu.VMEM((HQ, nb * tt), F32),
            pltpu.VMEM((N_KV_HEADS * nb, L, GROUP * tt), F32),
        ],
    )
    return pl.pallas_call(
        kern,
        grid_spec=grid_spec,
        out_shape=jax.ShapeDtypeStruct((B, T, D_MODEL), F32),
        compiler_params=_params(),
        name="swa_prompt",
    )(sinks, x, kb, kb, vt, vt, bias_t, norm.reshape(1, -1), w_qt, b_q.reshape(-1, 1),
      q_norm.reshape(-1, 1), w_o, b_o.reshape(1, -1))


def kernel(x_prompt, x_sample, state_lru_h, state_lru_conv, state_ffn_conv, cache_k_win, cache_v_win,
           a_norm, a_w_in, a_conv_w, a_conv_b, a_gate_r_w, a_gate_r_b, a_gate_i_w, a_gate_i_b,
           a_lambda, a_w_out, kv_norm, w_kv, b_kv, k_norm, b_norm, w_q, b_q, q_norm, sinks,
           w_o, b_o, rel_bias, f_norm, f_w_up, f_conv_w, f_conv_b, f_w_down):
    bf = lambda w: w.astype(BF16)
    a_w_in, a_gate_r_w, a_gate_i_w, a_w_out = bf(a_w_in), bf(a_gate_r_w), bf(a_gate_i_w), bf(a_w_out)
    w_kv, w_q, w_o, f_w_up, f_w_down = bf(w_kv), bf(w_q), bf(w_o), bf(f_w_up), bf(f_w_down)
    w_qt = jnp.swapaxes(w_q, 1, 2)
    B, S, _ = x_prompt.shape
    DB, DT, _ = x_sample.shape
    tile_p, tile_s = _tile(B, S, LRU_ROWS), _tile(DB, DT, LRU_ROWS)

    zero_h = jnp.zeros((N_A, B, D_RNN), F32)
    zero_c = jnp.zeros((N_A, B, CONV_W - 1, D_RNN), F32)
    zero_f = jnp.zeros((DEPTH, B, FFN_CONV_W - 1, D_FF), F32)
    win_k = cache_k_win.reshape(DB, WINDOW, HKV)
    win_v = cache_v_win.reshape(DB, WINDOW, HKV)
    bias_p = _rel_bias_table(rel_bias, WINDOW, keys_on_rows=True)
    bias_s = _rel_bias_table(rel_bias, DT, keys_on_rows=False)

    xp, xs = x_prompt, x_sample
    h_p, c_p, f_p, h_s, c_s, f_s = [], [], [], [], [], []
    for layer in range(DEPTH):
        if layer < N_A:
            i = layer
            lru_w = (a_norm[i], a_w_in, a_conv_w[i], a_conv_b[i], a_gate_r_w, a_gate_r_b[i],
                     a_gate_i_w, a_gate_i_b[i], a_lambda[i], a_w_out)
            xp, h, c = _lru_block(xp, zero_h[i], zero_c[i], *lru_w, **tile_p, layer=i)
            h_p.append(h.reshape(B, D_RNN))
            c_p.append(c)
            xs, h, c = _lru_block(xs, state_lru_h[i], state_lru_conv[i], *lru_w, **tile_s, layer=i)
            h_s.append(h.reshape(DB, D_RNN))
            c_s.append(c)
        else:
            j = layer - N_A
            xp = _attn_prompt_block(xp, kb_p, vt_p, bias_p, sinks[j], b_norm[j], w_qt, b_q[j],
                                    q_norm[j], w_o, b_o[j], nb=B, layer=j)
            xs = _attn_sample_block(xs, win_k, k_s, win_v, v_s, bias_s, sinks[j], b_norm[j], w_q,
                                    b_q[j], q_norm[j], w_o, b_o[j], nb=min(DB, ATTN_SEQS), layer=j)
        ffn_w = (f_norm[layer], f_w_up, f_conv_w[layer], f_conv_b[layer], f_w_down)
        xp, f = _conv_ffn(xp, zero_f, *ffn_w, **_tile(B, S, FFN_ROWS), layer=layer)
        f_p.append(f)
        xs, f = _conv_ffn(xs, state_ffn_conv, *ffn_w, **_tile(DB, DT, FFN_ROWS), layer=layer)
        f_s.append(f)
        if layer == N_A - 1:
            k_p, v_p, kb_p, vt_p = _shared_kv(xp, kv_norm, w_kv, b_kv, k_norm, **tile_p,
                                              key_major_copies=True)
            k_s, v_s = _shared_kv(xs, kv_norm, w_kv, b_kv, k_norm, **tile_s,
                                  key_major_copies=False)
    k_s = jnp.concatenate([win_k, k_s], axis=1)[:, -WINDOW:]
    v_s = jnp.concatenate([win_v, v_s], axis=1)[:, -WINDOW:]
    heads = lambda kv: kv.reshape(kv.shape[0], WINDOW, N_KV_HEADS, HEAD_DIM)
    return (xp, xs,
            jnp.stack(h_p), jnp.stack(c_p), jnp.stack(f_p), heads(k_p), heads(v_p),
            jnp.stack(h_s), jnp.stack(c_s), jnp.stack(f_s), heads(k_s), heads(v_s))
```

```python
import functools
import math

import numpy as np
import jax
import jax.numpy as jnp
from jax import lax
from jax.experimental import pallas as pl
from jax.experimental.pallas import tpu as pltpu

D_MODEL = 1024
DEPTH = 4
N_A = DEPTH // 2
D_RNN = D_MODEL
N_LRU_BLOCKS = 4
LRU_BW = D_RNN // N_LRU_BLOCKS
CONV_W = 4
C_GATE = 8.0
N_HEADS = 16
N_KV_HEADS = 4
HEAD_DIM = 64
GROUP = N_HEADS // N_KV_HEADS
HQ = N_HEADS * HEAD_DIM
HKV = N_KV_HEADS * HEAD_DIM
WINDOW = 128
N_BUCKETS = 32
MAX_DISTANCE = 128
D_FF = 3 * D_MODEL
FFN_CONV_W = 3
EPS = 1e-6
NEG_INF = -1e30
LOG2_E = math.log2(math.e)
LN_2 = math.log(2.0)

F32 = jnp.float32
BF16 = jnp.bfloat16

SUBLANES = 8
LANES = 128
HEADS_PER_TILE = LANES // HEAD_DIM
FF_CHUNK = 256
FFN_ROWS = 512
LRU_ROWS = 1024
X_RING = 3
ATTN_SEQS = 64
SEQS_PER_STEP = 4
V7X_VMEM_BYTES = 64 * 1024 * 1024
VMEM_LIMIT_BYTES = V7X_VMEM_BYTES - 8 * 1024 * 1024


def _rmsnorm(x, g):
    return x * lax.rsqrt(jnp.mean(x * x, axis=-1, keepdims=True) + EPS) * g


def _softplus(x):
    e = jnp.exp(-jnp.abs(x))
    u = 1.0 + e
    exact = u == 1.0
    log1p_e = jnp.where(exact, e, jnp.log(u) * e / jnp.where(exact, 1.0, u - 1.0))
    return jnp.maximum(x, 0.0) + log1p_e


def _mm(a, b):
    return jnp.dot(a.astype(BF16), b, preferred_element_type=F32)


def _causal_dwconv(x, last_group, w, b):
    nb, tt, c = x.shape
    taps = w.shape[0]
    groups = tt // SUBLANES
    x4 = x.reshape(nb, groups, SUBLANES, c)
    t = lax.broadcasted_iota(jnp.int32, (1, 1, SUBLANES, c), 2)
    tap = lambda k: w[k:k + 1, :].reshape(1, 1, 1, c)
    y = b.reshape(1, 1, 1, c) + tap(taps - 1) * x4
    for s in range(1, taps):
        cur = pltpu.roll(x4.reshape(nb * groups, SUBLANES, c), s, axis=1)
        cur = cur.reshape(nb, groups, SUBLANES, c)
        before = pltpu.roll(last_group, s, axis=1)[:, None]
        if groups > 1:
            before = jnp.concatenate([before, cur[:, :groups - 1]], axis=1)
        y = y + tap(taps - 1 - s) * jnp.where(t >= s, cur, before)
    return y.reshape(nb, tt, c), x4[:, groups - 1]


def _tile(batch, seq_len, rows):
    tt = min(seq_len, max(SUBLANES, rows // batch))
    return dict(nb=min(batch, rows // tt), tt=tt)


def _const_spec(shape, layer=None):
    nd = len(shape)
    if layer is None:
        return pl.BlockSpec(shape, lambda b, j: (0,) * nd, pipeline_mode=pl.Buffered(1))
    return pl.BlockSpec((None,) + shape, lambda b, j: (layer,) + (0,) * nd,
                        pipeline_mode=pl.Buffered(1))


def _params():
    return pltpu.CompilerParams(dimension_semantics=("arbitrary", "arbitrary"),
                                vmem_limit_bytes=VMEM_LIMIT_BYTES)


def _lru_kernel(x_ref, h0_ref, cbuf_ref, norm_ref, win_ref, cw_ref, cb_ref, grw_ref, grb_ref,
                giw_ref, gib_ref, lam_ref, wout_ref, y_ref, hlast_ref, cnew_ref,
                last_ref, h_ref, mid_ref, *, nb, tt):
    m = nb * tt
    tail = SUBLANES - (CONV_W - 1)

    @pl.when(pl.program_id(1) == 0)
    def _():
        _init_conv_rows(last_ref, cbuf_ref, CONV_W)
        h_ref[...] = h0_ref[...]

    x = x_ref[...].reshape(m, D_MODEL)
    xn = _rmsnorm(x, norm_ref[...]).astype(BF16)
    decay = (-C_GATE * LOG2_E) * _softplus(-lam_ref[...])
    for n in range(N_LRU_BLOCKS):
        cs = slice(n * LRU_BW, (n + 1) * LRU_BW)
        xb = jnp.dot(xn, win_ref[:, D_RNN + n * LRU_BW:D_RNN + (n + 1) * LRU_BW],
                     preferred_element_type=F32)
        xc, last = _causal_dwconv(xb.reshape(nb, tt, LRU_BW), last_ref[:, :, cs],
                                  cw_ref[:, cs], cb_ref[:, cs])
        last_ref[:, :, cs] = last
        xc = xc.reshape(m, LRU_BW)
        xcb = xc.astype(BF16)
        r = jax.nn.sigmoid(jnp.dot(xcb, grw_ref[n], preferred_element_type=F32) + grb_ref[:, cs])
        i = jax.nn.sigmoid(jnp.dot(xcb, giw_ref[n], preferred_element_type=F32) + gib_ref[:, cs])
        log2_a = r * decay[:, cs]
        a = jnp.exp2(log2_a)
        y = jnp.tanh(log2_a * (-LN_2)) * (1.0 + a * a)
        b = jnp.exp2(0.5 * jnp.log2(y)) * (i * xc)
        hs, h_last = _linear_scan(a.reshape(nb, tt, LRU_BW), b.reshape(nb, tt, LRU_BW),
                                  h_ref[:, :, cs])
        h_ref[:, :, cs] = h_last
        gate = jnp.dot(xn, win_ref[:, cs], preferred_element_type=F32)
        mid_ref[:, cs] = (jax.nn.gelu(gate) * hs.reshape(m, LRU_BW)).astype(BF16)
    cnew_ref[...] = last_ref[:, tail:, :]
    hlast_ref[...] = h_ref[...]
    y = jnp.dot(mid_ref[...], wout_ref[...], preferred_element_type=F32)
    y_ref[...] = (x + y).reshape(nb, tt, D_MODEL)


def _linear_scan(a, b, h0):
    nb, tt, c = a.shape
    groups = tt // SUBLANES
    a = a.reshape(nb * groups, SUBLANES, c)
    b = b.reshape(nb * groups, SUBLANES, c)
    t = lax.broadcasted_iota(jnp.int32, a.shape, 1)
    d = 1
    while d < SUBLANES:
        keep = t >= d
        b = jnp.where(keep, a * pltpu.roll(b, d, axis=1) + b, b)
        a = jnp.where(keep, a * pltpu.roll(a, d, axis=1), a)
        d *= 2
    a = a.reshape(nb, groups, SUBLANES, c)
    b = b.reshape(nb, groups, SUBLANES, c)
    h = h0
    out = []
    for g in range(groups):
        hg = a[:, g] * h + b[:, g]
        out.append(hg)
        h = hg[:, SUBLANES - 1:SUBLANES, :]
    return jnp.concatenate(out, axis=1), h


def _init_conv_rows(last_ref, buf_ref, taps):
    nb, _, c = last_ref.shape
    tail = SUBLANES - (taps - 1)
    last_ref[:, :tail, :] = jnp.zeros((nb, tail, c), F32)
    last_ref[:, tail:, :] = buf_ref[...]


def _lru_block(x, h0, cbuf, norm, w_in, cw, cb, grw, grb, giw, gib, lam, w_out, *, nb, tt, layer):
    B, T, _ = x.shape
    kern = functools.partial(_lru_kernel, nb=nb, tt=tt)
    row = lambda v: v.reshape(1, -1)
    return pl.pallas_call(
        kern,
        grid=(B // nb, T // tt),
        in_specs=[
            pl.BlockSpec((nb, tt, D_MODEL), lambda b, j: (b, j, 0)),
            pl.BlockSpec((nb, 1, D_RNN), lambda b, j: (b, 0, 0)),
            pl.BlockSpec((nb, CONV_W - 1, D_RNN), lambda b, j: (b, 0, 0)),
            _const_spec((1, D_MODEL)),
            _const_spec((D_MODEL, 2 * D_RNN), layer),
            _const_spec((CONV_W, D_RNN)),
            _const_spec((1, D_RNN)),
            _const_spec((N_LRU_BLOCKS, LRU_BW, LRU_BW), layer),
            _const_spec((1, D_RNN)),
            _const_spec((N_LRU_BLOCKS, LRU_BW, LRU_BW), layer),
            _const_spec((1, D_RNN)),
            _const_spec((1, D_RNN)),
            _const_spec((D_RNN, D_MODEL), layer),
        ],
        out_specs=[
            pl.BlockSpec((nb, tt, D_MODEL), lambda b, j: (b, j, 0)),
            pl.BlockSpec((nb, 1, D_RNN), lambda b, j: (b, 0, 0)),
            pl.BlockSpec((nb, CONV_W - 1, D_RNN), lambda b, j: (b, 0, 0)),
        ],
        out_shape=[
            jax.ShapeDtypeStruct((B, T, D_MODEL), F32),
            jax.ShapeDtypeStruct((B, 1, D_RNN), F32),
            jax.ShapeDtypeStruct((B, CONV_W - 1, D_RNN), F32),
        ],
        scratch_shapes=[
            pltpu.VMEM((nb, SUBLANES, D_RNN), F32),
            pltpu.VMEM((nb, 1, D_RNN), F32),
            pltpu.VMEM((nb * tt, D_RNN), BF16),
        ],
        compiler_params=_params(),
        name="rglru_block",
    )(x, h0.reshape(B, 1, D_RNN), cbuf, row(norm), w_in, cw, row(cb), grw, row(grb), giw, row(gib), row(lam), w_out)


def _ffn_kernel(x_ref, buf_ref, norm_ref, wup_ref, cw_ref, cb_ref, wdown_ref, y_ref, bnew_ref,
                last_ref, ext_ref, mid_ref, *, nb, tt):
    m = nb * tt
    tail = SUBLANES - (FFN_CONV_W - 1)

    @pl.when(pl.program_id(1) == 0)
    def _():
        _init_conv_rows(last_ref, buf_ref, FFN_CONV_W)

    x = x_ref[...].reshape(m, D_MODEL)
    xn = _rmsnorm(x, norm_ref[...]).astype(BF16)
    for c in range(D_FF // FF_CHUNK):
        cs = slice(c * FF_CHUNK, (c + 1) * FF_CHUNK)
        g = jnp.dot(xn, wup_ref[:, cs], preferred_element_type=F32).reshape(nb, tt, FF_CHUNK)
        v = jnp.dot(xn, wup_ref[:, D_FF + c * FF_CHUNK:D_FF + (c + 1) * FF_CHUNK],
                    preferred_element_type=F32)
        if tt == SUBLANES:
            gc, last = _causal_dwconv(g, last_ref[:, :, cs], cw_ref[:, cs], cb_ref[:, cs])
            last_ref[:, :, cs] = last
        else:
            ext_ref[:, tail:SUBLANES, :] = last_ref[:, tail:, cs]
            ext_ref[:, SUBLANES:, :] = g
            gc = cb_ref[:, cs].reshape(1, 1, FF_CHUNK)
            for k in range(FFN_CONV_W):
                gc = gc + (cw_ref[k:k + 1, cs].reshape(1, 1, FF_CHUNK)
                           * ext_ref[:, tail + k:tail + k + tt, :])
            last_ref[:, tail:, cs] = ext_ref[:, tail + tt:SUBLANES + tt, :]
        mid_ref[:, cs] = (jax.nn.gelu(gc).reshape(m, FF_CHUNK) * v).astype(BF16)
    bnew_ref[...] = last_ref[:, tail:, :]
    y = jnp.dot(mid_ref[...], wdown_ref[...], preferred_element_type=F32)
    y_ref[...] = (x + y).reshape(nb, tt, D_MODEL)


def _conv_ffn(x, buf, norm, w_up, cw, cb, w_down, *, nb, tt, layer):
    B, T, _ = x.shape
    kern = functools.partial(_ffn_kernel, nb=nb, tt=tt)
    return pl.pallas_call(
        kern,
        grid=(B // nb, T // tt),
        in_specs=[
            pl.BlockSpec((nb, tt, D_MODEL), lambda b, j: (b, j, 0)),
            pl.BlockSpec((None, nb, FFN_CONV_W - 1, D_FF), lambda b, j: (layer, b, 0, 0)),
            _const_spec((1, D_MODEL)),
            _const_spec((D_MODEL, 2 * D_FF), layer),
            _const_spec((FFN_CONV_W, D_FF)),
            _const_spec((1, D_FF)),
            _const_spec((D_FF, D_MODEL), layer),
        ],
        out_specs=[
            pl.BlockSpec((nb, tt, D_MODEL), lambda b, j: (b, j, 0)),
            pl.BlockSpec((nb, FFN_CONV_W - 1, D_FF), lambda b, j: (b, 0, 0)),
        ],
        out_shape=[
            jax.ShapeDtypeStruct((B, T, D_MODEL), F32),
            jax.ShapeDtypeStruct((B, FFN_CONV_W - 1, D_FF), F32),
        ],
        scratch_shapes=[
            pltpu.VMEM((nb, SUBLANES, D_FF), F32),
            pltpu.VMEM((nb, SUBLANES + tt, FF_CHUNK), F32),
            pltpu.VMEM((nb * tt, D_FF), BF16),
        ],
        compiler_params=_params(),
        name="conv_ffn",
    )(x, buf, norm.reshape(1, -1), w_up, cw, cb.reshape(1, -1), w_down)


def _kv_kernel(x_ref, norm_ref, w_ref, b_ref, kn_ref, *rest, nb, tt, key_major_copies):
    m = nb * tt
    x = x_ref[...].reshape(m, D_MODEL)
    xn = _rmsnorm(x, norm_ref[...]).astype(BF16)
    kv = jnp.dot(xn, w_ref[...], preferred_element_type=F32) + b_ref[...]
    k = kv[:, :HKV]
    if key_major_copies:
        wvt_ref, bvt_ref, k_ref, v_ref, kb_ref, vt_ref = rest
        vt = lax.dot_general(wvt_ref[...], xn, (((1,), (1,)), ((), ())),
                             preferred_element_type=F32) + bvt_ref[...]
        for n in range(nb):
            vt_ref[n] = vt[:, n * tt:(n + 1) * tt].astype(BF16)
    else:
        k_ref, v_ref = rest
    rows = lax.broadcasted_iota(jnp.int32, (HKV, HKV), 0) // HEAD_DIM
    cols = lax.broadcasted_iota(jnp.int32, (HKV, HKV), 1) // HEAD_DIM
    avg = jnp.where(rows == cols, 1.0 / HEAD_DIM, 0.0).astype(BF16)
    sq = k * k
    hi = sq.astype(BF16)
    lo = (sq - hi.astype(F32)).astype(BF16)
    ms = (jnp.dot(hi, avg, preferred_element_type=F32)
          + jnp.dot(lo, avg, preferred_element_type=F32))
    kn = (k * lax.rsqrt(ms + EPS) * kn_ref[...]).reshape(nb, tt, HKV)
    v = kv[:, HKV:].reshape(nb, tt, HKV)
    if key_major_copies:
        k_ref[...] = kn[:, tt - WINDOW:, :]
        v_ref[...] = v[:, tt - WINDOW:, :]
        kb_ref[...] = kn.astype(BF16)
    else:
        k_ref[...] = kn
        v_ref[...] = v


def _shared_kv(x, norm, w_kv, b_kv, k_norm, *, nb, tt, key_major_copies):
    B, T, _ = x.shape
    kern = functools.partial(_kv_kernel, nb=nb, tt=tt, key_major_copies=key_major_copies)
    tok_spec = pl.BlockSpec((nb, tt, HKV), lambda b, j: (b, j, 0))
    in_specs = [
        pl.BlockSpec((nb, tt, D_MODEL), lambda b, j: (b, j, 0)),
        _const_spec((1, D_MODEL)),
        _const_spec((D_MODEL, 2 * HKV)),
        _const_spec((1, 2 * HKV)),
        _const_spec((1, HKV)),
    ]
    args = [x, norm.reshape(1, -1), w_kv, b_kv.reshape(1, -1),
            jnp.tile(k_norm, N_KV_HEADS).reshape(1, -1)]
    out_specs = [tok_spec, tok_spec]
    out_shape = [jax.ShapeDtypeStruct((B, T, HKV), F32), jax.ShapeDtypeStruct((B, T, HKV), F32)]
    if key_major_copies:
        assert tt % WINDOW == 0
        last_spec = pl.BlockSpec((nb, WINDOW, HKV), lambda b, j: (b, 0, 0))
        in_specs += [_const_spec((HKV, D_MODEL)), _const_spec((HKV, 1))]
        args += [w_kv[:, HKV:].T, b_kv[HKV:].reshape(-1, 1)]
        out_specs = [last_spec, last_spec, tok_spec,
                     pl.BlockSpec((nb, HKV, tt), lambda b, j: (b, 0, j))]
        out_shape = [jax.ShapeDtypeStruct((B, WINDOW, HKV), F32),
                     jax.ShapeDtypeStruct((B, WINDOW, HKV), F32),
                     jax.ShapeDtypeStruct((B, T, HKV), BF16),
                     jax.ShapeDtypeStruct((B, HKV, T), BF16)]
    return pl.pallas_call(
        kern,
        grid=(B // nb, T // tt),
        in_specs=in_specs,
        out_specs=out_specs,
        out_shape=out_shape,
        compiler_params=_params(),
        name="shared_kv",
    )(*args)


def _rel_buckets(tt):
    qi = np.arange(tt)[:, None]
    sj = np.arange(WINDOW + tt)[None, :]
    dist = qi + WINDOW - sj
    max_exact = N_BUCKETS // 2
    d = np.maximum(dist, 0)
    df = np.maximum(d, 1).astype(np.float32)
    large = max_exact + (np.log(df / max_exact) / math.log(MAX_DISTANCE / max_exact)
                         * (N_BUCKETS - max_exact)).astype(np.int32)
    large = np.minimum(large, N_BUCKETS - 1)
    bucket = np.where(d < max_exact, d, large)
    valid = (dist >= 0) & (dist < WINDOW)
    return np.where(valid, bucket, -1).astype(np.int32)


def _bias_of_buckets(rb_ref, bkt, h):
    acc = jnp.full(bkt.shape, NEG_INF, F32)
    for c in range(N_BUCKETS):
        acc = jnp.where(bkt == c, rb_ref[c, h], acc)
    return acc


def _bias_kernel(rb_ref, bkt_ref, out_ref):
    bkt = bkt_ref[...]

    def head(h, carry):
        out_ref[h] = _bias_of_buckets(rb_ref, bkt, h)
        return carry

    lax.fori_loop(0, N_HEADS, head, 0)


def _bias_t_kernel(rb_ref, bkt_ref, out_ref, *, tt):
    bkt = bkt_ref[...]
    before_tile = lax.broadcasted_iota(jnp.int32, bkt.shape, 0) < WINDOW

    def group(g, carry):
        for hh in range(GROUP):
            acc = _bias_of_buckets(rb_ref, bkt, g * GROUP + hh) * LOG2_E
            out_ref[0, g, :, hh * tt:(hh + 1) * tt] = acc
            out_ref[1, g, :, hh * tt:(hh + 1) * tt] = jnp.where(before_tile, NEG_INF, acc)
        return carry

    lax.fori_loop(0, N_KV_HEADS, group, 0)


def _rel_bias_table(rel_bias, tt, *, keys_on_rows):
    L = WINDOW + tt
    bkt = _rel_buckets(tt)
    if keys_on_rows:
        kern = functools.partial(_bias_t_kernel, tt=tt)
        bkt = bkt.T
        out_shape = jax.ShapeDtypeStruct((2, N_KV_HEADS, L, GROUP * tt), F32)
    else:
        kern = _bias_kernel
        out_shape = jax.ShapeDtypeStruct((N_HEADS, tt, L), F32)
    return pl.pallas_call(
        kern,
        in_specs=[
            pl.BlockSpec(memory_space=pltpu.SMEM),
            pl.BlockSpec(memory_space=pltpu.VMEM),
        ],
        out_specs=pl.BlockSpec(memory_space=pltpu.VMEM),
        out_shape=out_shape,
        name="rel_bias_table",
    )(rel_bias, jnp.asarray(bkt))


def _attn_sample_kernel(sinks_ref, x_ref, kp_ref, kc_ref, vp_ref, vc_ref, bias_ref, norm_ref, wq_ref,
                        bq_ref, qn_ref, wo_ref, bo_ref, y_ref, q_ref, o_ref, *, nb, tt):
    m = nb * tt
    L = WINDOW + tt
    sb = SEQS_PER_STEP
    x = x_ref[...].reshape(m, D_MODEL)
    q = _mm(_rmsnorm(x, norm_ref[...]), wq_ref[...]) + bq_ref[...]
    qgain = qn_ref[...] * (HEAD_DIM ** -0.5)
    for h in range(N_HEADS):
        hs = slice(h * HEAD_DIM, (h + 1) * HEAD_DIM)
        qh = q[:, hs]
        q_ref[:, hs] = qh * lax.rsqrt(jnp.mean(qh * qh, axis=-1, keepdims=True) + EPS) * qgain
    sink = jnp.concatenate([jnp.full((tt, 1), sinks_ref[h], F32) for h in range(N_HEADS)],
                           axis=0)[None]
    bias = bias_ref[...].reshape(1, N_HEADS * tt, L)

    half_of_lane = lax.broadcasted_iota(jnp.int32, (1, LANES), 1) // HEAD_DIM
    zeros_tile = jnp.zeros((sb * tt, LANES), F32)

    def in_kv_lanes(qblk, h):
        g = h // GROUP
        qt = qblk[:, (h // HEADS_PER_TILE) * LANES:(h // HEADS_PER_TILE + 1) * LANES]
        qt = jnp.where(half_of_lane == h % HEADS_PER_TILE, qt, 0.0)
        if h % HEADS_PER_TILE != g % HEADS_PER_TILE:
            qt = pltpu.roll(qt, HEAD_DIM, axis=1)
        tiles = [qt if t == g // HEADS_PER_TILE else zeros_tile for t in range(HKV // LANES)]
        return jnp.concatenate(tiles, axis=1).reshape(sb, tt, HKV)

    def seqs(i, carry):
        n0 = pl.multiple_of(i * sb, sb)
        rows = pl.ds(pl.multiple_of(i * (sb * tt), sb * tt), sb * tt)
        qblk = q_ref[rows, :]
        kall = jnp.concatenate([kp_ref[pl.ds(n0, sb)], kc_ref[pl.ds(n0, sb)]], axis=1).astype(BF16)
        vall = jnp.concatenate([vp_ref[pl.ds(n0, sb)], vc_ref[pl.ds(n0, sb)]], axis=1).astype(BF16)
        qp = jnp.concatenate([in_kv_lanes(qblk, h) for h in range(N_HEADS)], axis=1).astype(BF16)
        s = jnp.einsum('bqd,bkd->bqk', qp, kall, preferred_element_type=F32) + bias
        mx = jnp.maximum(jnp.max(s, axis=-1, keepdims=True), sink)
        p = jnp.exp(s - mx)
        den = jnp.sum(p, axis=-1, keepdims=True) + jnp.exp(sink - mx)
        o = jnp.einsum('bqk,bkd->bqd', p.astype(BF16), vall, preferred_element_type=F32) / den
        for h in range(N_HEADS):
            g = h // GROUP
            o_ref[rows, h * HEAD_DIM:(h + 1) * HEAD_DIM] = (
                o[:, h * tt:(h + 1) * tt, g * HEAD_DIM:(g + 1) * HEAD_DIM].reshape(sb * tt, HEAD_DIM))
        return carry

    lax.fori_loop(0, nb // sb, seqs, 0)
    y = _mm(o_ref[...], wo_ref[...]) + bo_ref[...]
    y_ref[...] = (x + y).reshape(nb, tt, D_MODEL)


def _attn_prompt_kernel(sinks_ref, x_ref, kp_ref, kc_ref, vtp_ref, vtc_ref, bias_ref, norm_ref,
                        wqt_ref, bq_ref, qn_ref, wo_ref, bo_ref, y_ref, qt_ref, ot_ref, s_ref,
                        xbuf_ref, xsem, *, nb, tt):
    m = nb * tt
    j = pl.program_id(1)
    steps = pl.num_programs(1)

    def x_tile(step):
        slot = step % X_RING
        return pltpu.make_async_copy(x_ref.at[:, pl.ds(step * tt, tt), :], xbuf_ref.at[slot],
                                     xsem.at[slot])

    @pl.when(j == 0)
    def _():
        for step in range(X_RING - 1):
            x_tile(step).start()

    @pl.when(j + (X_RING - 1) < steps)
    def _():
        x_tile(j + (X_RING - 1)).start()

    x_tile(j).wait()
    x = xbuf_ref[j % X_RING].reshape(m, D_MODEL)
    xn = _rmsnorm(x, norm_ref[...]).astype(BF16)
    qt = lax.dot_general(wqt_ref[...], xn, (((1,), (1,)), ((), ())),
                         preferred_element_type=F32) + bq_ref[...]
    qgain = qn_ref[...] * (HEAD_DIM ** -0.5 * LOG2_E)
    for h in range(N_HEADS):
        hs = slice(h * HEAD_DIM, (h + 1) * HEAD_DIM)
        qh = qt[hs, :]
        ms = jnp.mean(qh * qh, axis=0, keepdims=True)
        qt_ref[hs, :] = (qh * lax.rsqrt(ms + EPS) * qgain).astype(BF16)

    head_of_lane = lax.broadcasted_iota(jnp.int32, (1, GROUP * tt), 1) // tt
    sinks = []
    for g in range(N_KV_HEADS):
        sink = jnp.full((1, GROUP * tt), sinks_ref[g * GROUP] * LOG2_E, F32)
        for hh in range(1, GROUP):
            sink = jnp.where(head_of_lane == hh, sinks_ref[g * GROUP + hh] * LOG2_E, sink)
        sinks.append(sink)
    ones_rows = jnp.ones((2 * SUBLANES, WINDOW + tt), BF16)

    def scores(g, n):
        gs = slice(g * HEAD_DIM, (g + 1) * HEAD_DIM)
        ts = slice(n * tt, (n + 1) * tt)
        kk = jnp.concatenate([kp_ref[n, :, gs], kc_ref[n, :, gs]], axis=0)
        q4 = jnp.concatenate([qt_ref[h * HEAD_DIM:(h + 1) * HEAD_DIM, ts]
                              for h in range(g * GROUP, (g + 1) * GROUP)], axis=1)
        return jnp.dot(kk, q4, preferred_element_type=F32)

    pairs = [(g, n) for g in range(N_KV_HEADS) for n in range(nb)]
    for idx, (g, n) in enumerate(pairs):
        s_ref[idx] = scores(g, n) + bias_ref[0, g]
    for idx, (g, n) in enumerate(pairs):
        gs = slice(g * HEAD_DIM, (g + 1) * HEAD_DIM)
        ts = slice(n * tt, (n + 1) * tt)
        s = s_ref[idx]
        mx = jnp.maximum(jnp.max(s, axis=0, keepdims=True), sinks[g])
        p = jnp.exp2(s - mx).astype(BF16)
        vt = jnp.concatenate([vtp_ref[n, gs, :], vtc_ref[n, gs, :]], axis=1)
        pv = jnp.dot(jnp.concatenate([vt, ones_rows], axis=0), p,
                     preferred_element_type=F32)
        den = pv[HEAD_DIM:HEAD_DIM + 1, :] + jnp.exp2(sinks[g] - mx)
        o = pv[:HEAD_DIM, :] / den
        for hh in range(GROUP):
            h = g * GROUP + hh
            ot_ref[h * HEAD_DIM:(h + 1) * HEAD_DIM, ts] = o[:, hh * tt:(hh + 1) * tt]
    y = _mm(ot_ref[...].T, wo_ref[...]) + bo_ref[...]
    y_ref[...] = (x + y).reshape(nb, tt, D_MODEL)


def _const_spec3(shape, layer=None):
    nd = len(shape)
    if layer is None:
        return pl.BlockSpec(shape, lambda b, j, s: (0,) * nd, pipeline_mode=pl.Buffered(1))
    return pl.BlockSpec((None,) + shape, lambda b, j, s: (layer,) + (0,) * nd,
                        pipeline_mode=pl.Buffered(1))


def _attn_sample_block(x, k_win, k_new, v_win, v_new, bias, sinks, norm, w_q, b_q, q_norm, w_o, b_o,
                       *, nb, layer):
    B, tt, _ = x.shape
    L = WINDOW + tt
    assert nb % SEQS_PER_STEP == 0 and tt % SUBLANES == 0
    kern = functools.partial(_attn_sample_kernel, nb=nb, tt=tt)
    seq_map = lambda b, j, s: (b, 0, 0)
    grid_spec = pltpu.PrefetchScalarGridSpec(
        num_scalar_prefetch=1,
        grid=(B // nb, 1),
        in_specs=[
            pl.BlockSpec((nb, tt, D_MODEL), seq_map),
            pl.BlockSpec((nb, WINDOW, HKV), seq_map),
            pl.BlockSpec((nb, tt, HKV), seq_map),
            pl.BlockSpec((nb, WINDOW, HKV), seq_map),
            pl.BlockSpec((nb, tt, HKV), seq_map),
            _const_spec3((N_HEADS, tt, L)),
            _const_spec3((1, D_MODEL)),
            _const_spec3((D_MODEL, HQ), layer),
            _const_spec3((1, HQ)),
            _const_spec3((1, HEAD_DIM)),
            _const_spec3((HQ, D_MODEL), layer),
            _const_spec3((1, D_MODEL)),
        ],
        out_specs=pl.BlockSpec((nb, tt, D_MODEL), seq_map),
        scratch_shapes=[
            pltpu.VMEM((nb * tt, HQ), F32),
            pltpu.VMEM((nb * tt, HQ), F32),
        ],
    )
    return pl.pallas_call(
        kern,
        grid_spec=grid_spec,
        out_shape=jax.ShapeDtypeStruct((B, tt, D_MODEL), F32),
        compiler_params=_params(),
        name="swa_sample",
    )(sinks, x, k_win, k_new, v_win, v_new, bias, norm.reshape(1, -1), w_q, b_q.reshape(1, -1),
      q_norm.reshape(1, -1), w_o, b_o.reshape(1, -1))


def _attn_prompt_block(x, kb, vt, bias_t, sinks, norm, w_qt, b_q, q_norm, w_o, b_o, *, nb, layer):
    B, T, _ = x.shape
    tt = WINDOW
    L = WINDOW + tt
    assert nb == B and T // tt >= X_RING
    kern = functools.partial(_attn_prompt_kernel, nb=nb, tt=tt)
    cur_map = lambda b, j, s: (b, j, 0)
    prev_map = lambda b, j, s: (b, jnp.maximum(j - 1, 0), 0)
    grid_spec = pltpu.PrefetchScalarGridSpec(
        num_scalar_prefetch=1,
        grid=(B // nb, T // tt),
        in_specs=[
            pl.BlockSpec(memory_space=pl.ANY),
            pl.BlockSpec((nb, tt, HKV), prev_map),
            pl.BlockSpec((nb, tt, HKV), cur_map),
            pl.BlockSpec((nb, HKV, tt), lambda b, j, s: (b, 0, jnp.maximum(j - 1, 0))),
            pl.BlockSpec((nb, HKV, tt), lambda b, j, s: (b, 0, j)),
            pl.BlockSpec((1, N_KV_HEADS, L, GROUP * tt),
                         lambda b, j, s: (jnp.where(j == 0, 1, 0), 0, 0, 0)),
            _const_spec3((1, D_MODEL)),
            _const_spec3((HQ, D_MODEL), layer),
            _const_spec3((HQ, 1)),
            _const_spec3((HEAD_DIM, 1)),
            _const_spec3((HQ, D_MODEL), layer),
            _const_spec3((1, D_MODEL)),
        ],
        out_specs=pl.BlockSpec((nb, tt, D_MODEL), cur_map),
        scratch_shapes=[
            pltpu.VMEM((HQ, nb * tt), BF16),
            pltpu.VMEM((HQ, nb * tt), F32),
            pltpu.VMEM((N_KV_HEADS * nb, L, GROUP * tt), F32),
            pltpu.VMEM((X_RING, nb, tt, D_MODEL), F32),
            pltpu.SemaphoreType.DMA((X_RING,)),
        ],
    )
    return pl.pallas_call(
        kern,
        grid_spec=grid_spec,
        out_shape=jax.ShapeDtypeStruct((B, T, D_MODEL), F32),
        compiler_params=_params(),
        name="swa_prompt",
    )(sinks, x, kb, kb, vt, vt, bias_t, norm.reshape(1, -1), w_qt, b_q.reshape(-1, 1),
      q_norm.reshape(-1, 1), w_o, b_o.reshape(1, -1))


def kernel(x_prompt, x_sample, state_lru_h, state_lru_conv, state_ffn_conv, cache_k_win, cache_v_win,
           a_norm, a_w_in, a_conv_w, a_conv_b, a_gate_r_w, a_gate_r_b, a_gate_i_w, a_gate_i_b,
           a_lambda, a_w_out, kv_norm, w_kv, b_kv, k_norm, b_norm, w_q, b_q, q_norm, sinks,
           w_o, b_o, rel_bias, f_norm, f_w_up, f_conv_w, f_conv_b, f_w_down):
    bf = lambda w: w.astype(BF16)
    a_w_in, a_gate_r_w, a_gate_i_w, a_w_out = bf(a_w_in), bf(a_gate_r_w), bf(a_gate_i_w), bf(a_w_out)
    w_kv, w_q, w_o, f_w_up, f_w_down = bf(w_kv), bf(w_q), bf(w_o), bf(f_w_up), bf(f_w_down)
    w_qt = jnp.swapaxes(w_q, 1, 2)
    B, S, _ = x_prompt.shape
    DB, DT, _ = x_sample.shape
    tile_p, tile_s = _tile(B, S, LRU_ROWS), _tile(DB, DT, LRU_ROWS)

    zero_h = jnp.zeros((N_A, B, D_RNN), F32)
    zero_c = jnp.zeros((N_A, B, CONV_W - 1, D_RNN), F32)
    zero_f = jnp.zeros((DEPTH, B, FFN_CONV_W - 1, D_FF), F32)
    win_k = cache_k_win.reshape(DB, WINDOW, HKV)
    win_v = cache_v_win.reshape(DB, WINDOW, HKV)
    bias_p = _rel_bias_table(rel_bias, WINDOW, keys_on_rows=True)
    bias_s = _rel_bias_table(rel_bias, DT, keys_on_rows=False)

    xp, xs = x_prompt, x_sample
    h_p, c_p, f_p, h_s, c_s, f_s = [], [], [], [], [], []
    for layer in range(DEPTH):
        if layer < N_A:
            i = layer
            lru_w = (a_norm[i], a_w_in, a_conv_w[i], a_conv_b[i], a_gate_r_w, a_gate_r_b[i],
                     a_gate_i_w, a_gate_i_b[i], a_lambda[i], a_w_out)
            xp, h, c = _lru_block(xp, zero_h[i], zero_c[i], *lru_w, **tile_p, layer=i)
            h_p.append(h.reshape(B, D_RNN))
            c_p.append(c)
            xs, h, c = _lru_block(xs, state_lru_h[i], state_lru_conv[i], *lru_w, **tile_s, layer=i)
            h_s.append(h.reshape(DB, D_RNN))
            c_s.append(c)
        else:
            j = layer - N_A
            xp = _attn_prompt_block(xp, kb_p, vt_p, bias_p, sinks[j], b_norm[j], w_qt, b_q[j],
                                    q_norm[j], w_o, b_o[j], nb=B, layer=j)
            xs = _attn_sample_block(xs, win_k, k_s, win_v, v_s, bias_s, sinks[j], b_norm[j], w_q,
                                    b_q[j], q_norm[j], w_o, b_o[j], nb=min(DB, ATTN_SEQS), layer=j)
        ffn_w = (f_norm[layer], f_w_up, f_conv_w[layer], f_conv_b[layer], f_w_down)
        xp, f = _conv_ffn(xp, zero_f, *ffn_w, **_tile(B, S, FFN_ROWS), layer=layer)
        f_p.append(f)
        xs, f = _conv_ffn(xs, state_ffn_conv, *ffn_w, **_tile(DB, DT, FFN_ROWS), layer=layer)
        f_s.append(f)
        if layer == N_A - 1:
            k_p, v_p, kb_p, vt_p = _shared_kv(xp, kv_norm, w_kv, b_kv, k_norm, **tile_p,
                                              key_major_copies=True)
            k_s, v_s = _shared_kv(xs, kv_norm, w_kv, b_kv, k_norm, **tile_s,
                                  key_major_copies=False)
    k_s = jnp.concatenate([win_k, k_s], axis=1)[:, -WINDOW:]
    v_s = jnp.concatenate([win_v, v_s], axis=1)[:, -WINDOW:]
    heads = lambda kv: kv.reshape(kv.shape[0], WINDOW, N_KV_HEADS, HEAD_DIM)
    return (xp, xs,
            jnp.stack(h_p), jnp.stack(c_p), jnp.stack(f_p), heads(k_p), heads(v_p),
            jnp.stack(h_s), jnp.stack(c_s), jnp.stack(f_s), heads(k_s), heads(v_s))
```

```python
import functools
import math

import numpy as np
import jax
import jax.numpy as jnp
from jax import lax
from jax.experimental import pallas as pl
from jax.experimental.pallas import tpu as pltpu

D_MODEL = 1024
DEPTH = 4
N_A = DEPTH // 2
D_RNN = D_MODEL
N_LRU_BLOCKS = 4
LRU_BW = D_RNN // N_LRU_BLOCKS
CONV_W = 4
C_GATE = 8.0
N_HEADS = 16
N_KV_HEADS = 4
HEAD_DIM = 64
GROUP = N_HEADS // N_KV_HEADS
HQ = N_HEADS * HEAD_DIM
HKV = N_KV_HEADS * HEAD_DIM
WINDOW = 128
N_BUCKETS = 32
MAX_DISTANCE = 128
D_FF = 3 * D_MODEL
FFN_CONV_W = 3
EPS = 1e-6
NEG_INF = -1e30
LOG2_E = math.log2(math.e)
LN_2 = math.log(2.0)

F32 = jnp.float32
BF16 = jnp.bfloat16

SUBLANES = 8
LANES = 128
HEADS_PER_TILE = LANES // HEAD_DIM
FF_CHUNK = 256
FFN_ROWS = 512
LRU_ROWS = 1024
SCORES_AHEAD = 4
ATTN_SEQS = 64
SEQS_PER_STEP = 4
V7X_VMEM_BYTES = 64 * 1024 * 1024
VMEM_LIMIT_BYTES = V7X_VMEM_BYTES - 8 * 1024 * 1024


def _rmsnorm(x, g):
    return x * lax.rsqrt(jnp.mean(x * x, axis=-1, keepdims=True) + EPS) * g


def _softplus(x):
    e = jnp.exp(-jnp.abs(x))
    u = 1.0 + e
    exact = u == 1.0
    log1p_e = jnp.where(exact, e, jnp.log(u) * e / jnp.where(exact, 1.0, u - 1.0))
    return jnp.maximum(x, 0.0) + log1p_e


def _mm(a, b):
    return jnp.dot(a.astype(BF16), b, preferred_element_type=F32)


def _causal_dwconv(x, last_group, w, b):
    nb, tt, c = x.shape
    taps = w.shape[0]
    groups = tt // SUBLANES
    x4 = x.reshape(nb, groups, SUBLANES, c)
    t = lax.broadcasted_iota(jnp.int32, (1, 1, SUBLANES, c), 2)
    tap = lambda k: w[k:k + 1, :].reshape(1, 1, 1, c)
    y = b.reshape(1, 1, 1, c) + tap(taps - 1) * x4
    for s in range(1, taps):
        cur = pltpu.roll(x4.reshape(nb * groups, SUBLANES, c), s, axis=1)
        cur = cur.reshape(nb, groups, SUBLANES, c)
        before = pltpu.roll(last_group, s, axis=1)[:, None]
        if groups > 1:
            before = jnp.concatenate([before, cur[:, :groups - 1]], axis=1)
        y = y + tap(taps - 1 - s) * jnp.where(t >= s, cur, before)
    return y.reshape(nb, tt, c), x4[:, groups - 1]


def _tile(batch, seq_len, rows):
    tt = min(seq_len, max(SUBLANES, rows // batch))
    return dict(nb=min(batch, rows // tt), tt=tt)


def _const_spec(shape, layer=None):
    nd = len(shape)
    if layer is None:
        return pl.BlockSpec(shape, lambda b, j: (0,) * nd, pipeline_mode=pl.Buffered(1))
    return pl.BlockSpec((None,) + shape, lambda b, j: (layer,) + (0,) * nd,
                        pipeline_mode=pl.Buffered(1))


def _params():
    return pltpu.CompilerParams(dimension_semantics=("arbitrary", "arbitrary"),
                                vmem_limit_bytes=VMEM_LIMIT_BYTES)


def _lru_kernel(x_ref, h0_ref, cbuf_ref, norm_ref, win_ref, cw_ref, cb_ref, grw_ref, grb_ref,
                giw_ref, gib_ref, lam_ref, wout_ref, y_ref, hlast_ref, cnew_ref,
                last_ref, h_ref, mid_ref, *, nb, tt):
    m = nb * tt
    tail = SUBLANES - (CONV_W - 1)

    @pl.when(pl.program_id(1) == 0)
    def _():
        _init_conv_rows(last_ref, cbuf_ref, CONV_W)
        h_ref[...] = h0_ref[...]

    x = x_ref[...].reshape(m, D_MODEL)
    xn = _rmsnorm(x, norm_ref[...]).astype(BF16)
    decay = (-C_GATE * LOG2_E) * _softplus(-lam_ref[...])
    for n in range(N_LRU_BLOCKS):
        cs = slice(n * LRU_BW, (n + 1) * LRU_BW)
        xb = jnp.dot(xn, win_ref[:, D_RNN + n * LRU_BW:D_RNN + (n + 1) * LRU_BW],
                     preferred_element_type=F32)
        xc, last = _causal_dwconv(xb.reshape(nb, tt, LRU_BW), last_ref[:, :, cs],
                                  cw_ref[:, cs], cb_ref[:, cs])
        last_ref[:, :, cs] = last
        xc = xc.reshape(m, LRU_BW)
        xcb = xc.astype(BF16)
        r = jax.nn.sigmoid(jnp.dot(xcb, grw_ref[n], preferred_element_type=F32) + grb_ref[:, cs])
        i = jax.nn.sigmoid(jnp.dot(xcb, giw_ref[n], preferred_element_type=F32) + gib_ref[:, cs])
        log2_a = r * decay[:, cs]
        a = jnp.exp2(log2_a)
        y = jnp.tanh(log2_a * (-LN_2)) * (1.0 + a * a)
        b = jnp.exp2(0.5 * jnp.log2(y)) * (i * xc)
        hs, h_last = _linear_scan(a.reshape(nb, tt, LRU_BW), b.reshape(nb, tt, LRU_BW),
                                  h_ref[:, :, cs])
        h_ref[:, :, cs] = h_last
        gate = jnp.dot(xn, win_ref[:, cs], preferred_element_type=F32)
        mid_ref[:, cs] = (jax.nn.gelu(gate) * hs.reshape(m, LRU_BW)).astype(BF16)
    cnew_ref[...] = last_ref[:, tail:, :]
    hlast_ref[...] = h_ref[...]
    y = jnp.dot(mid_ref[...], wout_ref[...], preferred_element_type=F32)
    y_ref[...] = (x + y).reshape(nb, tt, D_MODEL)


def _linear_scan(a, b, h0):
    nb, tt, c = a.shape
    groups = tt // SUBLANES
    a = a.reshape(nb * groups, SUBLANES, c)
    b = b.reshape(nb * groups, SUBLANES, c)
    t = lax.broadcasted_iota(jnp.int32, a.shape, 1)
    d = 1
    while d < SUBLANES:
        keep = t >= d
        b = jnp.where(keep, a * pltpu.roll(b, d, axis=1) + b, b)
        a = jnp.where(keep, a * pltpu.roll(a, d, axis=1), a)
        d *= 2
    a = a.reshape(nb, groups, SUBLANES, c)
    b = b.reshape(nb, groups, SUBLANES, c)
    h = h0
    out = []
    for g in range(groups):
        hg = a[:, g] * h + b[:, g]
        out.append(hg)
        h = hg[:, SUBLANES - 1:SUBLANES, :]
    return jnp.concatenate(out, axis=1), h


def _init_conv_rows(last_ref, buf_ref, taps):
    nb, _, c = last_ref.shape
    tail = SUBLANES - (taps - 1)
    last_ref[:, :tail, :] = jnp.zeros((nb, tail, c), F32)
    last_ref[:, tail:, :] = buf_ref[...]


def _lru_block(x, h0, cbuf, norm, w_in, cw, cb, grw, grb, giw, gib, lam, w_out, *, nb, tt, layer):
    B, T, _ = x.shape
    kern = functools.partial(_lru_kernel, nb=nb, tt=tt)
    row = lambda v: v.reshape(1, -1)
    return pl.pallas_call(
        kern,
        grid=(B // nb, T // tt),
        in_specs=[
            pl.BlockSpec((nb, tt, D_MODEL), lambda b, j: (b, j, 0)),
            pl.BlockSpec((nb, 1, D_RNN), lambda b, j: (b, 0, 0)),
            pl.BlockSpec((nb, CONV_W - 1, D_RNN), lambda b, j: (b, 0, 0)),
            _const_spec((1, D_MODEL)),
            _const_spec((D_MODEL, 2 * D_RNN), layer),
            _const_spec((CONV_W, D_RNN)),
            _const_spec((1, D_RNN)),
            _const_spec((N_LRU_BLOCKS, LRU_BW, LRU_BW), layer),
            _const_spec((1, D_RNN)),
            _const_spec((N_LRU_BLOCKS, LRU_BW, LRU_BW), layer),
            _const_spec((1, D_RNN)),
            _const_spec((1, D_RNN)),
            _const_spec((D_RNN, D_MODEL), layer),
        ],
        out_specs=[
            pl.BlockSpec((nb, tt, D_MODEL), lambda b, j: (b, j, 0)),
            pl.BlockSpec((nb, 1, D_RNN), lambda b, j: (b, 0, 0)),
            pl.BlockSpec((nb, CONV_W - 1, D_RNN), lambda b, j: (b, 0, 0)),
        ],
        out_shape=[
            jax.ShapeDtypeStruct((B, T, D_MODEL), F32),
            jax.ShapeDtypeStruct((B, 1, D_RNN), F32),
            jax.ShapeDtypeStruct((B, CONV_W - 1, D_RNN), F32),
        ],
        scratch_shapes=[
            pltpu.VMEM((nb, SUBLANES, D_RNN), F32),
            pltpu.VMEM((nb, 1, D_RNN), F32),
            pltpu.VMEM((nb * tt, D_RNN), BF16),
        ],
        compiler_params=_params(),
        name="rglru_block",
    )(x, h0.reshape(B, 1, D_RNN), cbuf, row(norm), w_in, cw, row(cb), grw, row(grb), giw, row(gib), row(lam), w_out)


def _ffn_kernel(x_ref, buf_ref, norm_ref, wup_ref, cw_ref, cb_ref, wdown_ref, y_ref, bnew_ref,
                last_ref, ext_ref, mid_ref, *, nb, tt):
    m = nb * tt
    tail = SUBLANES - (FFN_CONV_W - 1)

    @pl.when(pl.program_id(1) == 0)
    def _():
        _init_conv_rows(last_ref, buf_ref, FFN_CONV_W)

    x = x_ref[...].reshape(m, D_MODEL)
    xn = _rmsnorm(x, norm_ref[...]).astype(BF16)
    for c in range(D_FF // FF_CHUNK):
        cs = slice(c * FF_CHUNK, (c + 1) * FF_CHUNK)
        g = jnp.dot(xn, wup_ref[:, cs], preferred_element_type=F32).reshape(nb, tt, FF_CHUNK)
        v = jnp.dot(xn, wup_ref[:, D_FF + c * FF_CHUNK:D_FF + (c + 1) * FF_CHUNK],
                    preferred_element_type=F32)
        if tt == SUBLANES:
            gc, last = _causal_dwconv(g, last_ref[:, :, cs], cw_ref[:, cs], cb_ref[:, cs])
            last_ref[:, :, cs] = last
        else:
            ext_ref[:, tail:SUBLANES, :] = last_ref[:, tail:, cs]
            ext_ref[:, SUBLANES:, :] = g
            gc = cb_ref[:, cs].reshape(1, 1, FF_CHUNK)
            for k in range(FFN_CONV_W):
                gc = gc + (cw_ref[k:k + 1, cs].reshape(1, 1, FF_CHUNK)
                           * ext_ref[:, tail + k:tail + k + tt, :])
            last_ref[:, tail:, cs] = ext_ref[:, tail + tt:SUBLANES + tt, :]
        mid_ref[:, cs] = (jax.nn.gelu(gc).reshape(m, FF_CHUNK) * v).astype(BF16)
    bnew_ref[...] = last_ref[:, tail:, :]
    y = jnp.dot(mid_ref[...], wdown_ref[...], preferred_element_type=F32)
    y_ref[...] = (x + y).reshape(nb, tt, D_MODEL)


def _conv_ffn(x, buf, norm, w_up, cw, cb, w_down, *, nb, tt, layer):
    B, T, _ = x.shape
    kern = functools.partial(_ffn_kernel, nb=nb, tt=tt)
    return pl.pallas_call(
        kern,
        grid=(B // nb, T // tt),
        in_specs=[
            pl.BlockSpec((nb, tt, D_MODEL), lambda b, j: (b, j, 0)),
            pl.BlockSpec((None, nb, FFN_CONV_W - 1, D_FF), lambda b, j: (layer, b, 0, 0)),
            _const_spec((1, D_MODEL)),
            _const_spec((D_MODEL, 2 * D_FF), layer),
            _const_spec((FFN_CONV_W, D_FF)),
            _const_spec((1, D_FF)),
            _const_spec((D_FF, D_MODEL), layer),
        ],
        out_specs=[
            pl.BlockSpec((nb, tt, D_MODEL), lambda b, j: (b, j, 0)),
            pl.BlockSpec((nb, FFN_CONV_W - 1, D_FF), lambda b, j: (b, 0, 0)),
        ],
        out_shape=[
            jax.ShapeDtypeStruct((B, T, D_MODEL), F32),
            jax.ShapeDtypeStruct((B, FFN_CONV_W - 1, D_FF), F32),
        ],
        scratch_shapes=[
            pltpu.VMEM((nb, SUBLANES, D_FF), F32),
            pltpu.VMEM((nb, SUBLANES + tt, FF_CHUNK), F32),
            pltpu.VMEM((nb * tt, D_FF), BF16),
        ],
        compiler_params=_params(),
        name="conv_ffn",
    )(x, buf, norm.reshape(1, -1), w_up, cw, cb.reshape(1, -1), w_down)


def _kv_kernel(x_ref, norm_ref, w_ref, b_ref, kn_ref, *rest, nb, tt, key_major_copies):
    m = nb * tt
    x = x_ref[...].reshape(m, D_MODEL)
    xn = _rmsnorm(x, norm_ref[...]).astype(BF16)
    kv = jnp.dot(xn, w_ref[...], preferred_element_type=F32) + b_ref[...]
    k = kv[:, :HKV]
    if key_major_copies:
        wvt_ref, bvt_ref, k_ref, v_ref, kb_ref, vt_ref = rest
        vt = lax.dot_general(wvt_ref[...], xn, (((1,), (1,)), ((), ())),
                             preferred_element_type=F32) + bvt_ref[...]
        for n in range(nb):
            vt_ref[n] = vt[:, n * tt:(n + 1) * tt].astype(BF16)
    else:
        k_ref, v_ref = rest
    rows = lax.broadcasted_iota(jnp.int32, (HKV, HKV), 0) // HEAD_DIM
    cols = lax.broadcasted_iota(jnp.int32, (HKV, HKV), 1) // HEAD_DIM
    avg = jnp.where(rows == cols, 1.0 / HEAD_DIM, 0.0).astype(BF16)
    sq = k * k
    hi = sq.astype(BF16)
    lo = (sq - hi.astype(F32)).astype(BF16)
    ms = (jnp.dot(hi, avg, preferred_element_type=F32)
          + jnp.dot(lo, avg, preferred_element_type=F32))
    kn = (k * lax.rsqrt(ms + EPS) * kn_ref[...]).reshape(nb, tt, HKV)
    v = kv[:, HKV:].reshape(nb, tt, HKV)
    if key_major_copies:
        k_ref[...] = kn[:, tt - WINDOW:, :]
        v_ref[...] = v[:, tt - WINDOW:, :]
        kb_ref[...] = kn.astype(BF16)
    else:
        k_ref[...] = kn
        v_ref[...] = v


def _shared_kv(x, norm, w_kv, b_kv, k_norm, *, nb, tt, key_major_copies):
    B, T, _ = x.shape
    kern = functools.partial(_kv_kernel, nb=nb, tt=tt, key_major_copies=key_major_copies)
    tok_spec = pl.BlockSpec((nb, tt, HKV), lambda b, j: (b, j, 0))
    in_specs = [
        pl.BlockSpec((nb, tt, D_MODEL), lambda b, j: (b, j, 0)),
        _const_spec((1, D_MODEL)),
        _const_spec((D_MODEL, 2 * HKV)),
        _const_spec((1, 2 * HKV)),
        _const_spec((1, HKV)),
    ]
    args = [x, norm.reshape(1, -1), w_kv, b_kv.reshape(1, -1),
            jnp.tile(k_norm, N_KV_HEADS).reshape(1, -1)]
    out_specs = [tok_spec, tok_spec]
    out_shape = [jax.ShapeDtypeStruct((B, T, HKV), F32), jax.ShapeDtypeStruct((B, T, HKV), F32)]
    if key_major_copies:
        assert tt % WINDOW == 0
        last_spec = pl.BlockSpec((nb, WINDOW, HKV), lambda b, j: (b, 0, 0))
        in_specs += [_const_spec((HKV, D_MODEL)), _const_spec((HKV, 1))]
        args += [w_kv[:, HKV:].T, b_kv[HKV:].reshape(-1, 1)]
        out_specs = [last_spec, last_spec, tok_spec,
                     pl.BlockSpec((nb, HKV, tt), lambda b, j: (b, 0, j))]
        out_shape = [jax.ShapeDtypeStruct((B, WINDOW, HKV), F32),
                     jax.ShapeDtypeStruct((B, WINDOW, HKV), F32),
                     jax.ShapeDtypeStruct((B, T, HKV), BF16),
                     jax.ShapeDtypeStruct((B, HKV, T), BF16)]
    return pl.pallas_call(
        kern,
        grid=(B // nb, T // tt),
        in_specs=in_specs,
        out_specs=out_specs,
        out_shape=out_shape,
        compiler_params=_params(),
        name="shared_kv",
    )(*args)


def _rel_buckets(tt):
    qi = np.arange(tt)[:, None]
    sj = np.arange(WINDOW + tt)[None, :]
    dist = qi + WINDOW - sj
    max_exact = N_BUCKETS // 2
    d = np.maximum(dist, 0)
    df = np.maximum(d, 1).astype(np.float32)
    large = max_exact + (np.log(df / max_exact) / math.log(MAX_DISTANCE / max_exact)
                         * (N_BUCKETS - max_exact)).astype(np.int32)
    large = np.minimum(large, N_BUCKETS - 1)
    bucket = np.where(d < max_exact, d, large)
    valid = (dist >= 0) & (dist < WINDOW)
    return np.where(valid, bucket, -1).astype(np.int32)


def _bias_of_buckets(rb_ref, bkt, h):
    acc = jnp.full(bkt.shape, NEG_INF, F32)
    for c in range(N_BUCKETS):
        acc = jnp.where(bkt == c, rb_ref[c, h], acc)
    return acc


def _bias_kernel(rb_ref, bkt_ref, out_ref):
    bkt = bkt_ref[...]

    def head(h, carry):
        out_ref[h] = _bias_of_buckets(rb_ref, bkt, h)
        return carry

    lax.fori_loop(0, N_HEADS, head, 0)


def _bias_t_kernel(rb_ref, bkt_ref, out_ref, *, tt):
    bkt = bkt_ref[...]
    before_tile = lax.broadcasted_iota(jnp.int32, bkt.shape, 0) < WINDOW

    def group(g, carry):
        for hh in range(GROUP):
            acc = _bias_of_buckets(rb_ref, bkt, g * GROUP + hh) * LOG2_E
            out_ref[0, g, :, hh * tt:(hh + 1) * tt] = acc
            out_ref[1, g, :, hh * tt:(hh + 1) * tt] = jnp.where(before_tile, NEG_INF, acc)
        return carry

    lax.fori_loop(0, N_KV_HEADS, group, 0)


def _rel_bias_table(rel_bias, tt, *, keys_on_rows):
    L = WINDOW + tt
    bkt = _rel_buckets(tt)
    if keys_on_rows:
        kern = functools.partial(_bias_t_kernel, tt=tt)
        bkt = bkt.T
        out_shape = jax.ShapeDtypeStruct((2, N_KV_HEADS, L, GROUP * tt), F32)
    else:
        kern = _bias_kernel
        out_shape = jax.ShapeDtypeStruct((N_HEADS, tt, L), F32)
    return pl.pallas_call(
        kern,
        in_specs=[
            pl.BlockSpec(memory_space=pltpu.SMEM),
            pl.BlockSpec(memory_space=pltpu.VMEM),
        ],
        out_specs=pl.BlockSpec(memory_space=pltpu.VMEM),
        out_shape=out_shape,
        name="rel_bias_table",
    )(rel_bias, jnp.asarray(bkt))


def _attn_sample_kernel(sinks_ref, x_ref, kp_ref, kc_ref, vp_ref, vc_ref, bias_ref, norm_ref, wq_ref,
                        bq_ref, qn_ref, wo_ref, bo_ref, y_ref, q_ref, o_ref, *, nb, tt):
    m = nb * tt
    L = WINDOW + tt
    sb = SEQS_PER_STEP
    x = x_ref[...].reshape(m, D_MODEL)
    q = _mm(_rmsnorm(x, norm_ref[...]), wq_ref[...]) + bq_ref[...]
    qgain = qn_ref[...] * (HEAD_DIM ** -0.5)
    for h in range(N_HEADS):
        hs = slice(h * HEAD_DIM, (h + 1) * HEAD_DIM)
        qh = q[:, hs]
        q_ref[:, hs] = qh * lax.rsqrt(jnp.mean(qh * qh, axis=-1, keepdims=True) + EPS) * qgain
    sink = jnp.concatenate([jnp.full((tt, 1), sinks_ref[h], F32) for h in range(N_HEADS)],
                           axis=0)[None]
    bias = bias_ref[...].reshape(1, N_HEADS * tt, L)

    half_of_lane = lax.broadcasted_iota(jnp.int32, (1, LANES), 1) // HEAD_DIM
    zeros_tile = jnp.zeros((sb * tt, LANES), F32)

    def in_kv_lanes(qblk, h):
        g = h // GROUP
        qt = qblk[:, (h // HEADS_PER_TILE) * LANES:(h // HEADS_PER_TILE + 1) * LANES]
        qt = jnp.where(half_of_lane == h % HEADS_PER_TILE, qt, 0.0)
        if h % HEADS_PER_TILE != g % HEADS_PER_TILE:
            qt = pltpu.roll(qt, HEAD_DIM, axis=1)
        tiles = [qt if t == g // HEADS_PER_TILE else zeros_tile for t in range(HKV // LANES)]
        return jnp.concatenate(tiles, axis=1).reshape(sb, tt, HKV)

    def seqs(i, carry):
        n0 = pl.multiple_of(i * sb, sb)
        rows = pl.ds(pl.multiple_of(i * (sb * tt), sb * tt), sb * tt)
        qblk = q_ref[rows, :]
        kall = jnp.concatenate([kp_ref[pl.ds(n0, sb)], kc_ref[pl.ds(n0, sb)]], axis=1).astype(BF16)
        vall = jnp.concatenate([vp_ref[pl.ds(n0, sb)], vc_ref[pl.ds(n0, sb)]], axis=1).astype(BF16)
        qp = jnp.concatenate([in_kv_lanes(qblk, h) for h in range(N_HEADS)], axis=1).astype(BF16)
        s = jnp.einsum('bqd,bkd->bqk', qp, kall, preferred_element_type=F32) + bias
        mx = jnp.maximum(jnp.max(s, axis=-1, keepdims=True), sink)
        p = jnp.exp(s - mx)
        den = jnp.sum(p, axis=-1, keepdims=True) + jnp.exp(sink - mx)
        o = jnp.einsum('bqk,bkd->bqd', p.astype(BF16), vall, preferred_element_type=F32) / den
        for h in range(N_HEADS):
            g = h // GROUP
            o_ref[rows, h * HEAD_DIM:(h + 1) * HEAD_DIM] = (
                o[:, h * tt:(h + 1) * tt, g * HEAD_DIM:(g + 1) * HEAD_DIM].reshape(sb * tt, HEAD_DIM))
        return carry

    lax.fori_loop(0, nb // sb, seqs, 0)
    y = _mm(o_ref[...], wo_ref[...]) + bo_ref[...]
    y_ref[...] = (x + y).reshape(nb, tt, D_MODEL)


def _attn_prompt_kernel(sinks_ref, x_ref, kp_ref, kc_ref, vtp_ref, vtc_ref, bias_ref, norm_ref,
                        wqt_ref, bq_ref, qn_ref, wo_ref, bo_ref, y_ref, qt_ref, ot_ref, s_ref,
                        *, nb, tt):
    m = nb * tt
    x = x_ref[...].reshape(m, D_MODEL)
    xn = _rmsnorm(x, norm_ref[...]).astype(BF16)
    qt = lax.dot_general(wqt_ref[...], xn, (((1,), (1,)), ((), ())),
                         preferred_element_type=F32) + bq_ref[...]
    qgain = qn_ref[...] * (HEAD_DIM ** -0.5 * LOG2_E)
    for h in range(N_HEADS):
        hs = slice(h * HEAD_DIM, (h + 1) * HEAD_DIM)
        qh = qt[hs, :]
        ms = jnp.mean(qh * qh, axis=0, keepdims=True)
        qt_ref[hs, :] = (qh * lax.rsqrt(ms + EPS) * qgain).astype(BF16)

    head_of_lane = lax.broadcasted_iota(jnp.int32, (1, GROUP * tt), 1) // tt
    sinks = []
    for g in range(N_KV_HEADS):
        sink = jnp.full((1, GROUP * tt), sinks_ref[g * GROUP] * LOG2_E, F32)
        for hh in range(1, GROUP):
            sink = jnp.where(head_of_lane == hh, sinks_ref[g * GROUP + hh] * LOG2_E, sink)
        sinks.append(sink)
    ones_rows = jnp.ones((2 * SUBLANES, WINDOW + tt), BF16)

    def scores(g, n):
        gs = slice(g * HEAD_DIM, (g + 1) * HEAD_DIM)
        ts = slice(n * tt, (n + 1) * tt)
        kk = jnp.concatenate([kp_ref[n, :, gs], kc_ref[n, :, gs]], axis=0)
        q4 = jnp.concatenate([qt_ref[h * HEAD_DIM:(h + 1) * HEAD_DIM, ts]
                              for h in range(g * GROUP, (g + 1) * GROUP)], axis=1)
        return jnp.dot(kk, q4, preferred_element_type=F32)

    pairs = [(g, n) for g in range(N_KV_HEADS) for n in range(nb)]

    def park(idx):
        g, n = pairs[idx]
        s_ref[idx] = scores(g, n) + bias_ref[0, g]

    for idx in range(min(SCORES_AHEAD, len(pairs))):
        park(idx)
    for idx, (g, n) in enumerate(pairs):
        if idx + SCORES_AHEAD < len(pairs):
            park(idx + SCORES_AHEAD)
        gs = slice(g * HEAD_DIM, (g + 1) * HEAD_DIM)
        ts = slice(n * tt, (n + 1) * tt)
        s = s_ref[idx]
        mx = jnp.maximum(jnp.max(s, axis=0, keepdims=True), sinks[g])
        p = jnp.exp2(s - mx).astype(BF16)
        vt = jnp.concatenate([vtp_ref[n, gs, :], vtc_ref[n, gs, :]], axis=1)
        pv = jnp.dot(jnp.concatenate([vt, ones_rows], axis=0), p,
                     preferred_element_type=F32)
        den = pv[HEAD_DIM:HEAD_DIM + 1, :] + jnp.exp2(sinks[g] - mx)
        o = pv[:HEAD_DIM, :] / den
        for hh in range(GROUP):
            h = g * GROUP + hh
            ot_ref[h * HEAD_DIM:(h + 1) * HEAD_DIM, ts] = o[:, hh * tt:(hh + 1) * tt]
    y = _mm(ot_ref[...].T, wo_ref[...]) + bo_ref[...]
    y_ref[...] = (x + y).reshape(nb, tt, D_MODEL)


def _const_spec3(shape, layer=None):
    nd = len(shape)
    if layer is None:
        return pl.BlockSpec(shape, lambda b, j, s: (0,) * nd, pipeline_mode=pl.Buffered(1))
    return pl.BlockSpec((None,) + shape, lambda b, j, s: (layer,) + (0,) * nd,
                        pipeline_mode=pl.Buffered(1))


def _attn_sample_block(x, k_win, k_new, v_win, v_new, bias, sinks, norm, w_q, b_q, q_norm, w_o, b_o,
                       *, nb, layer):
    B, tt, _ = x.shape
    L = WINDOW + tt
    assert nb % SEQS_PER_STEP == 0 and tt % SUBLANES == 0
    kern = functools.partial(_attn_sample_kernel, nb=nb, tt=tt)
    seq_map = lambda b, j, s: (b, 0, 0)
    grid_spec = pltpu.PrefetchScalarGridSpec(
        num_scalar_prefetch=1,
        grid=(B // nb, 1),
        in_specs=[
            pl.BlockSpec((nb, tt, D_MODEL), seq_map),
            pl.BlockSpec((nb, WINDOW, HKV), seq_map),
            pl.BlockSpec((nb, tt, HKV), seq_map),
            pl.BlockSpec((nb, WINDOW, HKV), seq_map),
            pl.BlockSpec((nb, tt, HKV), seq_map),
            _const_spec3((N_HEADS, tt, L)),
            _const_spec3((1, D_MODEL)),
            _const_spec3((D_MODEL, HQ), layer),
            _const_spec3((1, HQ)),
            _const_spec3((1, HEAD_DIM)),
            _const_spec3((HQ, D_MODEL), layer),
            _const_spec3((1, D_MODEL)),
        ],
        out_specs=pl.BlockSpec((nb, tt, D_MODEL), seq_map),
        scratch_shapes=[
            pltpu.VMEM((nb * tt, HQ), F32),
            pltpu.VMEM((nb * tt, HQ), F32),
        ],
    )
    return pl.pallas_call(
        kern,
        grid_spec=grid_spec,
        out_shape=jax.ShapeDtypeStruct((B, tt, D_MODEL), F32),
        compiler_params=_params(),
        name="swa_sample",
    )(sinks, x, k_win, k_new, v_win, v_new, bias, norm.reshape(1, -1), w_q, b_q.reshape(1, -1),
      q_norm.reshape(1, -1), w_o, b_o.reshape(1, -1))


def _attn_prompt_block(x, kb, vt, bias_t, sinks, norm, w_qt, b_q, q_norm, w_o, b_o, *, nb, layer):
    B, T, _ = x.shape
    tt = WINDOW
    L = WINDOW + tt
    kern = functools.partial(_attn_prompt_kernel, nb=nb, tt=tt)
    cur_map = lambda b, j, s: (b, j, 0)
    prev_map = lambda b, j, s: (b, jnp.maximum(j - 1, 0), 0)
    grid_spec = pltpu.PrefetchScalarGridSpec(
        num_scalar_prefetch=1,
        grid=(B // nb, T // tt),
        in_specs=[
            pl.BlockSpec((nb, tt, D_MODEL), cur_map),
            pl.BlockSpec((nb, tt, HKV), prev_map),
            pl.BlockSpec((nb, tt, HKV), cur_map),
            pl.BlockSpec((nb, HKV, tt), lambda b, j, s: (b, 0, jnp.maximum(j - 1, 0))),
            pl.BlockSpec((nb, HKV, tt), lambda b, j, s: (b, 0, j)),
            pl.BlockSpec((1, N_KV_HEADS, L, GROUP * tt),
                         lambda b, j, s: (jnp.where(j == 0, 1, 0), 0, 0, 0)),
            _const_spec3((1, D_MODEL)),
            _const_spec3((HQ, D_MODEL), layer),
            _const_spec3((HQ, 1)),
            _const_spec3((HEAD_DIM, 1)),
            _const_spec3((HQ, D_MODEL), layer),
            _const_spec3((1, D_MODEL)),
        ],
        out_specs=pl.BlockSpec((nb, tt, D_MODEL), cur_map),
        scratch_shapes=[
            pltpu.VMEM((HQ, nb * tt), BF16),
            pltpu.VMEM((HQ, nb * tt), F32),
            pltpu.VMEM((N_KV_HEADS * nb, L, GROUP * tt), F32),
        ],
    )
    return pl.pallas_call(
        kern,
        grid_spec=grid_spec,
        out_shape=jax.ShapeDtypeStruct((B, T, D_MODEL), F32),
        compiler_params=_params(),
        name="swa_prompt",
    )(sinks, x, kb, kb, vt, vt, bias_t, norm.reshape(1, -1), w_qt, b_q.reshape(-1, 1),
      q_norm.reshape(-1, 1), w_o, b_o.reshape(1, -1))


def kernel(x_prompt, x_sample, state_lru_h, state_lru_conv, state_ffn_conv, cache_k_win, cache_v_win,
           a_norm, a_w_in, a_conv_w, a_conv_b, a_gate_r_w, a_gate_r_b, a_gate_i_w, a_gate_i_b,
           a_lambda, a_w_out, kv_norm, w_kv, b_kv, k_norm, b_norm, w_q, b_q, q_norm, sinks,
           w_o, b_o, rel_bias, f_norm, f_w_up, f_conv_w, f_conv_b, f_w_down):
    bf = lambda w: w.astype(BF16)
    a_w_in, a_gate_r_w, a_gate_i_w, a_w_out = bf(a_w_in), bf(a_gate_r_w), bf(a_gate_i_w), bf(a_w_out)
    w_kv, w_q, w_o, f_w_up, f_w_down = bf(w_kv), bf(w_q), bf(w_o), bf(f_w_up), bf(f_w_down)
    w_qt = jnp.swapaxes(w_q, 1, 2)
    B, S, _ = x_prompt.shape
    DB, DT, _ = x_sample.shape
    tile_p, tile_s = _tile(B, S, LRU_ROWS), _tile(DB, DT, LRU_ROWS)

    zero_h = jnp.zeros((N_A, B, D_RNN), F32)
    zero_c = jnp.zeros((N_A, B, CONV_W - 1, D_RNN), F32)
    zero_f = jnp.zeros((DEPTH, B, FFN_CONV_W - 1, D_FF), F32)
    win_k = cache_k_win.reshape(DB, WINDOW, HKV)
    win_v = cache_v_win.reshape(DB, WINDOW, HKV)
    bias_p = _rel_bias_table(rel_bias, WINDOW, keys_on_rows=True)
    bias_s = _rel_bias_table(rel_bias, DT, keys_on_rows=False)

    xp, xs = x_prompt, x_sample
    h_p, c_p, f_p, h_s, c_s, f_s = [], [], [], [], [], []
    for layer in range(DEPTH):
        if layer < N_A:
            i = layer
            lru_w = (a_norm[i], a_w_in, a_conv_w[i], a_conv_b[i], a_gate_r_w, a_gate_r_b[i],
                     a_gate_i_w, a_gate_i_b[i], a_lambda[i], a_w_out)
            xp, h, c = _lru_block(xp, zero_h[i], zero_c[i], *lru_w, **tile_p, layer=i)
            h_p.append(h.reshape(B, D_RNN))
            c_p.append(c)
            xs, h, c = _lru_block(xs, state_lru_h[i], state_lru_conv[i], *lru_w, **tile_s, layer=i)
            h_s.append(h.reshape(DB, D_RNN))
            c_s.append(c)
        else:
            j = layer - N_A
            xp = _attn_prompt_block(xp, kb_p, vt_p, bias_p, sinks[j], b_norm[j], w_qt, b_q[j],
                                    q_norm[j], w_o, b_o[j], nb=B, layer=j)
            xs = _attn_sample_block(xs, win_k, k_s, win_v, v_s, bias_s, sinks[j], b_norm[j], w_q,
                                    b_q[j], q_norm[j], w_o, b_o[j], nb=min(DB, ATTN_SEQS), layer=j)
        ffn_w = (f_norm[layer], f_w_up, f_conv_w[layer], f_conv_b[layer], f_w_down)
        xp, f = _conv_ffn(xp, zero_f, *ffn_w, **_tile(B, S, FFN_ROWS), layer=layer)
        f_p.append(f)
        xs, f = _conv_ffn(xs, state_ffn_conv, *ffn_w, **_tile(DB, DT, FFN_ROWS), layer=layer)
        f_s.append(f)
        if layer == N_A - 1:
            k_p, v_p, kb_p, vt_p = _shared_kv(xp, kv_norm, w_kv, b_kv, k_norm, **tile_p,
                                              key_major_copies=True)
            k_s, v_s = _shared_kv(xs, kv_norm, w_kv, b_kv, k_norm, **tile_s,
                                  key_major_copies=False)
    k_s = jnp.concatenate([win_k, k_s], axis=1)[:, -WINDOW:]
    v_s = jnp.concatenate([win_v, v_s], axis=1)[:, -WINDOW:]
    heads = lambda kv: kv.reshape(kv.shape[0], WINDOW, N_KV_HEADS, HEAD_DIM)
    return (xp, xs,
            jnp.stack(h_p), jnp.stack(c_p), jnp.stack(f_p), heads(k_p), heads(v_p),
            jnp.stack(h_s), jnp.stack(c_s), jnp.stack(f_s), heads(k_s), heads(v_s))
```
